```python
import math
import jax, jax.numpy as jnp
from jax import lax
import numpy as np

D_MODEL = 2048
BATCH = 4
SEQ = 4096
DEPTH = 2

GRID_W = 64
CTX_LEN = 256

MIX_WIDTH = D_MODEL
NA_DIM = 64
NA_HEADS = MIX_WIDTH // 4 // NA_DIM
NA_WIN_ROWS = 8
NA_WIN_COLS = 16
NA_QCOLS = 16
NA_KCOLS = NA_QCOLS + NA_WIN_COLS
DA_DIM = 64
DA_HEADS = MIX_WIDTH // 4 // (2 * DA_DIM)
DA_QBLOCK = 128
ROPE_BASE = 10000.0
SSD_INNER = MIX_WIDTH // 2
SSD_HEAD_DIM = 64
SSD_HEADS = SSD_INNER // SSD_HEAD_DIM
SSD_GROUPS = 2
SSD_HPG = SSD_HEADS // SSD_GROUPS
SSD_STATE = 128
SSD_CONV = 5
SSD_CHUNK = 128
SSD_CONV_CH = SSD_INNER + 2 * SSD_GROUPS * SSD_STATE
PEER_HEADS = 8
PEER_QDIM = 256
PEER_N_KEYS = 128
PEER_EXPERTS = PEER_N_KEYS * PEER_N_KEYS
PEER_TOPK = 16
PEER_BLOCK = 128

N_MOD = 6
EPS = 1e-6
NA_W = NA_HEADS * NA_DIM
DA_QK_W = DA_HEADS * 2 * DA_DIM
DA_V_W = DA_HEADS * 2 * DA_DIM
PROJ_SIZES = (NA_W, NA_W, NA_W, DA_QK_W, DA_QK_W, DA_V_W, SSD_INNER, SSD_CONV_CH, 2 * SSD_HEADS)
PROJ_WIDTH = sum(PROJ_SIZES)

kernel_name = 'hybrid_na_ssd_diffattn_peer_dit'

f32 = jnp.float32


def rms_norm(x, g):
    xf = x.astype(f32)
    y = xf * lax.rsqrt(jnp.mean(xf * xf, axis=-1, keepdims=True) + EPS)
    return (y * g.astype(f32)).astype(x.dtype)


def modulate(h, shift, scale):
    return h * (1 + scale) + shift


def split_proj(p):
    offs = np.cumsum(PROJ_SIZES)[:-1].tolist()
    return jnp.split(p, offs, axis=-1)


def split_heads(t, h):
    return t.reshape(t.shape[0], t.shape[1], h, -1)


def axial_rope(x, pos_r, pos_c):
    hd = x.shape[-1]
    half = hd // 2
    quarter = half // 2
    inv_freq = 1.0 / (ROPE_BASE ** (jnp.arange(quarter, dtype=f32) / quarter))

    def rot(xh, pos):
        ang = pos.astype(f32)[:, None] * inv_freq
        shape = (1, ang.shape[0]) + (1,) * (xh.ndim - 3) + (quarter,)
        cos = jnp.cos(ang).reshape(shape)
        sin = jnp.sin(ang).reshape(shape)
        x1 = xh[..., :quarter].astype(f32)
        x2 = xh[..., quarter:].astype(f32)
        return jnp.concatenate([x1 * cos - x2 * sin, x1 * sin + x2 * cos], axis=-1)

    out = jnp.concatenate([rot(x[..., :half], pos_r), rot(x[..., half:], pos_c)], axis=-1)
    return out.astype(x.dtype)


def softmax_attend(q, k, v, scale):
    s = jnp.einsum('bqhd,bkhd->bhqk', q, k).astype(f32) * scale
    p = jax.nn.softmax(s, axis=-1).astype(v.dtype)
    return jnp.einsum('bhqk,bkhe->bqhe', p, v)


def neighbourhood_attention(q, k, v, k_ctx, v_ctx, rpb):
    Bn, S, H, d = q.shape
    rows = S // GRID_W
    kh = min(NA_WIN_ROWS, rows)
    n_cb = GRID_W // NA_QCOLS
    r = np.arange(rows)
    row_start = np.clip(r - kh // 2, 0, rows - kh)
    row_idx = row_start[:, None] + np.arange(kh)[None, :] - r[:, None] + NA_WIN_ROWS - 1
    qcol = np.arange(n_cb)[:, None] * NA_QCOLS + np.arange(NA_QCOLS)[None, :]
    win_start = np.clip(qcol - NA_WIN_COLS // 2, 0, GRID_W - NA_WIN_COLS)
    blk_start = np.clip(np.arange(n_cb) * NA_QCOLS - NA_WIN_COLS // 2, 0, GRID_W - NA_KCOLS)
    kcol = blk_start[:, None] + np.arange(NA_KCOLS)[None, :]
    kc3 = kcol[:, None, :]
    col_valid = (kc3 >= win_start[:, :, None]) & (kc3 < win_start[:, :, None] + NA_WIN_COLS)
    col_idx = np.clip(kc3 - qcol[:, :, None] + NA_WIN_COLS - 1, 0, 2 * NA_WIN_COLS - 2)
    n_loc = kh * NA_KCOLS
    scale = d ** -0.5
    kg = k.reshape(Bn, rows, GRID_W, H, d)
    vg = v.reshape(Bn, rows, GRID_W, H, d)

    def row_step(args):
        q_r, rs, ridx = args
        k_rows = lax.dynamic_slice_in_dim(kg, rs, kh, axis=1)
        v_rows = lax.dynamic_slice_in_dim(vg, rs, kh, axis=1)
        k_blk = k_rows[:, :, kcol]
        v_blk = v_rows[:, :, kcol]
        s_loc = jnp.einsum('bjqhd,bkjchd->bhjqkc', q_r, k_blk).astype(f32) * scale
        bias = jnp.transpose(rpb[:, ridx][:, :, col_idx], (0, 2, 3, 1, 4)).astype(f32)
        s_loc = jnp.where(col_valid[None, None, :, :, None, :], s_loc + bias[None], -jnp.inf)
        s_ctx = jnp.einsum('bjqhd,bmhd->bhjqm', q_r, k_ctx).astype(f32) * scale
        s = jnp.concatenate([s_loc.reshape(s_loc.shape[:4] + (n_loc,)), s_ctx], axis=-1)
        p = jax.nn.softmax(s, axis=-1).astype(v.dtype)
        p_loc = p[..., :n_loc].reshape(s_loc.shape)
        p_ctx = p[..., n_loc:]
        return (jnp.einsum('bhjqkc,bkjchd->bjqhd', p_loc, v_blk)
                + jnp.einsum('bhjqm,bmhd->bjqhd', p_ctx, v_ctx))

    q_rows = jnp.moveaxis(q.reshape(Bn, rows, n_cb, NA_QCOLS, H, d), 1, 0)
    out = lax.map(row_step, (q_rows, jnp.asarray(row_start, jnp.int32), jnp.asarray(row_idx, jnp.int32)))
    return jnp.moveaxis(out, 0, 1).reshape(Bn, S, H * d)


def diff_attend(q, k, v, lam, lam_init, sub_g):
    s = jnp.einsum('bqhmd,bkhmd->bhmqk', q, k).astype(f32) * (DA_DIM ** -0.5)
    p = jax.nn.softmax(s, axis=-1)
    a = (p[:, :, 0] - lam * p[:, :, 1]).astype(v.dtype)
    o = jnp.einsum('bhqk,bkhe->bqhe', a, v)
    return rms_norm(o, sub_g) * (1.0 - lam_init)


def segsum(a):
    T = a.shape[-1]
    cs = jnp.cumsum(a, axis=-1)
    diff = cs[..., :, None] - cs[..., None, :]
    return jnp.where(jnp.tril(jnp.ones((T, T), dtype=bool)), diff, -jnp.inf)


def ssd_scan(xs, dt, A, Bm, Cm, d_skip):
    b, L, G, R, P = xs.shape
    N = Bm.shape[-1]
    Q = SSD_CHUNK
    nc = L // Q
    xf = xs.astype(f32)
    xdt = (xf * dt[..., None]).reshape(b, nc, Q, G, R, P)
    Bc = Bm.astype(f32).reshape(b, nc, Q, G, N)
    Cc = Cm.astype(f32).reshape(b, nc, Q, G, N)
    a = jnp.moveaxis((dt * A).reshape(b, nc, Q, G, R), 2, -1)
    a_cs = jnp.cumsum(a, axis=-1)
    cb = jnp.einsum('bclgn,bcsgn->bcgls', Cc, Bc)
    w = cb[:, :, :, None] * jnp.exp(segsum(a))
    y_diag = jnp.einsum('bcgrls,bcsgrp->bclgrp', w, xdt)
    decay_states = jnp.exp(a_cs[..., -1:] - a_cs)
    states = jnp.einsum('bclgn,bcgrl,bclgrp->bcgrpn', Bc, decay_states, xdt)
    states = jnp.concatenate([jnp.zeros_like(states[:, :1]), states], axis=1)
    chunk_a = jnp.pad(a_cs[..., -1], ((0, 0), (1, 0), (0, 0), (0, 0)))
    decay_chunk = jnp.exp(segsum(jnp.moveaxis(chunk_a, 1, -1)))
    states = jnp.einsum('bgrzc,bcgrpn->bzgrpn', decay_chunk, states)[:, :-1]
    y_off = jnp.einsum('bclgn,bcgrpn,bcgrl->bclgrp', Cc, states, jnp.exp(a_cs))
    return (y_diag + y_off).reshape(b, L, G, R, P) + xf * d_skip[..., None]


def dw_conv_centered(u, w, bias):
    K, C = w.shape
    out = lax.conv_general_dilated(u, w[:, None, :].astype(u.dtype), window_strides=(1,),
                                   padding=[(K // 2, K // 2)], dimension_numbers=('NWC', 'WIO', 'NWC'),
                                   feature_group_count=C)
    return out + bias


def seg_flip(u, n_ctx):
    return jnp.concatenate([jnp.flip(u[:, :n_ctx], axis=1), jnp.flip(u[:, n_ctx:], axis=1)], axis=1)


def ssd_mixer(z_l, xbc_l, dt_l, z_c, xbc_c, dt_c, conv_w, conv_b, dt_bias, a_log, d_skip, norm_g):
    n_ctx = z_c.shape[1]
    xbc = jnp.concatenate([jax.nn.silu(dw_conv_centered(xbc_c, conv_w, conv_b)),
                           jax.nn.silu(dw_conv_centered(xbc_l, conv_w, conv_b))], axis=1)
    dtr = jnp.concatenate([dt_c, dt_l], axis=1)
    b, L = xbc.shape[:2]
    dtr = dtr.reshape(b, L, 2, SSD_HEADS)
    gn = SSD_GROUPS * SSD_STATE

    def run(u, dt_raw, direction):
        xs = u[..., :SSD_INNER].reshape(b, L, SSD_GROUPS, SSD_HPG, SSD_HEAD_DIM)
        Bm = u[..., SSD_INNER:SSD_INNER + gn].reshape(b, L, SSD_GROUPS, SSD_STATE)
        Cm = u[..., SSD_INNER + gn:].reshape(b, L, SSD_GROUPS, SSD_STATE)
        dt = jax.nn.softplus(dt_raw.astype(f32) + dt_bias[direction].astype(f32))
        A = -jnp.exp(a_log[direction].astype(f32)).reshape(SSD_GROUPS, SSD_HPG)
        return ssd_scan(xs, dt.reshape(b, L, SSD_GROUPS, SSD_HPG), A, Bm, Cm,
                        d_skip[direction].astype(f32).reshape(SSD_GROUPS, SSD_HPG))

    y_fwd = run(xbc, dtr[:, :, 0], 0)
    y_bwd = seg_flip(run(seg_flip(xbc, n_ctx), seg_flip(dtr[:, :, 1], n_ctx), 1), n_ctx)
    z = jnp.concatenate([z_c, z_l], axis=1)
    gated = (y_fwd + y_bwd).reshape(b, L, SSD_INNER) * jax.nn.silu(z.astype(f32))
    gg = gated.reshape(b, L, SSD_GROUPS, SSD_INNER // SSD_GROUPS)
    gg = gg * lax.rsqrt(jnp.mean(gg * gg, axis=-1, keepdims=True) + EPS)
    out = (gg.reshape(b, L, SSD_INNER) * norm_g.astype(f32)).astype(z_l.dtype)
    return out[:, :n_ctx], out[:, n_ctx:]


def token_mixers(p_lat, p_ctx, pos_r, pos_c, layer, rpb, conv_w, conv_b, dt_bias, a_log, d_skip,
                 ssd_g, lam_p, sub_g, need_ctx):
    Bn, S, _ = p_lat.shape
    n_ctx = p_ctx.shape[1]
    nq_l, nk_l, nv_l, dq_l, dk_l, dv_l, z_l, xbc_l, dt_l = split_proj(p_lat)
    nq_c, nk_c, nv_c, dq_c, dk_c, dv_c, z_c, xbc_c, dt_c = split_proj(p_ctx)
    na_k_c = split_heads(nk_c, NA_HEADS)
    na_v_c = split_heads(nv_c, NA_HEADS)
    na_lat = neighbourhood_attention(split_heads(nq_l, NA_HEADS), split_heads(nk_l, NA_HEADS),
                                     split_heads(nv_l, NA_HEADS), na_k_c, na_v_c, rpb)
    lam_init = 0.8 - 0.6 * math.exp(-0.3 * layer)
    lf = lam_p.astype(f32)
    lam = jnp.exp(jnp.sum(lf[0] * lf[1])) - jnp.exp(jnp.sum(lf[2] * lf[3])) + lam_init
    da_q_l = axial_rope(dq_l.reshape(Bn, S, DA_HEADS, 2, DA_DIM), pos_r, pos_c)
    da_k_l = axial_rope(dk_l.reshape(Bn, S, DA_HEADS, 2, DA_DIM), pos_r, pos_c)
    da_k_c = dk_c.reshape(Bn, n_ctx, DA_HEADS, 2, DA_DIM)
    da_v_c = split_heads(dv_c, DA_HEADS)
    k_all = jnp.concatenate([da_k_l, da_k_c], axis=1)
    v_all = jnp.concatenate([split_heads(dv_l, DA_HEADS), da_v_c], axis=1)
    q_blocks = jnp.moveaxis(da_q_l.reshape(Bn, S // DA_QBLOCK, DA_QBLOCK, DA_HEADS, 2, DA_DIM), 1, 0)
    da_lat = lax.map(lambda qq: diff_attend(qq, k_all, v_all, lam, lam_init, sub_g), q_blocks)
    da_lat = jnp.moveaxis(da_lat, 0, 1).reshape(Bn, S, DA_V_W)
    ssd_c, ssd_l = ssd_mixer(z_l, xbc_l, dt_l, z_c, xbc_c, dt_c, conv_w, conv_b, dt_bias, a_log,
                             d_skip, ssd_g)
    mix_lat = jnp.concatenate([na_lat, da_lat, ssd_l], axis=-1)
    if not need_ctx:
        return mix_lat, None
    na_ctx = softmax_attend(split_heads(nq_c, NA_HEADS), na_k_c, na_v_c, NA_DIM ** -0.5)
    da_ctx = diff_attend(dq_c.reshape(Bn, n_ctx, DA_HEADS, 2, DA_DIM), da_k_c, da_v_c, lam, lam_init, sub_g)
    mix_ctx = jnp.concatenate([na_ctx.reshape(Bn, n_ctx, NA_W), da_ctx.reshape(Bn, n_ctx, DA_V_W), ssd_c],
                              axis=-1)
    return mix_lat, mix_ctx


def peer_ffn(h, w_q, sub_keys, u, v):
    T, D = h.shape

    def block(hh):
        q = (hh @ w_q).reshape(PEER_BLOCK, PEER_HEADS, 2, PEER_QDIM // 2)
        s = jnp.einsum('thpk,hpnk->thpn', q, sub_keys).astype(f32)
        s1, i1 = lax.top_k(s[:, :, 0], PEER_TOPK)
        s2, i2 = lax.top_k(s[:, :, 1], PEER_TOPK)
        cand_s = (s1[..., :, None] + s2[..., None, :]).reshape(PEER_BLOCK, PEER_HEADS, PEER_TOPK * PEER_TOPK)
        cand_i = (i1[..., :, None] * PEER_N_KEYS + i2[..., None, :]).reshape(PEER_BLOCK, PEER_HEADS,
                                                                             PEER_TOPK * PEER_TOPK)
        top_s, pos = lax.top_k(cand_s, PEER_TOPK)
        idx = jnp.take_along_axis(cand_i, pos, axis=-1)
        gate = jax.nn.softmax(top_s, axis=-1)
        ue = jnp.take(u, idx, axis=0)
        ve = jnp.take(v, idx, axis=0)
        act = jax.nn.gelu(jnp.einsum('td,thkd->thk', hh, ue).astype(f32))
        return jnp.einsum('thk,thkd->td', (gate * act).astype(hh.dtype), ve)

    return lax.map(block, h.reshape(T // PEER_BLOCK, PEER_BLOCK, D)).reshape(T, D)


def setup_inputs(seed: int = 0) -> dict:
    key = jax.random.key(seed)
    ks = jax.random.split(key, 22)
    D = D_MODEL

    def nrm(k, shape, s):
        return jax.random.normal(k, shape, f32) * s

    dt0 = jnp.exp(jax.random.uniform(ks[12], (DEPTH, 2, SSD_HEADS), f32,
                                     minval=math.log(1e-3), maxval=math.log(1e-1)))
    return {
        'x': nrm(ks[0], (BATCH, SEQ, D), 1.0),
        'c': nrm(ks[1], (BATCH, D), 1.0),
        'ctx': nrm(ks[2], (BATCH, CTX_LEN, D), 1.0),
        'c_ctx': nrm(ks[3], (D,), 1.0),
        'ada_w': nrm(ks[4], (DEPTH, D, N_MOD * D), 0.2 * D ** -0.5),
        'ada_b': nrm(ks[5], (DEPTH, N_MOD * D), 0.01),
        'norm_g': 1.0 + nrm(ks[6], (DEPTH, 4, D), 0.01),
        'w_in': nrm(ks[7], (DEPTH, D, PROJ_WIDTH), D ** -0.5),
        'w_out': nrm(ks[8], (DEPTH, MIX_WIDTH, D), MIX_WIDTH ** -0.5),
        'na_rpb': nrm(ks[9], (DEPTH, NA_HEADS, 2 * NA_WIN_ROWS - 1, 2 * NA_WIN_COLS - 1), 0.1),
        'ssd_conv_w': nrm(ks[10], (DEPTH, SSD_CONV, SSD_CONV_CH), SSD_CONV ** -0.5),
        'ssd_conv_b': nrm(ks[11], (DEPTH, SSD_CONV_CH), 0.01),
        'ssd_dt_bias': dt0 + jnp.log(-jnp.expm1(-dt0)),
        'ssd_a_log': jnp.log(jax.random.uniform(ks[13], (DEPTH, 2, SSD_HEADS), f32, minval=1.0, maxval=16.0)),
        'ssd_d': 1.0 + nrm(ks[14], (DEPTH, 2, SSD_HEADS), 0.1),
        'ssd_norm_g': 1.0 + nrm(ks[15], (DEPTH, SSD_INNER), 0.01),
        'da_lambda': nrm(ks[16], (DEPTH, 4, DA_DIM), 0.1),
        'da_sub_g': 1.0 + nrm(ks[17], (DEPTH, 2 * DA_DIM), 0.01),
        'peer_w_q': nrm(ks[18], (DEPTH, D, PEER_HEADS * PEER_QDIM), D ** -0.5),
        'peer_sub_keys': nrm(ks[19], (DEPTH, PEER_HEADS, 2, PEER_N_KEYS, PEER_QDIM // 2), (PEER_QDIM // 2) ** -0.5),
        'peer_u': nrm(ks[20], (DEPTH, PEER_EXPERTS, D), D ** -0.5),
        'peer_v': nrm(ks[21], (DEPTH, PEER_EXPERTS, D), PEER_HEADS ** -0.5),
    }


def reference(x, c, ctx, c_ctx, ada_w, ada_b, norm_g, w_in, w_out, na_rpb, ssd_conv_w, ssd_conv_b,
              ssd_dt_bias, ssd_a_log, ssd_d, ssd_norm_g, da_lambda, da_sub_g, peer_w_q, peer_sub_keys,
              peer_u, peer_v):
    Bn, S, D = x.shape
    n_ctx = ctx.shape[1]
    t = jnp.arange(S)
    pos_r = t // GRID_W
    pos_c = t % GRID_W
    sc_lat = jax.nn.silu(c)
    sc_ctx = jax.nn.silu(c_ctx)
    for layer in range(DEPTH):
        need_ctx = layer < DEPTH - 1
        mod_l = (sc_lat @ ada_w[layer] + ada_b[layer]).reshape(Bn, N_MOD, 1, D)
        mod_c = (sc_ctx @ ada_w[layer] + ada_b[layer]).reshape(N_MOD, D)
        g = norm_g[layer]
        h_l = modulate(rms_norm(x, g[0]), mod_l[:, 0], mod_l[:, 1])
        h_c = modulate(rms_norm(ctx, g[0]), mod_c[0], mod_c[1])
        mix_l, mix_c = token_mixers(h_l @ w_in[layer], h_c @ w_in[layer], pos_r, pos_c, layer,
                                    na_rpb[layer], ssd_conv_w[layer], ssd_conv_b[layer], ssd_dt_bias[layer],
                                    ssd_a_log[layer], ssd_d[layer], ssd_norm_g[layer], da_lambda[layer],
                                    da_sub_g[layer], need_ctx)
        x = x + mod_l[:, 2] * rms_norm(mix_l @ w_out[layer], g[1])
        h_l = modulate(rms_norm(x, g[2]), mod_l[:, 3], mod_l[:, 4])
        f_l = peer_ffn(h_l.reshape(Bn * S, D), peer_w_q[layer], peer_sub_keys[layer], peer_u[layer],
                       peer_v[layer]).reshape(Bn, S, D)
        x = x + mod_l[:, 5] * rms_norm(f_l, g[3])
        if need_ctx:
            ctx = ctx + mod_c[2] * rms_norm(mix_c @ w_out[layer], g[1])
            h_c = modulate(rms_norm(ctx, g[2]), mod_c[3], mod_c[4])
            f_c = peer_ffn(h_c.reshape(Bn * n_ctx, D), peer_w_q[layer], peer_sub_keys[layer], peer_u[layer],
                           peer_v[layer]).reshape(Bn, n_ctx, D)
            ctx = ctx + mod_c[5] * rms_norm(f_c, g[3])
    return x
```

```python
import functools
import math

import numpy as np
import jax
import jax.numpy as jnp
from jax import lax
from jax.experimental import pallas as pl
from jax.experimental.pallas import tpu as pltpu

F32 = jnp.float32
BF16 = jnp.bfloat16
EPS = 1e-6
NEG_INF = float("-inf")

VMEM_LIMIT_BYTES = 56 * 1024 * 1024
LANES = 128

GRID_W = 64
NA_DIM = 64
NA_HEADS = 8
NA_WIN_ROWS = 8
NA_WIN_COLS = 16
DA_DIM = 64
DA_HEADS = 4
ROPE_BASE = 10000.0
SSD_INNER = 1024
SSD_HEAD_DIM = 64
SSD_HEADS = 16
SSD_GROUPS = 2
SSD_STATE = 128
SSD_CONV = 5
SSD_CHUNK = 128
SSD_XBC = SSD_INNER + 2 * SSD_GROUPS * SSD_STATE
SSD_IN_W = SSD_INNER + SSD_XBC + LANES
PEER_HEADS = 8
PEER_QDIM = 256
PEER_N_KEYS = 128
PEER_TOPK = 16
N_MOD = 6


def _params(*sem):
    return pltpu.CompilerParams(dimension_semantics=sem, vmem_limit_bytes=VMEM_LIMIT_BYTES)


def _nt_dot(a, b):
    return lax.dot_general(a, b, (((1,), (1,)), ((), ())), preferred_element_type=F32)


def _lane_lt64(shape):
    return lax.broadcasted_iota(jnp.int32, shape, len(shape) - 1) < (LANES // 2)


def _ada_body(c_ref, w_ref, b_ref, o_ref):
    c = c_ref[...]
    sc = (c * jax.nn.sigmoid(c)).astype(BF16)
    o_ref[...] = jnp.dot(sc, w_ref[...].astype(BF16), preferred_element_type=F32) + b_ref[...]


def _ada_call(cond, w, b, tn=768):
    m, d = cond.shape
    n = w.shape[1]
    return pl.pallas_call(
        _ada_body,
        out_shape=jax.ShapeDtypeStruct((m, n), F32),
        grid=(n // tn,),
        in_specs=[pl.BlockSpec((m, d), lambda j: (0, 0)),
                  pl.BlockSpec((d, tn), lambda j: (0, j)),
                  pl.BlockSpec((1, tn), lambda j: (0, j))],
        out_specs=pl.BlockSpec((m, tn), lambda j: (0, j)),
        compiler_params=_params("arbitrary"),
        name="ada_mod",
    )(cond, w, b.reshape(1, n))


def _rms(x, g):
    return x * lax.rsqrt(jnp.mean(x * x, axis=-1, keepdims=True) + EPS) * g


def _norm_body(has_res, has_pre, *refs):
    refs = list(refs)
    x = refs.pop(0)[...]
    if has_res:
        f_ref, gate_ref, gpost_ref = refs.pop(0), refs.pop(0), refs.pop(0)
        x = x + gate_ref[0] * _rms(f_ref[...], gpost_ref[...])
    if has_pre:
        gpre_ref, shift_ref, scale_ref = refs.pop(0), refs.pop(0), refs.pop(0)
    if has_res:
        refs.pop(0)[...] = x
    if has_pre:
        h = _rms(x, gpre_ref[...]) * (1.0 + scale_ref[0]) + shift_ref[0]
        refs.pop(0)[...] = h.astype(BF16)


def _norm_call(x, sel, *, res=None, pre=None, tm=256):
    m, d = x.shape
    row = pl.BlockSpec((tm, d), lambda i: (i, 0))
    vec = pl.BlockSpec((1, d), lambda i: (0, 0))
    tab = pl.BlockSpec((1, 1, d), lambda i: (sel(i), 0, 0))
    args, specs, outs, ospecs = [x], [row], [], []
    if res is not None:
        args += list(res)
        specs += [row, tab, vec]
        outs.append(jax.ShapeDtypeStruct((m, d), F32))
        ospecs.append(row)
    if pre is not None:
        args += list(pre)
        specs += [vec, tab, tab]
        outs.append(jax.ShapeDtypeStruct((m, d), BF16))
        ospecs.append(row)
    res_out = pl.pallas_call(
        functools.partial(_norm_body, res is not None, pre is not None),
        out_shape=tuple(outs), grid=(m // tm,), in_specs=specs, out_specs=tuple(ospecs),
        compiler_params=_params("parallel"), name="norm_mod",
    )(*args)
    res_out = list(res_out)
    x_new = res_out.pop(0) if res is not None else None
    h = res_out.pop(0) if pre is not None else None
    return x_new, h


def _rope_partner(x):
    lane = lax.broadcasted_iota(jnp.int32, x.shape, 1)
    first = (lane % 32) < 16
    return jnp.where(first, pltpu.roll(x, LANES - 16, 1), pltpu.roll(x, 16, 1))


def _mm_body(epi, h_ref, w_ref, *rest):
    acc = jnp.dot(h_ref[...], w_ref[...], preferred_element_type=F32)
    if epi == "plain":
        (o_ref,) = rest
    elif epi == "scale":
        cs_ref, o_ref = rest
        acc = acc * cs_ref[...]
    else:
        cos_ref, sin_ref, cs_ref, o_ref = rest
        cos = cos_ref[...]
        sin = sin_ref[...]
        n = acc.shape[1]
        blocks = []
        for c in range(n // LANES):
            xb = acc[:, c * LANES:(c + 1) * LANES]
            blocks.append(xb * cos + _rope_partner(xb) * sin)
        acc = jnp.concatenate(blocks, axis=1) * cs_ref[...]
    o_ref[...] = acc.astype(o_ref.dtype)


def _mm_call(h, w, out_dtype, *, colscale=None, rope=None, tm=512, tn=None, name="proj"):
    m, k = h.shape
    n = w.shape[1]
    tn = n if tn is None else tn
    args = [h, w]
    specs = [pl.BlockSpec((tm, k), lambda i, j: (i, 0)),
             pl.BlockSpec((k, tn), lambda i, j: (0, j))]
    epi = "plain"
    if rope is not None:
        cos_t, sin_t = rope
        nt = cos_t.shape[0] // tm
        args += [cos_t, sin_t]
        specs += [pl.BlockSpec((tm, LANES), lambda i, j: (i % nt, 0))] * 2
        epi = "rope"
    if colscale is not None:
        args.append(colscale.reshape(1, n))
        specs.append(pl.BlockSpec((1, tn), lambda i, j: (0, j)))
        epi = "scale" if epi == "plain" else epi
    return pl.pallas_call(
        functools.partial(_mm_body, epi),
        out_shape=jax.ShapeDtypeStruct((m, n), out_dtype),
        grid=(m // tm, n // tn), in_specs=specs,
        out_specs=pl.BlockSpec((tm, tn), lambda i, j: (i, j)),
        compiler_params=_params("parallel", "arbitrary"), name=name,
    )(*args)


def _rope_tables(seq):
    quarter = DA_DIM // 4
    inv_freq = 1.0 / (ROPE_BASE ** (jnp.arange(quarter, dtype=F32) / quarter))
    t = jnp.arange(seq)
    ang_r = (t // GRID_W).astype(F32)[:, None] * inv_freq
    ang_c = (t % GRID_W).astype(F32)[:, None] * inv_freq

    def unit(ang):
        return (jnp.concatenate([jnp.cos(ang), jnp.cos(ang)], -1),
                jnp.concatenate([-jnp.sin(ang), jnp.sin(ang)], -1))

    cr, sr = unit(ang_r)
    cc, sc = unit(ang_c)
    cos64 = jnp.concatenate([cr, cc], -1)
    sin64 = jnp.concatenate([sr, sc], -1)
    return jnp.tile(cos64, (1, 2)), jnp.tile(sin64, (1, 2))


def _stack_heads(q2):
    lo = _lane_lt64(q2.shape)
    zero = jnp.zeros_like(q2)
    return jnp.concatenate([jnp.where(lo, q2, zero), jnp.where(lo, zero, q2)], axis=0)


def _unstack_heads(o):
    r = o.shape[0] // 2
    return jnp.where(_lane_lt64((r, o.shape[1])), o[:r], o[r:])


def _na_body(rows, q_ref, k_ref, v_ref, kc_ref, vc_ref, bias_ref, o_ref):
    kc = kc_ref[...]
    vc = vc_ref[...]
    kh = NA_WIN_ROWS

    def row(r, carry):
        rs = jnp.clip(r - kh // 2, 0, rows - kh)
        q0 = pl.multiple_of(r * GRID_W, GRID_W)
        k0 = pl.multiple_of(rs * GRID_W, GRID_W)
        qs = _stack_heads(q_ref[pl.ds(q0, GRID_W), :])
        ks = k_ref[pl.ds(k0, kh * GRID_W), :]
        vs = v_ref[pl.ds(k0, kh * GRID_W), :]
        s_loc = _nt_dot(qs, ks) + bias_ref[0, r - rs]
        s_ctx = _nt_dot(qs, kc)
        m = jnp.maximum(jnp.max(s_loc, axis=-1, keepdims=True), jnp.max(s_ctx, axis=-1, keepdims=True))
        p_loc = jnp.exp(s_loc - m)
        p_ctx = jnp.exp(s_ctx - m)
        l = jnp.sum(p_loc, axis=-1, keepdims=True) + jnp.sum(p_ctx, axis=-1, keepdims=True)
        o = (jnp.dot(p_loc.astype(BF16), vs, preferred_element_type=F32)
             + jnp.dot(p_ctx.astype(BF16), vc, preferred_element_type=F32)) / l
        o_ref[pl.ds(q0, GRID_W), :] = _unstack_heads(o).astype(o_ref.dtype)
        return carry

    lax.fori_loop(0, rows, row, 0)


def _na_bias(rpb):
    kh, w = NA_WIN_ROWS, GRID_W
    c = np.arange(w)
    wstart = np.clip(c - NA_WIN_COLS // 2, 0, w - NA_WIN_COLS)
    kc = np.arange(w)
    valid = (kc[None, :] >= wstart[:, None]) & (kc[None, :] < wstart[:, None] + NA_WIN_COLS)
    cidx = np.clip(kc[None, :] - c[:, None] + NA_WIN_COLS - 1, 0, 2 * NA_WIN_COLS - 2)
    typ = np.arange(kh)
    ridx = np.arange(kh)[None, :] - typ[:, None] + NA_WIN_ROWS - 1
    b = rpb[:, ridx][:, :, :, cidx]
    b = jnp.where(valid[None, None, None], b, NEG_INF)
    b = jnp.transpose(b, (0, 1, 3, 2, 4)).reshape(NA_HEADS, kh, w, kh * w)
    b = b.reshape(NA_HEADS // 2, 2, kh, w, kh * w)
    return jnp.transpose(b, (0, 2, 1, 3, 4)).reshape(NA_HEADS // 2, kh, 2 * w, kh * w).astype(F32)


def _na_call(a_lat, a_ctx, bias, batch, seq, n_ctx):
    rows = seq // GRID_W
    pairs = NA_HEADS // 2
    lat = lambda c0: pl.BlockSpec((seq, LANES), lambda b, p: (b, c0 + p))
    ctx = lambda c0: pl.BlockSpec((n_ctx, LANES), lambda b, p: (b, c0 + p))
    return pl.pallas_call(
        functools.partial(_na_body, rows),
        out_shape=jax.ShapeDtypeStruct((batch * seq, NA_HEADS * NA_DIM), BF16),
        grid=(batch, pairs),
        in_specs=[lat(0), lat(pairs), lat(2 * pairs), ctx(pairs), ctx(2 * pairs),
                  pl.BlockSpec((1,) + bias.shape[1:], lambda b, p: (p, 0, 0, 0))],
        out_specs=pl.BlockSpec((seq, LANES), lambda b, p: (b, p)),
        compiler_params=_params("parallel", "parallel"), name="na_attn",
    )(a_lat, a_lat, a_lat, a_ctx, a_ctx, bias)


def _da_lambda(lam_ref, lam_init):
    lf = lam_ref[...]
    a = jnp.sum(lf[0:1] * lf[1:2], axis=-1, keepdims=True)
    b = jnp.sum(lf[2:3] * lf[3:4], axis=-1, keepdims=True)
    return jnp.exp(a) - jnp.exp(b) + lam_init


def _da_finish(acc, l, lam, g):
    tq = acc.shape[0] // 2
    o = acc / l
    d = o[:tq] - lam * o[tq:]
    return d * lax.rsqrt(jnp.mean(d * d, axis=-1, keepdims=True) + EPS) * g


def _flash_step(qs, k, v, m, l, acc):
    s = _nt_dot(qs, k)
    m_new = jnp.maximum(m, jnp.max(s, axis=-1, keepdims=True))
    alpha = jnp.exp(m - m_new)
    p = jnp.exp(s - m_new)
    l = alpha * l + jnp.sum(p, axis=-1, keepdims=True)
    acc = alpha * acc + jnp.dot(p.astype(BF16), v, preferred_element_type=F32)
    return m_new, l, acc


def _da_body(lam_init, n_chunks, tk, q_ref, k_ref, v_ref, kc_ref, vc_ref, lam_ref, g_ref, o_ref):
    qs = _stack_heads(q_ref[...])
    r = qs.shape[0]
    init = (jnp.full((r, 1), NEG_INF, F32), jnp.zeros((r, 1), F32), jnp.zeros((r, LANES), F32))

    def chunk(c, carry):
        k0 = pl.multiple_of(c * tk, tk)
        return _flash_step(qs, k_ref[pl.ds(k0, tk), :], v_ref[pl.ds(k0, tk), :], *carry)

    carry = lax.fori_loop(0, n_chunks, chunk, init)
    m, l, acc = _flash_step(qs, kc_ref[...], vc_ref[...], *carry)
    o_ref[...] = _da_finish(acc, l, _da_lambda(lam_ref, lam_init), g_ref[...]).astype(o_ref.dtype)


def _da_call(qk_lat, qk_ctx, a_lat, a_ctx, lam_p, g_scaled, lam_init, batch, seq, n_ctx, tq=256, tk=512):
    nq = seq // tq
    h = DA_HEADS
    return pl.pallas_call(
        functools.partial(_da_body, lam_init, seq // tk, tk),
        out_shape=jax.ShapeDtypeStruct((batch * seq, h * 2 * DA_DIM), BF16),
        grid=(batch, h, nq),
        in_specs=[pl.BlockSpec((tq, LANES), lambda b, hh, i: (b * nq + i, hh)),
                  pl.BlockSpec((seq, LANES), lambda b, hh, i: (b, h + hh)),
                  pl.BlockSpec((seq, LANES), lambda b, hh, i: (b, 3 * h + hh)),
                  pl.BlockSpec((n_ctx, LANES), lambda b, hh, i: (b, h + hh)),
                  pl.BlockSpec((n_ctx, LANES), lambda b, hh, i: (b, 3 * h + hh)),
                  pl.BlockSpec(lam_p.shape, lambda b, hh, i: (0, 0)),
                  pl.BlockSpec((1, LANES), lambda b, hh, i: (0, 0))],
        out_specs=pl.BlockSpec((tq, LANES), lambda b, hh, i: (b * nq + i, hh)),
        compiler_params=_params("parallel", "parallel", "arbitrary"), name="da_attn",
    )(qk_lat, qk_lat, a_lat, qk_ctx, a_ctx, lam_p, g_scaled)


def _ctx_attn_body(lam_init, a_ref, qk_ref, lam_ref, g_ref, na_ref, da_ref):
    pairs = NA_HEADS // 2
    for p in range(pairs):
        qs = _stack_heads(a_ref[:, p * LANES:(p + 1) * LANES])
        k = a_ref[:, (pairs + p) * LANES:(pairs + p + 1) * LANES]
        v = a_ref[:, (2 * pairs + p) * LANES:(2 * pairs + p + 1) * LANES]
        s = _nt_dot(qs, k)
        e = jnp.exp(s - jnp.max(s, axis=-1, keepdims=True))
        o = jnp.dot(e.astype(BF16), v, preferred_element_type=F32) / jnp.sum(e, axis=-1, keepdims=True)
        na_ref[:, p * LANES:(p + 1) * LANES] = _unstack_heads(o).astype(na_ref.dtype)
    lam = _da_lambda(lam_ref, lam_init)
    for hh in range(DA_HEADS):
        qs = _stack_heads(qk_ref[:, hh * LANES:(hh + 1) * LANES])
        k = qk_ref[:, (DA_HEADS + hh) * LANES:(DA_HEADS + hh + 1) * LANES]
        v = a_ref[:, (3 * pairs + hh) * LANES:(3 * pairs + hh + 1) * LANES]
        r = qs.shape[0]
        init = (jnp.full((r, 1), NEG_INF, F32), jnp.zeros((r, 1), F32), jnp.zeros((r, LANES), F32))
        m, l, acc = _flash_step(qs, k, v, *init)
        da_ref[:, hh * LANES:(hh + 1) * LANES] = _da_finish(acc, l, lam, g_ref[...]).astype(da_ref.dtype)


def _ctx_attn_call(a_ctx, qk_ctx, lam_p, g_scaled, lam_init, batch, n_ctx):
    return pl.pallas_call(
        functools.partial(_ctx_attn_body, lam_init),
        out_shape=(jax.ShapeDtypeStruct((batch * n_ctx, NA_HEADS * NA_DIM), BF16),
                   jax.ShapeDtypeStruct((batch * n_ctx, DA_HEADS * 2 * DA_DIM), BF16)),
        grid=(batch,),
        in_specs=[pl.BlockSpec((n_ctx, a_ctx.shape[1]), lambda b: (b, 0)),
                  pl.BlockSpec((n_ctx, qk_ctx.shape[1]), lambda b: (b, 0)),
                  pl.BlockSpec(lam_p.shape, lambda b: (0, 0)),
                  pl.BlockSpec((1, LANES), lambda b: (0, 0))],
        out_specs=(pl.BlockSpec((n_ctx, NA_HEADS * NA_DIM), lambda b: (b, 0)),
                   pl.BlockSpec((n_ctx, DA_HEADS * 2 * DA_DIM), lambda b: (b, 0))),
        compiler_params=_params("parallel"), name="ctx_attn",
    )(a_ctx, qk_ctx, lam_p, g_scaled)


def _conv_seg(u, w, bias):
    n = u.shape[0]
    t = lax.broadcasted_iota(jnp.int32, u.shape, 0)
    acc = jnp.zeros_like(u) + bias
    for kk in range(SSD_CONV):
        d = kk - SSD_CONV // 2
        if d == 0:
            sh = u
        else:
            sh = pltpu.roll(u, (-d) % n, 0)
            sh = jnp.where((t + d >= 0) & (t + d < n), sh, 0.0)
        acc = acc + sh * w[kk:kk + 1, :]
    return acc * jax.nn.sigmoid(acc)


def _ssd_prep_body(seq, n_zc, lat_ref, ctx_ref, w_ref, b_ref, o_ref):
    j = pl.program_id(1)
    is_conv = (j >= n_zc) & (j < n_zc + SSD_XBC // LANES)

    @pl.when(is_conv)
    def _():
        w = w_ref[...]
        bias = b_ref[...]
        o_ref[0, :seq, :] = _conv_seg(lat_ref[...], w, bias)
        o_ref[0, seq:, :] = _conv_seg(ctx_ref[...], w, bias)

    @pl.when(jnp.logical_not(is_conv))
    def _():
        o_ref[0, :seq, :] = lat_ref[...]
        o_ref[0, seq:, :] = ctx_ref[...]


def _ssd_prep_call(s_lat, s_ctx, conv_w, conv_b, batch, seq, n_ctx):
    n_zc = SSD_INNER // LANES
    ncol = SSD_IN_W // LANES
    n_cw = SSD_XBC // LANES
    wcol = lambda b, j: (0, jnp.clip(j - n_zc, 0, n_cw - 1))
    return pl.pallas_call(
        functools.partial(_ssd_prep_body, seq, n_zc),
        out_shape=jax.ShapeDtypeStruct((batch, seq + n_ctx, SSD_IN_W), F32),
        grid=(batch, ncol),
        in_specs=[pl.BlockSpec((seq, LANES), lambda b, j: (b, j)),
                  pl.BlockSpec((n_ctx, LANES), lambda b, j: (b, j)),
                  pl.BlockSpec((SSD_CONV, LANES), wcol),
                  pl.BlockSpec((1, LANES), wcol)],
        out_specs=pl.BlockSpec((1, seq + n_ctx, LANES), lambda b, j: (b, 0, j)),
        compiler_params=_params("parallel", "parallel"), name="ssd_prep",
    )(s_lat, s_ctx, conv_w, conv_b.reshape(1, -1))


def _softplus(x):
    return jnp.maximum(x, 0.0) + jnp.log1p(jnp.exp(-jnp.abs(x)))


def _ssd_chunk(direction, xs_ref, bc_ref, dt_ref, dtt_ref, pr_ref, pc_ref, dexp_ref, s_ref):
    q = SSD_CHUNK
    nh = SSD_HEADS
    hi = lax.Precision.HIGHEST
    d0 = direction * nh
    dt = _softplus(dt_ref[0][:, d0:d0 + nh] + pr_ref[0:1, :])
    a = dt * (-jnp.exp(pr_ref[1:2, :]))
    dtt = _softplus(dtt_ref[0][d0:d0 + nh, :] + pc_ref[:, 0:1])
    at = dtt * (-jnp.exp(pc_ref[:, 1:2]))
    ii = lax.broadcasted_iota(jnp.int32, (q, q), 0)
    jj = lax.broadcasted_iota(jnp.int32, (q, q), 1)
    tri = (jj <= ii) if direction == 0 else (jj >= ii)
    tri_f = tri.astype(F32)
    cs = jnp.dot(tri_f, a, precision=hi, preferred_element_type=F32)
    cst = lax.dot_general(at, tri_f, (((1,), (1,)), ((), ())), precision=hi,
                          preferred_element_type=F32)
    edge = q - 1 if direction == 0 else 0
    tot_t = cst[:, edge:edge + 1]
    dst = jnp.exp(tot_t - cst)
    ecs = jnp.exp(cs)
    etot = jnp.exp(cs[edge:edge + 1, :])

    lo = _lane_lt64((q, LANES))
    groups = []
    for g in range(SSD_GROUPS):
        b_g = bc_ref[0][:, g * SSD_STATE:(g + 1) * SSD_STATE]
        c_g = bc_ref[0][:, (SSD_GROUPS + g) * SSD_STATE:(SSD_GROUPS + g + 1) * SSD_STATE]
        groups.append((c_g, _nt_dot(c_g.astype(BF16), b_g.astype(BF16)), b_g.T))
    ys = []
    for pair in range(nh // 2):
        c_g, cb, bt_g = groups[(2 * pair) // (nh // SSD_GROUPS)]
        xs_p = xs_ref[0][:, pair * LANES:(pair + 1) * LANES]
        r0, r1 = 2 * pair, 2 * pair + 1
        dt_p = jnp.where(lo, jnp.broadcast_to(dt[:, r0:r0 + 1], (q, LANES)),
                         jnp.broadcast_to(dt[:, r1:r1 + 1], (q, LANES)))
        xdt = (xs_p * dt_p).astype(BF16)
        zero_b = jnp.zeros_like(xdt)
        xdt_bd = jnp.concatenate([jnp.where(lo, xdt, zero_b), jnp.where(lo, zero_b, xdt)], axis=0)
        s_p = s_ref[pair]
        s_b = s_p.astype(BF16)
        s_bd = jnp.concatenate([jnp.where(lo, s_b, zero_b), jnp.where(lo, zero_b, s_b)], axis=0)
        lhs, bts = [], []
        for r in (r0, r1):
            colb = jnp.broadcast_to(cs[:, r:r + 1], (q, q))
            rowb = jnp.broadcast_to(cst[r:r + 1, :], (q, q))
            decay = jnp.exp(jnp.where(tri, colb - rowb, NEG_INF))
            lhs.append((cb * decay).astype(BF16))
            bts.append((bt_g * jnp.broadcast_to(dst[r:r + 1, :], (SSD_STATE, q))).astype(BF16))
        for r in (r0, r1):
            lhs.append((c_g * jnp.broadcast_to(ecs[:, r:r + 1], (q, SSD_STATE))).astype(BF16))
        y = jnp.dot(jnp.concatenate(lhs, axis=1), jnp.concatenate([xdt_bd, s_bd], axis=0),
                    preferred_element_type=F32)
        y = y + xs_p * dexp_ref[:, pair * LANES:(pair + 1) * LANES]
        dec_p = jnp.where(_lane_lt64((1, LANES)), jnp.broadcast_to(etot[:, r0:r0 + 1], (1, LANES)),
                          jnp.broadcast_to(etot[:, r1:r1 + 1], (1, LANES)))
        s_ref[pair] = s_p * dec_p + jnp.dot(jnp.concatenate(bts, axis=1), xdt_bd, preferred_element_type=F32)
        ys.append(y)
    return jnp.concatenate(ys, axis=1)


def _ssd_fwd_body(xs_ref, bc_ref, dt_ref, dtt_ref, pr_ref, pc_ref, dexp_ref, y_ref, s_ref):
    @pl.when(pl.program_id(1) == 0)
    def _():
        s_ref[...] = jnp.zeros_like(s_ref)

    y_ref[0] = _ssd_chunk(0, xs_ref, bc_ref, dt_ref, dtt_ref, pr_ref, pc_ref, dexp_ref, s_ref)


def _ssd_bwd_body(xs_ref, bc_ref, dt_ref, dtt_ref, pr_ref, pc_ref, dexp_ref, yf_ref, z_ref, g_ref, o_ref, s_ref):
    @pl.when(pl.program_id(1) == 0)
    def _():
        s_ref[...] = jnp.zeros_like(s_ref)

    y = _ssd_chunk(1, xs_ref, bc_ref, dt_ref, dtt_ref, pr_ref, pc_ref, dexp_ref, s_ref) + yf_ref[0]
    z = z_ref[0]
    gated = y * (z * jax.nn.sigmoid(z))
    gw = SSD_INNER // SSD_GROUPS
    parts = []
    for g in range(SSD_GROUPS):
        gg = gated[:, g * gw:(g + 1) * gw]
        parts.append(gg * lax.rsqrt(jnp.mean(gg * gg, axis=-1, keepdims=True) + EPS))
    o_ref[0] = (jnp.concatenate(parts, axis=1) * g_ref[...]).astype(o_ref.dtype)


def _ssd_call(u, dtt, dt_bias, a_log, d_skip, norm_g, batch, seq, n_ctx):
    q = SSD_CHUNK
    n_lat, n_c = seq // q, n_ctx // q
    n_ch = n_lat + n_c
    nh = SSD_HEADS
    fwd_chunk = lambda s: jnp.where(s < n_c, n_lat + s, s - n_c)
    bwd_chunk = lambda s: n_ch - 1 - s
    xs_blk = SSD_INNER // SSD_INNER
    bc_blk = (2 * SSD_INNER) // (2 * SSD_GROUPS * SSD_STATE)
    dt_blk = (SSD_INNER + SSD_XBC) // LANES
    dexp = [jnp.repeat(d_skip[d], SSD_HEAD_DIM).reshape(1, SSD_INNER) for d in range(2)]
    prow = [jnp.stack([dt_bias[d], a_log[d]], axis=0) for d in range(2)]
    pcol = [jnp.stack([dt_bias[d], a_log[d]], axis=1) for d in range(2)]

    def specs(chunk):
        return [pl.BlockSpec((1, q, SSD_INNER), lambda b, s: (b, chunk(s), xs_blk)),
                pl.BlockSpec((1, q, 2 * SSD_GROUPS * SSD_STATE), lambda b, s: (b, chunk(s), bc_blk)),
                pl.BlockSpec((1, q, LANES), lambda b, s: (b, chunk(s), dt_blk)),
                pl.BlockSpec((1, 2 * nh, q), lambda b, s: (b, 0, chunk(s))),
                pl.BlockSpec((2, nh), lambda b, s: (0, 0)),
                pl.BlockSpec((nh, 2), lambda b, s: (0, 0)),
                pl.BlockSpec((1, SSD_INNER), lambda b, s: (0, 0))]

    state = pltpu.VMEM((nh // 2, SSD_STATE, LANES), F32)
    yblk = lambda chunk: pl.BlockSpec((1, q, SSD_INNER), lambda b, s: (b, chunk(s), 0))
    y_f = pl.pallas_call(
        _ssd_fwd_body,
        out_shape=jax.ShapeDtypeStruct((batch, seq + n_ctx, SSD_INNER), F32),
        grid=(batch, n_ch), in_specs=specs(fwd_chunk), out_specs=yblk(fwd_chunk),
        scratch_shapes=[state], compiler_params=_params("parallel", "arbitrary"), name="ssd_fwd",
    )(u, u, u, dtt, prow[0], pcol[0], dexp[0])
    return pl.pallas_call(
        _ssd_bwd_body,
        out_shape=jax.ShapeDtypeStruct((batch, seq + n_ctx, SSD_INNER), BF16),
        grid=(batch, n_ch),
        in_specs=specs(bwd_chunk) + [yblk(bwd_chunk), yblk(bwd_chunk),
                                     pl.BlockSpec((1, SSD_INNER), lambda b, s: (0, 0))],
        out_specs=yblk(bwd_chunk),
        scratch_shapes=[state], compiler_params=_params("parallel", "arbitrary"), name="ssd_bwd",
    )(u, u, u, dtt, prow[1], pcol[1], dexp[1], y_f, u, norm_g.reshape(1, SSD_INNER))


def _wout_body(na_ref, da_ref, ssd_ref, w_ref, o_ref):
    k1 = na_ref.shape[1]
    k2 = k1 + da_ref.shape[1]
    acc = jnp.dot(na_ref[...], w_ref[:k1, :], preferred_element_type=F32)
    acc = acc + jnp.dot(da_ref[...], w_ref[k1:k2, :], preferred_element_type=F32)
    acc = acc + jnp.dot(ssd_ref[0], w_ref[k2:, :], preferred_element_type=F32)
    o_ref[...] = acc


def _wout_call(na, da, ssd, w, ssd_tile, tm):
    m = na.shape[0]
    d = w.shape[1]
    return pl.pallas_call(
        _wout_body,
        out_shape=jax.ShapeDtypeStruct((m, d), F32),
        grid=(m // tm,),
        in_specs=[pl.BlockSpec((tm, na.shape[1]), lambda i: (i, 0)),
                  pl.BlockSpec((tm, da.shape[1]), lambda i: (i, 0)),
                  pl.BlockSpec((1, tm, ssd.shape[2]), lambda i: ssd_tile(i) + (0,)),
                  pl.BlockSpec(w.shape, lambda i: (0, 0))],
        out_specs=pl.BlockSpec((tm, d), lambda i: (i, 0)),
        compiler_params=_params("parallel"), name="w_out",
    )(na, da, ssd, w)


def _top_vals(x, k):
    vals = []
    for _ in range(k):
        m = jnp.max(x, axis=0, keepdims=True)
        vals.append(m)
        x = jnp.where(x == m, NEG_INF, x)
    return jnp.concatenate(vals, axis=0)


def _peer_sel_body(h_ref, wq_ref, keys_ref, s1_ref, e1_ref, s2_ref, e2_ref, thr_ref):
    qd = PEER_QDIM // 2
    qv = jnp.dot(h_ref[...], wq_ref[...], preferred_element_type=F32).astype(BF16)
    thr_rows = []
    for hh in range(PEER_HEADS):
        c0 = hh * PEER_QDIM
        s1 = _nt_dot(keys_ref[2 * hh], qv[:, c0:c0 + qd])
        s2 = _nt_dot(keys_ref[2 * hh + 1], qv[:, c0 + qd:c0 + 2 * qd])
        v1 = _top_vals(s1, PEER_TOPK)
        v2 = _top_vals(s2, PEER_TOPK)
        cand = jnp.concatenate([v2 + v1[i:i + 1, :] for i in range(PEER_TOPK)], axis=0)
        top = _top_vals(cand, PEER_TOPK)
        z = jnp.sum(jnp.exp(top - top[0:1, :]), axis=0, keepdims=True)
        s1_ref[hh] = s1
        s2_ref[hh] = s2
        e1_ref[hh] = jnp.exp(s1 - v1[0:1, :])
        e2_ref[hh] = jnp.exp(s2 - v2[0:1, :]) / z
        thr_rows.append(top[PEER_TOPK - 1:PEER_TOPK, :])
    thr_ref[...] = jnp.concatenate(thr_rows, axis=0)


def _peer_sel_call(h, wq, keys, tm=256):
    t, d = h.shape
    stat = jax.ShapeDtypeStruct((PEER_HEADS, PEER_N_KEYS, t), F32)
    sblk = pl.BlockSpec((PEER_HEADS, PEER_N_KEYS, tm), lambda i: (0, 0, i))
    return pl.pallas_call(
        _peer_sel_body,
        out_shape=(stat, stat, stat, stat, jax.ShapeDtypeStruct((PEER_HEADS, t), F32)),
        grid=(t // tm,),
        in_specs=[pl.BlockSpec((tm, d), lambda i: (i, 0)),
                  pl.BlockSpec(wq.shape, lambda i: (0, 0)),
                  pl.BlockSpec(keys.shape, lambda i: (0, 0, 0))],
        out_specs=(sblk, sblk, sblk, sblk, pl.BlockSpec((PEER_HEADS, tm), lambda i: (0, i))),
        compiler_params=_params("parallel"), name="peer_select",
    )(h, wq, keys)


def _gelu_tanh(x):
    return 0.5 * x * (1.0 + jnp.tanh(math.sqrt(2.0 / math.pi) * (x + 0.044715 * (x * x * x))))


def _peer_ffn_body(n_a, h_ref, u_ref, vt_ref, s1_ref, e1_ref, s2_ref, e2_ref, thr_ref, o_ref, acc_ref, wt_ref):
    j = pl.program_id(1)

    @pl.when(j == 0)
    def _():
        acc_ref[...] = jnp.zeros_like(acc_ref)

    nk = PEER_N_KEYS
    act = _gelu_tanh(_nt_dot(u_ref[...], h_ref[...]))
    tm = act.shape[1]
    for a in range(n_a):
        gate = jnp.zeros((nk, tm), F32)
        for hh in range(PEER_HEADS):
            pair = s2_ref[hh] + s1_ref[hh, a:a + 1, :]
            w = e2_ref[hh] * e1_ref[hh, a:a + 1, :]
            gate = gate + jnp.where(pair >= thr_ref[hh:hh + 1, :], w, 0.0)
        wt_ref[a * nk:(a + 1) * nk, :] = (gate * act[a * nk:(a + 1) * nk, :]).astype(BF16)
    acc_ref[...] += jnp.dot(vt_ref[...], wt_ref[...], preferred_element_type=F32)

    @pl.when(j == pl.num_programs(1) - 1)
    def _():
        o_ref[...] = acc_ref[...].T


def _peer_ffn_call(h, u, vt, s1, e1, s2, e2, thr, tm=512, n_a=8):
    t, d = h.shape
    e = u.shape[0]
    te = n_a * PEER_N_KEYS
    a_blk = pl.BlockSpec((PEER_HEADS, n_a, tm), lambda i, j: (0, j, i))
    b_blk = pl.BlockSpec((PEER_HEADS, PEER_N_KEYS, tm), lambda i, j: (0, 0, i))
    return pl.pallas_call(
        functools.partial(_peer_ffn_body, n_a),
        out_shape=jax.ShapeDtypeStruct((t, d), F32),
        grid=(t // tm, e // te),
        in_specs=[pl.BlockSpec((tm, d), lambda i, j: (i, 0)),
                  pl.BlockSpec((te, d), lambda i, j: (j, 0)),
                  pl.BlockSpec((d, te), lambda i, j: (0, j)),
                  a_blk, a_blk, b_blk, b_blk,
                  pl.BlockSpec((PEER_HEADS, tm), lambda i, j: (0, i))],
        out_specs=pl.BlockSpec((tm, d), lambda i, j: (i, 0)),
        scratch_shapes=[pltpu.VMEM((d, tm), F32), pltpu.VMEM((te, tm), BF16)],
        compiler_params=_params("parallel", "arbitrary"), name="peer_ffn",
    )(h, u, vt, s1, e1, s2, e2, thr)


def _peer(h, wq, keys, u, vt):
    s1, e1, s2, e2, thr = _peer_sel_call(h, wq, keys)
    return _peer_ffn_call(h, u, vt, s1, e1, s2, e2, thr)


def kernel(x, c, ctx, c_ctx, ada_w, ada_b, norm_g, w_in, w_out, na_rpb, ssd_conv_w, ssd_conv_b, ssd_dt_bias,
           ssd_a_log, ssd_d, ssd_norm_g, da_lambda, da_sub_g, peer_w_q, peer_sub_keys, peer_u, peer_v):
    batch, seq, d = x.shape
    n_ctx = ctx.shape[1]
    depth = ada_w.shape[0]
    xl = x.reshape(batch * seq, d)
    xc = ctx.reshape(batch * n_ctx, d)
    tm_norm = 256
    sel_lat = lambda i: i // (seq // tm_norm)
    sel_ctx = lambda i: batch
    cos_t, sin_t = _rope_tables(seq)

    o_nq, o_nk, o_nv = 0, 512, 1024
    o_dq, o_dk, o_dv = 1536, 2048, 2560
    o_z = 3072
    o_end = o_z + SSD_INNER + SSD_XBC + 2 * SSD_HEADS
    scale_a = jnp.concatenate([jnp.full((512,), NA_DIM ** -0.5, F32), jnp.ones((1536,), F32)])
    scale_b = jnp.concatenate([jnp.full((512,), DA_DIM ** -0.5, F32), jnp.ones((512,), F32)])

    cond = jnp.concatenate([c, c_ctx[None, :], jnp.zeros((8 - batch - 1, d), F32)], axis=0)
    mods = [_ada_call(cond, ada_w[layer], ada_b[layer])[:batch + 1].reshape(batch + 1, N_MOD, 1, d)
            for layer in range(depth)]
    h_l = h_c = None
    for layer in range(depth):
        need_ctx = layer < depth - 1
        lam_init = 0.8 - 0.6 * math.exp(-0.3 * layer)
        tab = [mods[layer][:, k] for k in range(N_MOD)]
        g = norm_g[layer]
        if layer == 0:
            _, h_l = _norm_call(xl, sel_lat, pre=(g[0:1], tab[0], tab[1]), tm=tm_norm)
            _, h_c = _norm_call(xc, sel_ctx, pre=(g[0:1], tab[0], tab[1]), tm=tm_norm)

        wi = w_in[layer]
        w_a = jnp.concatenate([wi[:, o_nq:o_dq], wi[:, o_dv:o_z]], axis=1).astype(BF16)
        w_b = wi[:, o_dq:o_dv].astype(BF16)
        w_c = jnp.pad(wi[:, o_z:o_end], ((0, 0), (0, SSD_IN_W - (o_end - o_z)))).astype(BF16)
        a_lat = _mm_call(h_l, w_a, BF16, colscale=scale_a, name="proj_attn")
        a_ctx = _mm_call(h_c, w_a, BF16, colscale=scale_a, name="proj_attn")
        qk_lat = _mm_call(h_l, w_b, BF16, colscale=scale_b, rope=(cos_t, sin_t), name="proj_rope")
        qk_ctx = _mm_call(h_c, w_b, BF16, colscale=scale_b, name="proj_qk_ctx")
        s_lat = _mm_call(h_l, w_c, F32, tn=SSD_IN_W // 3, name="proj_ssd")
        s_ctx = _mm_call(h_c, w_c, F32, tn=SSD_IN_W // 3, name="proj_ssd")

        lam_p = da_lambda[layer]
        g_sub = (da_sub_g[layer] * (1.0 - lam_init)).reshape(1, LANES)
        na_l = _na_call(a_lat, a_ctx, _na_bias(na_rpb[layer]), batch, seq, n_ctx)
        da_l = _da_call(qk_lat, qk_ctx, a_lat, a_ctx, lam_p, g_sub, lam_init, batch, seq, n_ctx)

        u_ssd = _ssd_prep_call(s_lat, s_ctx, ssd_conv_w[layer], ssd_conv_b[layer], batch, seq, n_ctx)
        dtt = jnp.transpose(u_ssd[:, :, SSD_INNER + SSD_XBC:SSD_INNER + SSD_XBC + 2 * SSD_HEADS], (0, 2, 1))
        ssd_o = _ssd_call(u_ssd, dtt, ssd_dt_bias[layer], ssd_a_log[layer], ssd_d[layer], ssd_norm_g[layer],
                          batch, seq, n_ctx)

        wo = w_out[layer].astype(BF16)
        wq = peer_w_q[layer].astype(BF16)
        keys = peer_sub_keys[layer].reshape(2 * PEER_HEADS, PEER_N_KEYS, PEER_QDIM // 2).astype(BF16)
        pu = peer_u[layer].astype(BF16)
        pvt = peer_v[layer].T.astype(BF16)
        nxt = layer + 1 < depth

        tm_o = 512
        lat_tiles = seq // tm_o
        mix_l = _wout_call(na_l, da_l, ssd_o, wo, lambda i: (i // lat_tiles, i % lat_tiles), tm_o)
        xl, h2_l = _norm_call(xl, sel_lat, res=(mix_l, tab[2], g[1:2]), pre=(g[2:3], tab[3], tab[4]), tm=tm_norm)
        f_l = _peer(h2_l, wq, keys, pu, pvt)
        if nxt:
            pre_n = (norm_g[layer + 1][0:1], mods[layer + 1][:, 0], mods[layer + 1][:, 1])
        else:
            pre_n = None
        xl, h_l = _norm_call(xl, sel_lat, res=(f_l, tab[5], g[3:4]), pre=pre_n, tm=tm_norm)

        if need_ctx:
            na_c, da_c = _ctx_attn_call(a_ctx, qk_ctx, lam_p, g_sub, lam_init, batch, n_ctx)
            mix_c = _wout_call(na_c, da_c, ssd_o, wo, lambda i: (i, seq // n_ctx), n_ctx)
            xc, h2_c = _norm_call(xc, sel_ctx, res=(mix_c, tab[2], g[1:2]), pre=(g[2:3], tab[3], tab[4]), tm=tm_norm)
            f_c = _peer(h2_c, wq, keys, pu, pvt)
            xc, h_c = _norm_call(xc, sel_ctx, res=(f_c, tab[5], g[3:4]), pre=pre_n, tm=tm_norm)
    return xl.reshape(batch, seq, d)
```

```python
import functools
import math

import numpy as np
import jax
import jax.numpy as jnp
from jax import lax
from jax.experimental import pallas as pl
from jax.experimental.pallas import tpu as pltpu

F32 = jnp.float32
BF16 = jnp.bfloat16
EPS = 1e-6
NEG_INF = float("-inf")

VMEM_LIMIT_BYTES = 56 * 1024 * 1024
LANES = 128

GRID_W = 64
NA_DIM = 64
NA_HEADS = 8
NA_WIN_ROWS = 8
NA_WIN_COLS = 16
DA_DIM = 64
DA_HEADS = 4
ROPE_BASE = 10000.0
SSD_INNER = 1024
SSD_HEAD_DIM = 64
SSD_HEADS = 16
SSD_GROUPS = 2
SSD_STATE = 128
SSD_CONV = 5
SSD_CHUNK = 128
SSD_XBC = SSD_INNER + 2 * SSD_GROUPS * SSD_STATE
SSD_IN_W = SSD_INNER + SSD_XBC + LANES
PEER_HEADS = 8
PEER_QDIM = 256
PEER_N_KEYS = 128
PEER_TOPK = 16
N_MOD = 6


def _params(*sem):
    return pltpu.CompilerParams(dimension_semantics=sem, vmem_limit_bytes=VMEM_LIMIT_BYTES)


def _nt_dot(a, b):
    return lax.dot_general(a, b, (((1,), (1,)), ((), ())), preferred_element_type=F32)


def _lane_lt64(shape):
    return lax.broadcasted_iota(jnp.int32, shape, len(shape) - 1) < (LANES // 2)


def _ada_body(c_ref, w_ref, b_ref, o_ref):
    c = c_ref[...]
    sc = (c * jax.nn.sigmoid(c)).astype(BF16)
    o_ref[...] = jnp.dot(sc, w_ref[...].astype(BF16), preferred_element_type=F32) + b_ref[...]


def _ada_call(cond, w, b, tn=768):
    m, d = cond.shape
    n = w.shape[1]
    return pl.pallas_call(
        _ada_body,
        out_shape=jax.ShapeDtypeStruct((m, n), F32),
        grid=(n // tn,),
        in_specs=[pl.BlockSpec((m, d), lambda j: (0, 0)),
                  pl.BlockSpec((d, tn), lambda j: (0, j)),
                  pl.BlockSpec((1, tn), lambda j: (0, j))],
        out_specs=pl.BlockSpec((m, tn), lambda j: (0, j)),
        compiler_params=_params("arbitrary"),
        name="ada_mod",
    )(cond, w, b.reshape(1, n))


def _rms(x, g):
    return x * lax.rsqrt(jnp.mean(x * x, axis=-1, keepdims=True) + EPS) * g


def _norm_body(has_res, has_pre, *refs):
    refs = list(refs)
    x = refs.pop(0)[...]
    if has_res:
        f_ref, gate_ref, gpost_ref = refs.pop(0), refs.pop(0), refs.pop(0)
        x = x + gate_ref[0] * _rms(f_ref[...], gpost_ref[...])
    if has_pre:
        gpre_ref, shift_ref, scale_ref = refs.pop(0), refs.pop(0), refs.pop(0)
    if has_res:
        refs.pop(0)[...] = x
    if has_pre:
        h = _rms(x, gpre_ref[...]) * (1.0 + scale_ref[0]) + shift_ref[0]
        refs.pop(0)[...] = h.astype(BF16)


def _norm_call(x, sel, *, res=None, pre=None, tm=256):
    m, d = x.shape
    row = pl.BlockSpec((tm, d), lambda i: (i, 0))
    vec = pl.BlockSpec((1, d), lambda i: (0, 0))
    tab = pl.BlockSpec((1, 1, d), lambda i: (sel(i), 0, 0))
    args, specs, outs, ospecs = [x], [row], [], []
    if res is not None:
        args += list(res)
        specs += [row, tab, vec]
        outs.append(jax.ShapeDtypeStruct((m, d), F32))
        ospecs.append(row)
    if pre is not None:
        args += list(pre)
        specs += [vec, tab, tab]
        outs.append(jax.ShapeDtypeStruct((m, d), BF16))
        ospecs.append(row)
    res_out = pl.pallas_call(
        functools.partial(_norm_body, res is not None, pre is not None),
        out_shape=tuple(outs), grid=(m // tm,), in_specs=specs, out_specs=tuple(ospecs),
        compiler_params=_params("parallel"), name="norm_mod",
    )(*args)
    res_out = list(res_out)
    x_new = res_out.pop(0) if res is not None else None
    h = res_out.pop(0) if pre is not None else None
    return x_new, h


def _rope_partner(x):
    lane = lax.broadcasted_iota(jnp.int32, x.shape, 1)
    first = (lane % 32) < 16
    return jnp.where(first, pltpu.roll(x, LANES - 16, 1), pltpu.roll(x, 16, 1))


def _mm_body(epi, h_ref, w_ref, *rest):
    acc = jnp.dot(h_ref[...], w_ref[...], preferred_element_type=F32)
    if epi == "plain":
        (o_ref,) = rest
    elif epi == "scale":
        cs_ref, o_ref = rest
        acc = acc * cs_ref[...]
    else:
        cos_ref, sin_ref, cs_ref, o_ref = rest
        cos = cos_ref[...]
        sin = sin_ref[...]
        n = acc.shape[1]
        blocks = []
        for c in range(n // LANES):
            xb = acc[:, c * LANES:(c + 1) * LANES]
            blocks.append(xb * cos + _rope_partner(xb) * sin)
        acc = jnp.concatenate(blocks, axis=1) * cs_ref[...]
    o_ref[...] = acc.astype(o_ref.dtype)


def _mm_call(h, w, out_dtype, *, colscale=None, rope=None, tm=512, tn=None, name="proj"):
    m, k = h.shape
    n = w.shape[1]
    tn = n if tn is None else tn
    args = [h, w]
    specs = [pl.BlockSpec((tm, k), lambda i, j: (i, 0)),
             pl.BlockSpec((k, tn), lambda i, j: (0, j))]
    epi = "plain"
    if rope is not None:
        cos_t, sin_t = rope
        nt = cos_t.shape[0] // tm
        args += [cos_t, sin_t]
        specs += [pl.BlockSpec((tm, LANES), lambda i, j: (i % nt, 0))] * 2
        epi = "rope"
    if colscale is not None:
        args.append(colscale.reshape(1, n))
        specs.append(pl.BlockSpec((1, tn), lambda i, j: (0, j)))
        epi = "scale" if epi == "plain" else epi
    return pl.pallas_call(
        functools.partial(_mm_body, epi),
        out_shape=jax.ShapeDtypeStruct((m, n), out_dtype),
        grid=(m // tm, n // tn), in_specs=specs,
        out_specs=pl.BlockSpec((tm, tn), lambda i, j: (i, j)),
        compiler_params=_params("parallel", "arbitrary"), name=name,
    )(*args)


def _rope_tables(seq):
    quarter = DA_DIM // 4
    inv_freq = 1.0 / (ROPE_BASE ** (jnp.arange(quarter, dtype=F32) / quarter))
    t = jnp.arange(seq)
    ang_r = (t // GRID_W).astype(F32)[:, None] * inv_freq
    ang_c = (t % GRID_W).astype(F32)[:, None] * inv_freq

    def unit(ang):
        return (jnp.concatenate([jnp.cos(ang), jnp.cos(ang)], -1),
                jnp.concatenate([-jnp.sin(ang), jnp.sin(ang)], -1))

    cr, sr = unit(ang_r)
    cc, sc = unit(ang_c)
    cos64 = jnp.concatenate([cr, cc], -1)
    sin64 = jnp.concatenate([sr, sc], -1)
    return jnp.tile(cos64, (1, 2)), jnp.tile(sin64, (1, 2))


def _stack_heads(q2):
    lo = _lane_lt64(q2.shape)
    zero = jnp.zeros_like(q2)
    return jnp.concatenate([jnp.where(lo, q2, zero), jnp.where(lo, zero, q2)], axis=0)


def _unstack_heads(o):
    r = o.shape[0] // 2
    return jnp.where(_lane_lt64((r, o.shape[1])), o[:r], o[r:])


NA_ROW_GROUP = 8


def _na_body(rows, q_ref, k_ref, v_ref, kc_ref, vc_ref, bias_ref, o_ref):
    kc = kc_ref[...]
    vc = vc_ref[...]
    kh = NA_WIN_ROWS

    def scores(r):
        rs = jnp.clip(r - kh // 2, 0, rows - kh)
        q0 = pl.multiple_of(r * GRID_W, GRID_W)
        k0 = pl.multiple_of(rs * GRID_W, GRID_W)
        qs = _stack_heads(q_ref[pl.ds(q0, GRID_W), :])
        s_loc = _nt_dot(qs, k_ref[pl.ds(k0, kh * GRID_W), :]) + bias_ref[0, r - rs]
        return q0, k0, s_loc, _nt_dot(qs, kc)

    def finish(q0, k0, p_loc, p_ctx, l):
        o = (jnp.dot(p_loc, v_ref[pl.ds(k0, kh * GRID_W), :], preferred_element_type=F32)
             + jnp.dot(p_ctx, vc, preferred_element_type=F32)) / l
        o_ref[pl.ds(q0, GRID_W), :] = _unstack_heads(o).astype(o_ref.dtype)

    def row_group(gi, carry):
        nxt = scores(gi * NA_ROW_GROUP)
        pending = None
        for j in range(NA_ROW_GROUP):
            q0, k0, s_loc, s_ctx = nxt
            if j + 1 < NA_ROW_GROUP:
                nxt = scores(gi * NA_ROW_GROUP + j + 1)
            m = jnp.maximum(jnp.max(s_loc, axis=-1, keepdims=True), jnp.max(s_ctx, axis=-1, keepdims=True))
            p_loc = jnp.exp(s_loc - m)
            p_ctx = jnp.exp(s_ctx - m)
            l = jnp.sum(p_loc, axis=-1, keepdims=True) + jnp.sum(p_ctx, axis=-1, keepdims=True)
            if pending is not None:
                finish(*pending)
            pending = (q0, k0, p_loc.astype(BF16), p_ctx.astype(BF16), l)
        finish(*pending)
        return carry

    lax.fori_loop(0, rows // NA_ROW_GROUP, row_group, 0)


def _na_bias(rpb):
    kh, w = NA_WIN_ROWS, GRID_W
    c = np.arange(w)
    wstart = np.clip(c - NA_WIN_COLS // 2, 0, w - NA_WIN_COLS)
    kc = np.arange(w)
    valid = (kc[None, :] >= wstart[:, None]) & (kc[None, :] < wstart[:, None] + NA_WIN_COLS)
    cidx = np.clip(kc[None, :] - c[:, None] + NA_WIN_COLS - 1, 0, 2 * NA_WIN_COLS - 2)
    typ = np.arange(kh)
    ridx = np.arange(kh)[None, :] - typ[:, None] + NA_WIN_ROWS - 1
    b = rpb[:, ridx][:, :, :, cidx]
    b = jnp.where(valid[None, None, None], b, NEG_INF)
    b = jnp.transpose(b, (0, 1, 3, 2, 4)).reshape(NA_HEADS, kh, w, kh * w)
    b = b.reshape(NA_HEADS // 2, 2, kh, w, kh * w)
    return jnp.transpose(b, (0, 2, 1, 3, 4)).reshape(NA_HEADS // 2, kh, 2 * w, kh * w).astype(F32)


def _na_call(a_lat, a_ctx, bias, batch, seq, n_ctx):
    rows = seq // GRID_W
    pairs = NA_HEADS // 2
    lat = lambda c0: pl.BlockSpec((seq, LANES), lambda b, p: (b, c0 + p))
    ctx = lambda c0: pl.BlockSpec((n_ctx, LANES), lambda b, p: (b, c0 + p))
    return pl.pallas_call(
        functools.partial(_na_body, rows),
        out_shape=jax.ShapeDtypeStruct((batch * seq, NA_HEADS * NA_DIM), BF16),
        grid=(batch, pairs),
        in_specs=[lat(0), lat(pairs), lat(2 * pairs), ctx(pairs), ctx(2 * pairs),
                  pl.BlockSpec((1,) + bias.shape[1:], lambda b, p: (p, 0, 0, 0))],
        out_specs=pl.BlockSpec((seq, LANES), lambda b, p: (b, p)),
        compiler_params=_params("parallel", "parallel"), name="na_attn",
    )(a_lat, a_lat, a_lat, a_ctx, a_ctx, bias)


def _da_lambda(lam_ref, lam_init):
    lf = lam_ref[...]
    a = jnp.sum(lf[0:1] * lf[1:2], axis=-1, keepdims=True)
    b = jnp.sum(lf[2:3] * lf[3:4], axis=-1, keepdims=True)
    return jnp.exp(a) - jnp.exp(b) + lam_init


def _da_finish(acc, l, lam, g):
    tq = acc.shape[0] // 2
    o = acc / l
    d = o[:tq] - lam * o[tq:]
    return d * lax.rsqrt(jnp.mean(d * d, axis=-1, keepdims=True) + EPS) * g


def _flash_step(qs, k, v, m, l, acc):
    s = _nt_dot(qs, k)
    m_new = jnp.maximum(m, jnp.max(s, axis=-1, keepdims=True))
    alpha = jnp.exp(m - m_new)
    p = jnp.exp(s - m_new)
    l = alpha * l + jnp.sum(p, axis=-1, keepdims=True)
    acc = alpha * acc + jnp.dot(p.astype(BF16), v, preferred_element_type=F32)
    return m_new, l, acc


def _da_body(lam_init, n_chunks, tk, q_ref, k_ref, v_ref, kc_ref, vc_ref, lam_ref, g_ref, o_ref):
    qs = _stack_heads(q_ref[...])
    r = qs.shape[0]
    chunks = [(k_ref, v_ref, c * tk, tk) for c in range(n_chunks)] + [(kc_ref, vc_ref, 0, kc_ref.shape[0])]

    def scores(i):
        kr, _, k0, n = chunks[i]
        return _nt_dot(qs, kr[k0:k0 + n, :])

    m = jnp.full((r, 1), NEG_INF, F32)
    l = jnp.zeros((r, 1), F32)
    acc = jnp.zeros((r, LANES), F32)
    s_next = scores(0)
    pending = None
    for i in range(len(chunks)):
        s = s_next
        if i + 1 < len(chunks):
            s_next = scores(i + 1)
        m_new = jnp.maximum(m, jnp.max(s, axis=-1, keepdims=True))
        alpha = jnp.exp(m - m_new)
        p = jnp.exp(s - m_new)
        l = alpha * l + jnp.sum(p, axis=-1, keepdims=True)
        m = m_new
        if pending is not None:
            p_prev, alpha_prev, (_, vr, k0, n) = pending
            acc = alpha_prev * acc + jnp.dot(p_prev, vr[k0:k0 + n, :], preferred_element_type=F32)
        pending = (p.astype(BF16), alpha, chunks[i])
    p_prev, alpha_prev, (_, vr, k0, n) = pending
    acc = alpha_prev * acc + jnp.dot(p_prev, vr[k0:k0 + n, :], preferred_element_type=F32)
    o_ref[...] = _da_finish(acc, l, _da_lambda(lam_ref, lam_init), g_ref[...]).astype(o_ref.dtype)


def _da_call(qk_lat, qk_ctx, a_lat, a_ctx, lam_p, g_scaled, lam_init, batch, seq, n_ctx, tq=256, tk=512):
    nq = seq // tq
    h = DA_HEADS
    return pl.pallas_call(
        functools.partial(_da_body, lam_init, seq // tk, tk),
        out_shape=jax.ShapeDtypeStruct((batch * seq, h * 2 * DA_DIM), BF16),
        grid=(batch, h, nq),
        in_specs=[pl.BlockSpec((tq, LANES), lambda b, hh, i: (b * nq + i, hh)),
                  pl.BlockSpec((seq, LANES), lambda b, hh, i: (b, h + hh)),
                  pl.BlockSpec((seq, LANES), lambda b, hh, i: (b, 3 * h + hh)),
                  pl.BlockSpec((n_ctx, LANES), lambda b, hh, i: (b, h + hh)),
                  pl.BlockSpec((n_ctx, LANES), lambda b, hh, i: (b, 3 * h + hh)),
                  pl.BlockSpec(lam_p.shape, lambda b, hh, i: (0, 0)),
                  pl.BlockSpec((1, LANES), lambda b, hh, i: (0, 0))],
        out_specs=pl.BlockSpec((tq, LANES), lambda b, hh, i: (b * nq + i, hh)),
        compiler_params=_params("parallel", "parallel", "arbitrary"), name="da_attn",
    )(qk_lat, qk_lat, a_lat, qk_ctx, a_ctx, lam_p, g_scaled)


def _ctx_attn_body(lam_init, a_ref, qk_ref, lam_ref, g_ref, na_ref, da_ref):
    pairs = NA_HEADS // 2
    for p in range(pairs):
        qs = _stack_heads(a_ref[:, p * LANES:(p + 1) * LANES])
        k = a_ref[:, (pairs + p) * LANES:(pairs + p + 1) * LANES]
        v = a_ref[:, (2 * pairs + p) * LANES:(2 * pairs + p + 1) * LANES]
        s = _nt_dot(qs, k)
        e = jnp.exp(s - jnp.max(s, axis=-1, keepdims=True))
        o = jnp.dot(e.astype(BF16), v, preferred_element_type=F32) / jnp.sum(e, axis=-1, keepdims=True)
        na_ref[:, p * LANES:(p + 1) * LANES] = _unstack_heads(o).astype(na_ref.dtype)
    lam = _da_lambda(lam_ref, lam_init)
    for hh in range(DA_HEADS):
        qs = _stack_heads(qk_ref[:, hh * LANES:(hh + 1) * LANES])
        k = qk_ref[:, (DA_HEADS + hh) * LANES:(DA_HEADS + hh + 1) * LANES]
        v = a_ref[:, (3 * pairs + hh) * LANES:(3 * pairs + hh + 1) * LANES]
        r = qs.shape[0]
        init = (jnp.full((r, 1), NEG_INF, F32), jnp.zeros((r, 1), F32), jnp.zeros((r, LANES), F32))
        m, l, acc = _flash_step(qs, k, v, *init)
        da_ref[:, hh * LANES:(hh + 1) * LANES] = _da_finish(acc, l, lam, g_ref[...]).astype(da_ref.dtype)


def _ctx_attn_call(a_ctx, qk_ctx, lam_p, g_scaled, lam_init, batch, n_ctx):
    return pl.pallas_call(
        functools.partial(_ctx_attn_body, lam_init),
        out_shape=(jax.ShapeDtypeStruct((batch * n_ctx, NA_HEADS * NA_DIM), BF16),
                   jax.ShapeDtypeStruct((batch * n_ctx, DA_HEADS * 2 * DA_DIM), BF16)),
        grid=(batch,),
        in_specs=[pl.BlockSpec((n_ctx, a_ctx.shape[1]), lambda b: (b, 0)),
                  pl.BlockSpec((n_ctx, qk_ctx.shape[1]), lambda b: (b, 0)),
                  pl.BlockSpec(lam_p.shape, lambda b: (0, 0)),
                  pl.BlockSpec((1, LANES), lambda b: (0, 0))],
        out_specs=(pl.BlockSpec((n_ctx, NA_HEADS * NA_DIM), lambda b: (b, 0)),
                   pl.BlockSpec((n_ctx, DA_HEADS * 2 * DA_DIM), lambda b: (b, 0))),
        compiler_params=_params("parallel"), name="ctx_attn",
    )(a_ctx, qk_ctx, lam_p, g_scaled)


def _conv_seg(u, w, bias):
    n = u.shape[0]
    t = lax.broadcasted_iota(jnp.int32, u.shape, 0)
    acc = jnp.zeros_like(u) + bias
    for kk in range(SSD_CONV):
        d = kk - SSD_CONV // 2
        if d == 0:
            sh = u
        else:
            sh = pltpu.roll(u, (-d) % n, 0)
            sh = jnp.where((t + d >= 0) & (t + d < n), sh, 0.0)
        acc = acc + sh * w[kk:kk + 1, :]
    return acc * jax.nn.sigmoid(acc)


def _ssd_prep_body(seq, n_zc, lat_ref, ctx_ref, w_ref, b_ref, o_ref):
    j = pl.program_id(1)
    is_conv = (j >= n_zc) & (j < n_zc + SSD_XBC // LANES)

    @pl.when(is_conv)
    def _():
        w = w_ref[...]
        bias = b_ref[...]
        o_ref[0, :seq, :] = _conv_seg(lat_ref[...], w, bias)
        o_ref[0, seq:, :] = _conv_seg(ctx_ref[...], w, bias)

    @pl.when(jnp.logical_not(is_conv))
    def _():
        o_ref[0, :seq, :] = lat_ref[...]
        o_ref[0, seq:, :] = ctx_ref[...]


def _ssd_prep_call(s_lat, s_ctx, conv_w, conv_b, batch, seq, n_ctx):
    n_zc = SSD_INNER // LANES
    ncol = SSD_IN_W // LANES
    n_cw = SSD_XBC // LANES
    wcol = lambda b, j: (0, jnp.clip(j - n_zc, 0, n_cw - 1))
    return pl.pallas_call(
        functools.partial(_ssd_prep_body, seq, n_zc),
        out_shape=jax.ShapeDtypeStruct((batch, seq + n_ctx, SSD_IN_W), F32),
        grid=(batch, ncol),
        in_specs=[pl.BlockSpec((seq, LANES), lambda b, j: (b, j)),
                  pl.BlockSpec((n_ctx, LANES), lambda b, j: (b, j)),
                  pl.BlockSpec((SSD_CONV, LANES), wcol),
                  pl.BlockSpec((1, LANES), wcol)],
        out_specs=pl.BlockSpec((1, seq + n_ctx, LANES), lambda b, j: (b, 0, j)),
        compiler_params=_params("parallel", "parallel"), name="ssd_prep",
    )(s_lat, s_ctx, conv_w, conv_b.reshape(1, -1))


def _softplus(x):
    return jnp.maximum(x, 0.0) + jnp.log1p(jnp.exp(-jnp.abs(x)))


def _ssd_chunk(direction, xs_ref, bc_ref, dt_ref, dtt_ref, pr_ref, pc_ref, dexp_ref, s_ref):
    q = SSD_CHUNK
    nh = SSD_HEADS
    hi = lax.Precision.HIGHEST
    d0 = direction * nh
    dt = _softplus(dt_ref[0][:, d0:d0 + nh] + pr_ref[0:1, :])
    a = dt * (-jnp.exp(pr_ref[1:2, :]))
    dtt = _softplus(dtt_ref[0][d0:d0 + nh, :] + pc_ref[:, 0:1])
    at = dtt * (-jnp.exp(pc_ref[:, 1:2]))
    ii = lax.broadcasted_iota(jnp.int32, (q, q), 0)
    jj = lax.broadcasted_iota(jnp.int32, (q, q), 1)
    tri = (jj <= ii) if direction == 0 else (jj >= ii)
    tri_f = tri.astype(F32)
    cs = jnp.dot(tri_f, a, precision=hi, preferred_element_type=F32)
    cst = lax.dot_general(at, tri_f, (((1,), (1,)), ((), ())), precision=hi,
                          preferred_element_type=F32)
    edge = q - 1 if direction == 0 else 0
    tot_t = cst[:, edge:edge + 1]
    dst = jnp.exp(tot_t - cst)
    ecs = jnp.exp(cs)
    etot = jnp.exp(cs[edge:edge + 1, :])

    lo = _lane_lt64((q, LANES))
    groups = []
    for g in range(SSD_GROUPS):
        b_g = bc_ref[0][:, g * SSD_STATE:(g + 1) * SSD_STATE]
        c_g = bc_ref[0][:, (SSD_GROUPS + g) * SSD_STATE:(SSD_GROUPS + g + 1) * SSD_STATE]
        groups.append((c_g, _nt_dot(c_g.astype(BF16), b_g.astype(BF16)), b_g.T))
    ys = []
    for pair in range(nh // 2):
        c_g, cb, bt_g = groups[(2 * pair) // (nh // SSD_GROUPS)]
        xs_p = xs_ref[0][:, pair * LANES:(pair + 1) * LANES]
        r0, r1 = 2 * pair, 2 * pair + 1
        dt_p = jnp.where(lo, jnp.broadcast_to(dt[:, r0:r0 + 1], (q, LANES)),
                         jnp.broadcast_to(dt[:, r1:r1 + 1], (q, LANES)))
        xdt = (xs_p * dt_p).astype(BF16)
        zero_b = jnp.zeros_like(xdt)
        xdt_bd = jnp.concatenate([jnp.where(lo, xdt, zero_b), jnp.where(lo, zero_b, xdt)], axis=0)
        s_p = s_ref[pair]
        s_b = s_p.astype(BF16)
        s_bd = jnp.concatenate([jnp.where(lo, s_b, zero_b), jnp.where(lo, zero_b, s_b)], axis=0)
        lhs, bts = [], []
        for r in (r0, r1):
            colb = jnp.broadcast_to(cs[:, r:r + 1], (q, q))
            rowb = jnp.broadcast_to(cst[r:r + 1, :], (q, q))
            decay = jnp.exp(jnp.where(tri, colb - rowb, NEG_INF))
            lhs.append((cb * decay).astype(BF16))
            bts.append((bt_g * jnp.broadcast_to(dst[r:r + 1, :], (SSD_STATE, q))).astype(BF16))
        for r in (r0, r1):
            lhs.append((c_g * jnp.broadcast_to(ecs[:, r:r + 1], (q, SSD_STATE))).astype(BF16))
        y = jnp.dot(jnp.concatenate(lhs, axis=1), jnp.concatenate([xdt_bd, s_bd], axis=0),
                    preferred_element_type=F32)
        y = y + xs_p * dexp_ref[:, pair * LANES:(pair + 1) * LANES]
        dec_p = jnp.where(_lane_lt64((1, LANES)), jnp.broadcast_to(etot[:, r0:r0 + 1], (1, LANES)),
                          jnp.broadcast_to(etot[:, r1:r1 + 1], (1, LANES)))
        s_ref[pair] = s_p * dec_p + jnp.dot(jnp.concatenate(bts, axis=1), xdt_bd, preferred_element_type=F32)
        ys.append(y)
    return jnp.concatenate(ys, axis=1)


def _ssd_fwd_body(xs_ref, bc_ref, dt_ref, dtt_ref, pr_ref, pc_ref, dexp_ref, y_ref, s_ref):
    @pl.when(pl.program_id(1) == 0)
    def _():
        s_ref[...] = jnp.zeros_like(s_ref)

    y_ref[0] = _ssd_chunk(0, xs_ref, bc_ref, dt_ref, dtt_ref, pr_ref, pc_ref, dexp_ref, s_ref)


def _ssd_bwd_body(xs_ref, bc_ref, dt_ref, dtt_ref, pr_ref, pc_ref, dexp_ref, yf_ref, z_ref, g_ref, o_ref, s_ref):
    @pl.when(pl.program_id(1) == 0)
    def _():
        s_ref[...] = jnp.zeros_like(s_ref)

    y = _ssd_chunk(1, xs_ref, bc_ref, dt_ref, dtt_ref, pr_ref, pc_ref, dexp_ref, s_ref) + yf_ref[0]
    z = z_ref[0]
    gated = y * (z * jax.nn.sigmoid(z))
    gw = SSD_INNER // SSD_GROUPS
    parts = []
    for g in range(SSD_GROUPS):
        gg = gated[:, g * gw:(g + 1) * gw]
        parts.append(gg * lax.rsqrt(jnp.mean(gg * gg, axis=-1, keepdims=True) + EPS))
    o_ref[0] = (jnp.concatenate(parts, axis=1) * g_ref[...]).astype(o_ref.dtype)


def _ssd_call(u, dtt, dt_bias, a_log, d_skip, norm_g, batch, seq, n_ctx):
    q = SSD_CHUNK
    n_lat, n_c = seq // q, n_ctx // q
    n_ch = n_lat + n_c
    nh = SSD_HEADS
    fwd_chunk = lambda s: jnp.where(s < n_c, n_lat + s, s - n_c)
    bwd_chunk = lambda s: n_ch - 1 - s
    xs_blk = SSD_INNER // SSD_INNER
    bc_blk = (2 * SSD_INNER) // (2 * SSD_GROUPS * SSD_STATE)
    dt_blk = (SSD_INNER + SSD_XBC) // LANES
    dexp = [jnp.repeat(d_skip[d], SSD_HEAD_DIM).reshape(1, SSD_INNER) for d in range(2)]
    prow = [jnp.stack([dt_bias[d], a_log[d]], axis=0) for d in range(2)]
    pcol = [jnp.stack([dt_bias[d], a_log[d]], axis=1) for d in range(2)]

    def specs(chunk):
        return [pl.BlockSpec((1, q, SSD_INNER), lambda b, s: (b, chunk(s), xs_blk)),
                pl.BlockSpec((1, q, 2 * SSD_GROUPS * SSD_STATE), lambda b, s: (b, chunk(s), bc_blk)),
                pl.BlockSpec((1, q, LANES), lambda b, s: (b, chunk(s), dt_blk)),
                pl.BlockSpec((1, 2 * nh, q), lambda b, s: (b, 0, chunk(s))),
                pl.BlockSpec((2, nh), lambda b, s: (0, 0)),
                pl.BlockSpec((nh, 2), lambda b, s: (0, 0)),
                pl.BlockSpec((1, SSD_INNER), lambda b, s: (0, 0))]

    state = pltpu.VMEM((nh // 2, SSD_STATE, LANES), F32)
    yblk = lambda chunk: pl.BlockSpec((1, q, SSD_INNER), lambda b, s: (b, chunk(s), 0))
    y_f = pl.pallas_call(
        _ssd_fwd_body,
        out_shape=jax.ShapeDtypeStruct((batch, seq + n_ctx, SSD_INNER), F32),
        grid=(batch, n_ch), in_specs=specs(fwd_chunk), out_specs=yblk(fwd_chunk),
        scratch_shapes=[state], compiler_params=_params("parallel", "arbitrary"), name="ssd_fwd",
    )(u, u, u, dtt, prow[0], pcol[0], dexp[0])
    return pl.pallas_call(
        _ssd_bwd_body,
        out_shape=jax.ShapeDtypeStruct((batch, seq + n_ctx, SSD_INNER), BF16),
        grid=(batch, n_ch),
        in_specs=specs(bwd_chunk) + [yblk(bwd_chunk), yblk(bwd_chunk),
                                     pl.BlockSpec((1, SSD_INNER), lambda b, s: (0, 0))],
        out_specs=yblk(bwd_chunk),
        scratch_shapes=[state], compiler_params=_params("parallel", "arbitrary"), name="ssd_bwd",
    )(u, u, u, dtt, prow[1], pcol[1], dexp[1], y_f, u, norm_g.reshape(1, SSD_INNER))


def _wout_body(na_ref, da_ref, ssd_ref, w_ref, o_ref):
    k1 = na_ref.shape[1]
    k2 = k1 + da_ref.shape[1]
    acc = jnp.dot(na_ref[...], w_ref[:k1, :], preferred_element_type=F32)
    acc = acc + jnp.dot(da_ref[...], w_ref[k1:k2, :], preferred_element_type=F32)
    acc = acc + jnp.dot(ssd_ref[0], w_ref[k2:, :], preferred_element_type=F32)
    o_ref[...] = acc


def _wout_call(na, da, ssd, w, ssd_tile, tm):
    m = na.shape[0]
    d = w.shape[1]
    return pl.pallas_call(
        _wout_body,
        out_shape=jax.ShapeDtypeStruct((m, d), F32),
        grid=(m // tm,),
        in_specs=[pl.BlockSpec((tm, na.shape[1]), lambda i: (i, 0)),
                  pl.BlockSpec((tm, da.shape[1]), lambda i: (i, 0)),
                  pl.BlockSpec((1, tm, ssd.shape[2]), lambda i: ssd_tile(i) + (0,)),
                  pl.BlockSpec(w.shape, lambda i: (0, 0))],
        out_specs=pl.BlockSpec((tm, d), lambda i: (i, 0)),
        compiler_params=_params("parallel"), name="w_out",
    )(na, da, ssd, w)


def _top_vals(x, k, with_rank=False):
    vals = []
    rank = jnp.full(x.shape, float(k), F32) if with_rank else None
    for i in range(k):
        m = jnp.max(x, axis=0, keepdims=True)
        vals.append(m)
        hit = x == m
        if with_rank:
            rank = jnp.where(hit, float(i), rank)
        x = jnp.where(hit, NEG_INF, x)
    vals = jnp.concatenate(vals, axis=0)
    return (vals, rank) if with_rank else vals


def _pair_candidates(v1, v2):
    k = PEER_TOPK
    half = k // 2
    row = lax.broadcasted_iota(jnp.int32, (half, v1.shape[1]), 0)
    parts = [v2 + v1[0:1, :]]
    for i in range(1, half):
        parts.append(jnp.where(row < k // (i + 1), v2[:half] + v1[i:i + 1, :], NEG_INF))
    parts.append(v1[half:] + v2[0:1, :])
    return jnp.concatenate(parts, axis=0)


def _peer_sel_body(h_ref, wq_ref, keys_ref, cnt_ref, e1_ref, rank_ref, e2_ref):
    qd = PEER_QDIM // 2
    k = PEER_TOPK
    qv = jnp.dot(h_ref[...], wq_ref[...], preferred_element_type=F32).astype(BF16)
    for hh in range(PEER_HEADS):
        c0 = hh * PEER_QDIM
        s1 = _nt_dot(keys_ref[2 * hh], qv[:, c0:c0 + qd])
        s2 = _nt_dot(keys_ref[2 * hh + 1], qv[:, c0 + qd:c0 + 2 * qd])
        v1 = _top_vals(s1, k)
        v2, rank = _top_vals(s2, k, with_rank=True)
        top = _top_vals(_pair_candidates(v1, v2), k)
        thr = top[k - 1:k, :]
        z = jnp.sum(jnp.exp(top - top[0:1, :]), axis=0, keepdims=True)
        cnt = jnp.zeros_like(s1)
        for r in range(k):
            cnt = cnt + jnp.where(s1 + v2[r:r + 1, :] >= thr, 1.0, 0.0)
        cnt_ref[hh] = cnt
        rank_ref[hh] = rank.astype(BF16)
        e1_ref[hh] = jnp.exp(s1 - v1[0:1, :])
        e2_ref[hh] = (jnp.exp(s2 - v2[0:1, :]) / z).astype(BF16)


def _peer_sel_call(h, wq, keys, tm=256):
    t, d = h.shape
    shape = (PEER_HEADS, PEER_N_KEYS, t)
    sblk = pl.BlockSpec((PEER_HEADS, PEER_N_KEYS, tm), lambda i: (0, 0, i))
    return pl.pallas_call(
        _peer_sel_body,
        out_shape=(jax.ShapeDtypeStruct(shape, F32), jax.ShapeDtypeStruct(shape, F32),
                   jax.ShapeDtypeStruct(shape, BF16), jax.ShapeDtypeStruct(shape, BF16)),
        grid=(t // tm,),
        in_specs=[pl.BlockSpec((tm, d), lambda i: (i, 0)),
                  pl.BlockSpec(wq.shape, lambda i: (0, 0)),
                  pl.BlockSpec(keys.shape, lambda i: (0, 0, 0))],
        out_specs=(sblk, sblk, sblk, sblk),
        compiler_params=_params("parallel"), name="peer_select",
    )(h, wq, keys)


def _gelu_tanh(x):
    return 0.5 * x * (1.0 + jnp.tanh(math.sqrt(2.0 / math.pi) * (x + 0.044715 * (x * x * x))))


def _peer_ffn_body(n_a, n_split, h_ref, u_ref, vt_ref, cnt_ref, e1_ref, rank_ref, e2_ref, o_ref, acc_ref, wt_ref):
    j = pl.program_id(1)

    @pl.when(j == 0)
    def _():
        acc_ref[...] = jnp.zeros_like(acc_ref)

    nk = PEER_N_KEYS
    a_part = n_a // n_split
    rows = a_part * nk
    h = h_ref[...]
    acts = [_gelu_tanh(_nt_dot(u_ref[s * rows:(s + 1) * rows, :], h)).astype(BF16) for s in range(n_split)]
    for s in range(n_split):
        for al in range(a_part):
            a = s * a_part + al
            gate = None
            for hh in range(PEER_HEADS):
                cnt = cnt_ref[hh, a:a + 1, :].astype(BF16)
                e1 = e1_ref[hh, a:a + 1, :].astype(BF16)
                w = jnp.where(rank_ref[hh] < cnt, e2_ref[hh] * e1, jnp.zeros((), BF16))
                gate = w if gate is None else gate + w
            wt_ref[a * nk:(a + 1) * nk, :] = gate * acts[s][al * nk:(al + 1) * nk, :]
        acc_ref[...] += jnp.dot(vt_ref[:, s * rows:(s + 1) * rows], wt_ref[s * rows:(s + 1) * rows, :],
                                preferred_element_type=F32)

    @pl.when(j == pl.num_programs(1) - 1)
    def _():
        o_ref[...] = acc_ref[...].T


def _peer_ffn_call(h, u, vt, cnt, e1, rank, e2, tm=512, n_a=8, n_split=2):
    t, d = h.shape
    e = u.shape[0]
    te = n_a * PEER_N_KEYS
    a_blk = pl.BlockSpec((PEER_HEADS, n_a, tm), lambda i, j: (0, j, i))
    b_blk = pl.BlockSpec((PEER_HEADS, PEER_N_KEYS, tm), lambda i, j: (0, 0, i))
    return pl.pallas_call(
        functools.partial(_peer_ffn_body, n_a, n_split),
        out_shape=jax.ShapeDtypeStruct((t, d), F32),
        grid=(t // tm, e // te),
        in_specs=[pl.BlockSpec((tm, d), lambda i, j: (i, 0)),
                  pl.BlockSpec((te, d), lambda i, j: (j, 0)),
                  pl.BlockSpec((d, te), lambda i, j: (0, j)),
                  a_blk, a_blk, b_blk, b_blk],
        out_specs=pl.BlockSpec((tm, d), lambda i, j: (i, 0)),
        scratch_shapes=[pltpu.VMEM((d, tm), F32), pltpu.VMEM((te, tm), BF16)],
        compiler_params=_params("parallel", "arbitrary"), name="peer_ffn",
    )(h, u, vt, cnt, e1, rank, e2)


def _peer(h, wq, keys, u, vt):
    cnt, e1, rank, e2 = _peer_sel_call(h, wq, keys)
    return _peer_ffn_call(h, u, vt, cnt, e1, rank, e2)


def kernel(x, c, ctx, c_ctx, ada_w, ada_b, norm_g, w_in, w_out, na_rpb, ssd_conv_w, ssd_conv_b, ssd_dt_bias,
           ssd_a_log, ssd_d, ssd_norm_g, da_lambda, da_sub_g, peer_w_q, peer_sub_keys, peer_u, peer_v):
    batch, seq, d = x.shape
    n_ctx = ctx.shape[1]
    depth = ada_w.shape[0]
    xl = x.reshape(batch * seq, d)
    xc = ctx.reshape(batch * n_ctx, d)
    tm_norm = 256
    sel_lat = lambda i: i // (seq // tm_norm)
    sel_ctx = lambda i: batch
    cos_t, sin_t = _rope_tables(seq)

    o_nq, o_nk, o_nv = 0, 512, 1024
    o_dq, o_dk, o_dv = 1536, 2048, 2560
    o_z = 3072
    o_end = o_z + SSD_INNER + SSD_XBC + 2 * SSD_HEADS
    scale_a = jnp.concatenate([jnp.full((512,), NA_DIM ** -0.5, F32), jnp.ones((1536,), F32)])
    scale_b = jnp.concatenate([jnp.full((512,), DA_DIM ** -0.5, F32), jnp.ones((512,), F32)])

    cond = jnp.concatenate([c, c_ctx[None, :], jnp.zeros((8 - batch - 1, d), F32)], axis=0)
    mods = [_ada_call(cond, ada_w[layer], ada_b[layer])[:batch + 1].reshape(batch + 1, N_MOD, 1, d)
            for layer in range(depth)]
    h_l = h_c = None
    for layer in range(depth):
        need_ctx = layer < depth - 1
        lam_init = 0.8 - 0.6 * math.exp(-0.3 * layer)
        tab = [mods[layer][:, k] for k in range(N_MOD)]
        g = norm_g[layer]
        if layer == 0:
            _, h_l = _norm_call(xl, sel_lat, pre=(g[0:1], tab[0], tab[1]), tm=tm_norm)
            _, h_c = _norm_call(xc, sel_ctx, pre=(g[0:1], tab[0], tab[1]), tm=tm_norm)

        wi = w_in[layer]
        w_a = jnp.concatenate([wi[:, o_nq:o_dq], wi[:, o_dv:o_z]], axis=1).astype(BF16)
        w_b = wi[:, o_dq:o_dv].astype(BF16)
        w_c = jnp.pad(wi[:, o_z:o_end], ((0, 0), (0, SSD_IN_W - (o_end - o_z)))).astype(BF16)
        a_lat = _mm_call(h_l, w_a, BF16, colscale=scale_a, name="proj_attn")
        a_ctx = _mm_call(h_c, w_a, BF16, colscale=scale_a, name="proj_attn")
        qk_lat = _mm_call(h_l, w_b, BF16, colscale=scale_b, rope=(cos_t, sin_t), name="proj_rope")
        qk_ctx = _mm_call(h_c, w_b, BF16, colscale=scale_b, name="proj_qk_ctx")
        s_lat = _mm_call(h_l, w_c, F32, tn=SSD_IN_W // 3, name="proj_ssd")
        s_ctx = _mm_call(h_c, w_c, F32, tn=SSD_IN_W // 3, name="proj_ssd")

        lam_p = da_lambda[layer]
        g_sub = (da_sub_g[layer] * (1.0 - lam_init)).reshape(1, LANES)
        na_l = _na_call(a_lat, a_ctx, _na_bias(na_rpb[layer]), batch, seq, n_ctx)
        da_l = _da_call(qk_lat, qk_ctx, a_lat, a_ctx, lam_p, g_sub, lam_init, batch, seq, n_ctx)

        u_ssd = _ssd_prep_call(s_lat, s_ctx, ssd_conv_w[layer], ssd_conv_b[layer], batch, seq, n_ctx)
        dtt = jnp.transpose(u_ssd[:, :, SSD_INNER + SSD_XBC:SSD_INNER + SSD_XBC + 2 * SSD_HEADS], (0, 2, 1))
        ssd_o = _ssd_call(u_ssd, dtt, ssd_dt_bias[layer], ssd_a_log[layer], ssd_d[layer], ssd_norm_g[layer],
                          batch, seq, n_ctx)

        wo = w_out[layer].astype(BF16)
        wq = peer_w_q[layer].astype(BF16)
        keys = peer_sub_keys[layer].reshape(2 * PEER_HEADS, PEER_N_KEYS, PEER_QDIM // 2).astype(BF16)
        pu = peer_u[layer].astype(BF16)
        pvt = peer_v[layer].T.astype(BF16)
        nxt = layer + 1 < depth

        tm_o = 512
        lat_tiles = seq // tm_o
        mix_l = _wout_call(na_l, da_l, ssd_o, wo, lambda i: (i // lat_tiles, i % lat_tiles), tm_o)
        xl, h2_l = _norm_call(xl, sel_lat, res=(mix_l, tab[2], g[1:2]), pre=(g[2:3], tab[3], tab[4]), tm=tm_norm)
        f_l = _peer(h2_l, wq, keys, pu, pvt)
        if nxt:
            pre_n = (norm_g[layer + 1][0:1], mods[layer + 1][:, 0], mods[layer + 1][:, 1])
        else:
            pre_n = None
        xl, h_l = _norm_call(xl, sel_lat, res=(f_l, tab[5], g[3:4]), pre=pre_n, tm=tm_norm)

        if need_ctx:
            na_c, da_c = _ctx_attn_call(a_ctx, qk_ctx, lam_p, g_sub, lam_init, batch, n_ctx)
            mix_c = _wout_call(na_c, da_c, ssd_o, wo, lambda i: (i, seq // n_ctx), n_ctx)
            xc, h2_c = _norm_call(xc, sel_ctx, res=(mix_c, tab[2], g[1:2]), pre=(g[2:3], tab[3], tab[4]), tm=tm_norm)
            f_c = _peer(h2_c, wq, keys, pu, pvt)
            xc, h_c = _norm_call(xc, sel_ctx, res=(f_c, tab[5], g[3:4]), pre=pre_n, tm=tm_norm)
    return xl.reshape(batch, seq, d)
```

```python
import functools
import math

import numpy as np
import jax
import jax.numpy as jnp
from jax import lax
from jax.experimental import pallas as pl
from jax.experimental.pallas import tpu as pltpu

F32 = jnp.float32
BF16 = jnp.bfloat16
EPS = 1e-6
NEG_INF = float("-inf")

VMEM_LIMIT_BYTES = 56 * 1024 * 1024
LANES = 128

GRID_W = 64
NA_DIM = 64
NA_HEADS = 8
NA_WIN_ROWS = 8
NA_WIN_COLS = 16
DA_DIM = 64
DA_HEADS = 4
ROPE_BASE = 10000.0
SSD_INNER = 1024
SSD_HEAD_DIM = 64
SSD_HEADS = 16
SSD_GROUPS = 2
SSD_STATE = 128
SSD_CONV = 5
SSD_CHUNK = 128
SSD_XBC = SSD_INNER + 2 * SSD_GROUPS * SSD_STATE
SSD_IN_W = SSD_INNER + SSD_XBC + LANES
PEER_HEADS = 8
PEER_QDIM = 256
PEER_N_KEYS = 128
PEER_TOPK = 16
N_MOD = 6
TOKEN_TILE = 512


def _params(*sem):
    return pltpu.CompilerParams(dimension_semantics=sem, vmem_limit_bytes=VMEM_LIMIT_BYTES)


def _nt_dot(a, b):
    return lax.dot_general(a, b, (((1,), (1,)), ((), ())), preferred_element_type=F32)


def _lane_lt64(shape):
    return lax.broadcasted_iota(jnp.int32, shape, len(shape) - 1) < (LANES // 2)


def _ada_body(c_ref, w_ref, b_ref, o_ref):
    c = c_ref[...]
    sc = (c * jax.nn.sigmoid(c)).astype(BF16)
    o_ref[...] = jnp.dot(sc, w_ref[...].astype(BF16), preferred_element_type=F32) + b_ref[...]


def _ada_call(cond, w, b, tn=768):
    m, d = cond.shape
    n = w.shape[1]
    return pl.pallas_call(
        _ada_body,
        out_shape=jax.ShapeDtypeStruct((m, n), F32),
        grid=(n // tn,),
        in_specs=[pl.BlockSpec((m, d), lambda j: (0, 0)),
                  pl.BlockSpec((d, tn), lambda j: (0, j)),
                  pl.BlockSpec((1, tn), lambda j: (0, j))],
        out_specs=pl.BlockSpec((m, tn), lambda j: (0, j)),
        compiler_params=_params("arbitrary"),
        name="ada_mod",
    )(cond, w, b.reshape(1, n))


def _rms(x, g):
    return x * lax.rsqrt(jnp.mean(x * x, axis=-1, keepdims=True) + EPS) * g


def _norm_body(has_res, has_pre, *refs):
    refs = list(refs)
    x = refs.pop(0)[...]
    if has_res:
        f_ref, gate_ref, gpost_ref = refs.pop(0), refs.pop(0), refs.pop(0)
        x = x + gate_ref[0] * _rms(f_ref[...], gpost_ref[...])
    if has_pre:
        gpre_ref, shift_ref, scale_ref = refs.pop(0), refs.pop(0), refs.pop(0)
    if has_res:
        refs.pop(0)[...] = x
    if has_pre:
        h = _rms(x, gpre_ref[...]) * (1.0 + scale_ref[0]) + shift_ref[0]
        refs.pop(0)[...] = h.astype(BF16)


def _norm_call(x, sel, *, res=None, pre=None, tm=256):
    m, d = x.shape
    row = pl.BlockSpec((tm, d), lambda i: (i, 0))
    vec = pl.BlockSpec((1, d), lambda i: (0, 0))
    tab = pl.BlockSpec((1, 1, d), lambda i: (sel(i), 0, 0))
    args, specs, outs, ospecs = [x], [row], [], []
    if res is not None:
        args += list(res)
        specs += [row, tab, vec]
        outs.append(jax.ShapeDtypeStruct((m, d), F32))
        ospecs.append(row)
    if pre is not None:
        args += list(pre)
        specs += [vec, tab, tab]
        outs.append(jax.ShapeDtypeStruct((m, d), BF16))
        ospecs.append(row)
    res_out = pl.pallas_call(
        functools.partial(_norm_body, res is not None, pre is not None),
        out_shape=tuple(outs), grid=(m // tm,), in_specs=specs, out_specs=tuple(ospecs),
        compiler_params=_params("parallel"), name="norm_mod",
    )(*args)
    res_out = list(res_out)
    x_new = res_out.pop(0) if res is not None else None
    h = res_out.pop(0) if pre is not None else None
    return x_new, h


def _rope_partner(x):
    lane = lax.broadcasted_iota(jnp.int32, x.shape, 1)
    first = (lane % 32) < 16
    return jnp.where(first, pltpu.roll(x, LANES - 16, 1), pltpu.roll(x, 16, 1))


def _mm_body(epi, h_ref, w_ref, *rest):
    acc = jnp.dot(h_ref[...], w_ref[...], preferred_element_type=F32)
    if epi == "plain":
        (o_ref,) = rest
    elif epi == "scale":
        cs_ref, o_ref = rest
        acc = acc * cs_ref[...]
    else:
        cos_ref, sin_ref, cs_ref, o_ref = rest
        cos = cos_ref[...]
        sin = sin_ref[...]
        n = acc.shape[1]
        blocks = []
        for c in range(n // LANES):
            xb = acc[:, c * LANES:(c + 1) * LANES]
            blocks.append(xb * cos + _rope_partner(xb) * sin)
        acc = jnp.concatenate(blocks, axis=1) * cs_ref[...]
    o_ref[...] = acc.astype(o_ref.dtype)


def _mm_call(h, w, out_dtype, *, colscale=None, rope=None, tm=512, tn=None, name="proj"):
    m, k = h.shape
    n = w.shape[1]
    tn = n if tn is None else tn
    args = [h, w]
    specs = [pl.BlockSpec((tm, k), lambda i, j: (i, 0)),
             pl.BlockSpec((k, tn), lambda i, j: (0, j))]
    epi = "plain"
    if rope is not None:
        cos_t, sin_t = rope
        nt = cos_t.shape[0] // tm
        args += [cos_t, sin_t]
        specs += [pl.BlockSpec((tm, LANES), lambda i, j: (i % nt, 0))] * 2
        epi = "rope"
    if colscale is not None:
        args.append(colscale.reshape(1, n))
        specs.append(pl.BlockSpec((1, tn), lambda i, j: (0, j)))
        epi = "scale" if epi == "plain" else epi
    return pl.pallas_call(
        functools.partial(_mm_body, epi),
        out_shape=jax.ShapeDtypeStruct((m, n), out_dtype),
        grid=(m // tm, n // tn), in_specs=specs,
        out_specs=pl.BlockSpec((tm, tn), lambda i, j: (i, j)),
        compiler_params=_params("parallel", "arbitrary"), name=name,
    )(*args)


def _rope_tables(seq):
    quarter = DA_DIM // 4
    inv_freq = 1.0 / (ROPE_BASE ** (jnp.arange(quarter, dtype=F32) / quarter))
    t = jnp.arange(seq)
    ang_r = (t // GRID_W).astype(F32)[:, None] * inv_freq
    ang_c = (t % GRID_W).astype(F32)[:, None] * inv_freq

    def unit(ang):
        return (jnp.concatenate([jnp.cos(ang), jnp.cos(ang)], -1),
                jnp.concatenate([-jnp.sin(ang), jnp.sin(ang)], -1))

    cr, sr = unit(ang_r)
    cc, sc = unit(ang_c)
    cos64 = jnp.concatenate([cr, cc], -1)
    sin64 = jnp.concatenate([sr, sc], -1)
    return jnp.tile(cos64, (1, 2)), jnp.tile(sin64, (1, 2))


def _stack_heads(q2):
    lo = _lane_lt64(q2.shape)
    zero = jnp.zeros_like(q2)
    return jnp.concatenate([jnp.where(lo, q2, zero), jnp.where(lo, zero, q2)], axis=0)


def _unstack_heads(o):
    r = o.shape[0] // 2
    return jnp.where(_lane_lt64((r, o.shape[1])), o[:r], o[r:])


NA_ROW_GROUP = 8


def _na_body(rows, q_ref, k_ref, v_ref, kc_ref, vc_ref, bias_ref, o_ref):
    kc = kc_ref[...]
    vc = vc_ref[...]
    kh = NA_WIN_ROWS

    def scores(r):
        rs = jnp.clip(r - kh // 2, 0, rows - kh)
        q0 = pl.multiple_of(r * GRID_W, GRID_W)
        k0 = pl.multiple_of(rs * GRID_W, GRID_W)
        qs = _stack_heads(q_ref[pl.ds(q0, GRID_W), :])
        s_loc = _nt_dot(qs, k_ref[pl.ds(k0, kh * GRID_W), :]) + bias_ref[0, r - rs]
        return q0, k0, s_loc, _nt_dot(qs, kc)

    def finish(q0, k0, p_loc, p_ctx, l):
        o = (jnp.dot(p_loc, v_ref[pl.ds(k0, kh * GRID_W), :], preferred_element_type=F32)
             + jnp.dot(p_ctx, vc, preferred_element_type=F32)) / l
        o_ref[pl.ds(q0, GRID_W), :] = _unstack_heads(o).astype(o_ref.dtype)

    def row_group(gi, carry):
        nxt = scores(gi * NA_ROW_GROUP)
        pending = None
        for j in range(NA_ROW_GROUP):
            q0, k0, s_loc, s_ctx = nxt
            if j + 1 < NA_ROW_GROUP:
                nxt = scores(gi * NA_ROW_GROUP + j + 1)
            m = jnp.maximum(jnp.max(s_loc, axis=-1, keepdims=True), jnp.max(s_ctx, axis=-1, keepdims=True))
            p_loc = jnp.exp(s_loc - m)
            p_ctx = jnp.exp(s_ctx - m)
            l = jnp.sum(p_loc, axis=-1, keepdims=True) + jnp.sum(p_ctx, axis=-1, keepdims=True)
            if pending is not None:
                finish(*pending)
            pending = (q0, k0, p_loc.astype(BF16), p_ctx.astype(BF16), l)
        finish(*pending)
        return carry

    lax.fori_loop(0, rows // NA_ROW_GROUP, row_group, 0)


def _na_bias(rpb):
    kh, w = NA_WIN_ROWS, GRID_W
    c = np.arange(w)
    wstart = np.clip(c - NA_WIN_COLS // 2, 0, w - NA_WIN_COLS)
    kc = np.arange(w)
    valid = (kc[None, :] >= wstart[:, None]) & (kc[None, :] < wstart[:, None] + NA_WIN_COLS)
    cidx = np.clip(kc[None, :] - c[:, None] + NA_WIN_COLS - 1, 0, 2 * NA_WIN_COLS - 2)
    typ = np.arange(kh)
    ridx = np.arange(kh)[None, :] - typ[:, None] + NA_WIN_ROWS - 1
    pick_r = (ridx[:, :, None] == np.arange(2 * NA_WIN_ROWS - 1)).astype(np.float32)
    pick_c = ((cidx[:, :, None] == np.arange(2 * NA_WIN_COLS - 1)) & valid[:, :, None]).astype(np.float32)
    b = jnp.einsum('hrc,tjr,qkc->htqjk', rpb.astype(F32), pick_r, pick_c, precision=lax.Precision.HIGHEST)
    b = b + np.where(valid, 0.0, NEG_INF).astype(np.float32)[None, None, :, None, :]
    b = b.reshape(NA_HEADS // 2, 2, kh, w, kh * w)
    return jnp.transpose(b, (0, 2, 1, 3, 4)).reshape(NA_HEADS // 2, kh, 2 * w, kh * w).astype(F32)


def _na_call(a_lat, a_ctx, bias, batch, seq, n_ctx):
    rows = seq // GRID_W
    pairs = NA_HEADS // 2
    lat = lambda c0: pl.BlockSpec((seq, LANES), lambda b, p: (b, c0 + p))
    ctx = lambda c0: pl.BlockSpec((n_ctx, LANES), lambda b, p: (b, c0 + p))
    return pl.pallas_call(
        functools.partial(_na_body, rows),
        out_shape=jax.ShapeDtypeStruct((batch * seq, NA_HEADS * NA_DIM), BF16),
        grid=(batch, pairs),
        in_specs=[lat(0), lat(pairs), lat(2 * pairs), ctx(pairs), ctx(2 * pairs),
                  pl.BlockSpec((1,) + bias.shape[1:], lambda b, p: (p, 0, 0, 0))],
        out_specs=pl.BlockSpec((seq, LANES), lambda b, p: (b, p)),
        compiler_params=_params("parallel", "parallel"), name="na_attn",
    )(a_lat, a_lat, a_lat, a_ctx, a_ctx, bias)


def _da_lambda(lam_ref, lam_init):
    lf = lam_ref[...]
    a = jnp.sum(lf[0:1] * lf[1:2], axis=-1, keepdims=True)
    b = jnp.sum(lf[2:3] * lf[3:4], axis=-1, keepdims=True)
    return jnp.exp(a) - jnp.exp(b) + lam_init


def _da_finish(acc, l, lam, g):
    tq = acc.shape[0] // 2
    o = acc / l
    d = o[:tq] - lam * o[tq:]
    return d * lax.rsqrt(jnp.mean(d * d, axis=-1, keepdims=True) + EPS) * g


def _flash_step(qs, k, v, m, l, acc):
    s = _nt_dot(qs, k)
    m_new = jnp.maximum(m, jnp.max(s, axis=-1, keepdims=True))
    alpha = jnp.exp(m - m_new)
    p = jnp.exp(s - m_new)
    l = alpha * l + jnp.sum(p, axis=-1, keepdims=True)
    acc = alpha * acc + jnp.dot(p.astype(BF16), v, preferred_element_type=F32)
    return m_new, l, acc


def _da_body(lam_init, n_chunks, tk, q_ref, k_ref, v_ref, kc_ref, vc_ref, lam_ref, g_ref, o_ref):
    qs = _stack_heads(q_ref[...])
    r = qs.shape[0]
    chunks = [(k_ref, v_ref, c * tk, tk) for c in range(n_chunks)] + [(kc_ref, vc_ref, 0, kc_ref.shape[0])]

    def scores(i):
        kr, _, k0, n = chunks[i]
        return _nt_dot(qs, kr[k0:k0 + n, :])

    m = jnp.full((r, 1), NEG_INF, F32)
    l = jnp.zeros((r, 1), F32)
    acc = jnp.zeros((r, LANES), F32)
    s_next = scores(0)
    pending = None
    for i in range(len(chunks)):
        s = s_next
        if i + 1 < len(chunks):
            s_next = scores(i + 1)
        m_new = jnp.maximum(m, jnp.max(s, axis=-1, keepdims=True))
        alpha = jnp.exp(m - m_new)
        p = jnp.exp(s - m_new)
        l = alpha * l + jnp.sum(p, axis=-1, keepdims=True)
        m = m_new
        if pending is not None:
            p_prev, alpha_prev, (_, vr, k0, n) = pending
            acc = alpha_prev * acc + jnp.dot(p_prev, vr[k0:k0 + n, :], preferred_element_type=F32)
        pending = (p.astype(BF16), alpha, chunks[i])
    p_prev, alpha_prev, (_, vr, k0, n) = pending
    acc = alpha_prev * acc + jnp.dot(p_prev, vr[k0:k0 + n, :], preferred_element_type=F32)
    o_ref[...] = _da_finish(acc, l, _da_lambda(lam_ref, lam_init), g_ref[...]).astype(o_ref.dtype)


def _da_call(qk_lat, qk_ctx, a_lat, a_ctx, lam_p, g_scaled, lam_init, batch, seq, n_ctx, tq=256, tk=512):
    nq = seq // tq
    h = DA_HEADS
    return pl.pallas_call(
        functools.partial(_da_body, lam_init, seq // tk, tk),
        out_shape=jax.ShapeDtypeStruct((batch * seq, h * 2 * DA_DIM), BF16),
        grid=(batch, h, nq),
        in_specs=[pl.BlockSpec((tq, LANES), lambda b, hh, i: (b * nq + i, hh)),
                  pl.BlockSpec((seq, LANES), lambda b, hh, i: (b, h + hh)),
                  pl.BlockSpec((seq, LANES), lambda b, hh, i: (b, 3 * h + hh)),
                  pl.BlockSpec((n_ctx, LANES), lambda b, hh, i: (b, h + hh)),
                  pl.BlockSpec((n_ctx, LANES), lambda b, hh, i: (b, 3 * h + hh)),
                  pl.BlockSpec(lam_p.shape, lambda b, hh, i: (0, 0)),
                  pl.BlockSpec((1, LANES), lambda b, hh, i: (0, 0))],
        out_specs=pl.BlockSpec((tq, LANES), lambda b, hh, i: (b * nq + i, hh)),
        compiler_params=_params("parallel", "parallel", "arbitrary"), name="da_attn",
    )(qk_lat, qk_lat, a_lat, qk_ctx, a_ctx, lam_p, g_scaled)


def _ctx_attn_body(lam_init, a_ref, qk_ref, lam_ref, g_ref, na_ref, da_ref):
    pairs = NA_HEADS // 2
    for p in range(pairs):
        qs = _stack_heads(a_ref[:, p * LANES:(p + 1) * LANES])
        k = a_ref[:, (pairs + p) * LANES:(pairs + p + 1) * LANES]
        v = a_ref[:, (2 * pairs + p) * LANES:(2 * pairs + p + 1) * LANES]
        s = _nt_dot(qs, k)
        e = jnp.exp(s - jnp.max(s, axis=-1, keepdims=True))
        o = jnp.dot(e.astype(BF16), v, preferred_element_type=F32) / jnp.sum(e, axis=-1, keepdims=True)
        na_ref[:, p * LANES:(p + 1) * LANES] = _unstack_heads(o).astype(na_ref.dtype)
    lam = _da_lambda(lam_ref, lam_init)
    for hh in range(DA_HEADS):
        qs = _stack_heads(qk_ref[:, hh * LANES:(hh + 1) * LANES])
        k = qk_ref[:, (DA_HEADS + hh) * LANES:(DA_HEADS + hh + 1) * LANES]
        v = a_ref[:, (3 * pairs + hh) * LANES:(3 * pairs + hh + 1) * LANES]
        r = qs.shape[0]
        init = (jnp.full((r, 1), NEG_INF, F32), jnp.zeros((r, 1), F32), jnp.zeros((r, LANES), F32))
        m, l, acc = _flash_step(qs, k, v, *init)
        da_ref[:, hh * LANES:(hh + 1) * LANES] = _da_finish(acc, l, lam, g_ref[...]).astype(da_ref.dtype)


def _ctx_attn_call(a_ctx, qk_ctx, lam_p, g_scaled, lam_init, batch, n_ctx):
    return pl.pallas_call(
        functools.partial(_ctx_attn_body, lam_init),
        out_shape=(jax.ShapeDtypeStruct((batch * n_ctx, NA_HEADS * NA_DIM), BF16),
                   jax.ShapeDtypeStruct((batch * n_ctx, DA_HEADS * 2 * DA_DIM), BF16)),
        grid=(batch,),
        in_specs=[pl.BlockSpec((n_ctx, a_ctx.shape[1]), lambda b: (b, 0)),
                  pl.BlockSpec((n_ctx, qk_ctx.shape[1]), lambda b: (b, 0)),
                  pl.BlockSpec(lam_p.shape, lambda b: (0, 0)),
                  pl.BlockSpec((1, LANES), lambda b: (0, 0))],
        out_specs=(pl.BlockSpec((n_ctx, NA_HEADS * NA_DIM), lambda b: (b, 0)),
                   pl.BlockSpec((n_ctx, DA_HEADS * 2 * DA_DIM), lambda b: (b, 0))),
        compiler_params=_params("parallel"), name="ctx_attn",
    )(a_ctx, qk_ctx, lam_p, g_scaled)


def _conv_seg(u, w, bias):
    n = u.shape[0]
    t = lax.broadcasted_iota(jnp.int32, u.shape, 0)
    acc = jnp.zeros_like(u) + bias
    for kk in range(SSD_CONV):
        d = kk - SSD_CONV // 2
        if d == 0:
            sh = u
        else:
            sh = pltpu.roll(u, (-d) % n, 0)
            sh = jnp.where((t + d >= 0) & (t + d < n), sh, 0.0)
        acc = acc + sh * w[kk:kk + 1, :]
    return acc * jax.nn.sigmoid(acc)


def _ssd_prep_body(seq, n_zc, lat_ref, ctx_ref, w_ref, b_ref, o_ref):
    j = pl.program_id(1)
    is_conv = (j >= n_zc) & (j < n_zc + SSD_XBC // LANES)

    @pl.when(is_conv)
    def _():
        w = w_ref[...]
        bias = b_ref[...]
        o_ref[0, :seq, :] = _conv_seg(lat_ref[...], w, bias)
        o_ref[0, seq:, :] = _conv_seg(ctx_ref[...], w, bias)

    @pl.when(jnp.logical_not(is_conv))
    def _():
        o_ref[0, :seq, :] = lat_ref[...]
        o_ref[0, seq:, :] = ctx_ref[...]


def _ssd_prep_call(s_lat, s_ctx, conv_w, conv_b, batch, seq, n_ctx):
    n_zc = SSD_INNER // LANES
    ncol = SSD_IN_W // LANES
    n_cw = SSD_XBC // LANES
    wcol = lambda b, j: (0, jnp.clip(j - n_zc, 0, n_cw - 1))
    return pl.pallas_call(
        functools.partial(_ssd_prep_body, seq, n_zc),
        out_shape=jax.ShapeDtypeStruct((batch, seq + n_ctx, SSD_IN_W), F32),
        grid=(batch, ncol),
        in_specs=[pl.BlockSpec((seq, LANES), lambda b, j: (b, j)),
                  pl.BlockSpec((n_ctx, LANES), lambda b, j: (b, j)),
                  pl.BlockSpec((SSD_CONV, LANES), wcol),
                  pl.BlockSpec((1, LANES), wcol)],
        out_specs=pl.BlockSpec((1, seq + n_ctx, LANES), lambda b, j: (b, 0, j)),
        compiler_params=_params("parallel", "parallel"), name="ssd_prep",
    )(s_lat, s_ctx, conv_w, conv_b.reshape(1, -1))


def _softplus(x):
    return jnp.maximum(x, 0.0) + jnp.log1p(jnp.exp(-jnp.abs(x)))


def _ssd_chunk(direction, xs_ref, bc_ref, dt_ref, dtt_ref, pr_ref, pc_ref, dexp_ref, s_ref):
    q = SSD_CHUNK
    nh = SSD_HEADS
    hi = lax.Precision.HIGHEST
    d0 = direction * nh
    dt = _softplus(dt_ref[0][:, d0:d0 + nh] + pr_ref[0:1, :])
    a = dt * (-jnp.exp(pr_ref[1:2, :]))
    dtt = _softplus(dtt_ref[0][d0:d0 + nh, :] + pc_ref[:, 0:1])
    at = dtt * (-jnp.exp(pc_ref[:, 1:2]))
    ii = lax.broadcasted_iota(jnp.int32, (q, q), 0)
    jj = lax.broadcasted_iota(jnp.int32, (q, q), 1)
    tri = (jj <= ii) if direction == 0 else (jj >= ii)
    tri_f = tri.astype(F32)
    cs = jnp.dot(tri_f, a, precision=hi, preferred_element_type=F32)
    cst = lax.dot_general(at, tri_f, (((1,), (1,)), ((), ())), precision=hi,
                          preferred_element_type=F32)
    edge = q - 1 if direction == 0 else 0
    tot_t = cst[:, edge:edge + 1]
    dst = jnp.exp(tot_t - cst)
    etot = jnp.exp(cs[edge:edge + 1, :])

    lo = _lane_lt64((q, LANES))
    groups = []
    for g in range(SSD_GROUPS):
        b_g = bc_ref[0][:, g * SSD_STATE:(g + 1) * SSD_STATE]
        c_g = bc_ref[0][:, (SSD_GROUPS + g) * SSD_STATE:(SSD_GROUPS + g + 1) * SSD_STATE]
        groups.append((c_g, _nt_dot(c_g.astype(BF16), b_g.astype(BF16)), b_g.T))
    ys = []
    for pair in range(nh // 2):
        c_g, cb, bt_g = groups[(2 * pair) // (nh // SSD_GROUPS)]
        xs_p = xs_ref[0][:, pair * LANES:(pair + 1) * LANES]
        r0, r1 = 2 * pair, 2 * pair + 1
        dt_p = jnp.where(lo, jnp.broadcast_to(dt[:, r0:r0 + 1], (q, LANES)),
                         jnp.broadcast_to(dt[:, r1:r1 + 1], (q, LANES)))
        xdt = (xs_p * dt_p).astype(BF16)
        zero_b = jnp.zeros_like(xdt)
        xdt_bd = jnp.concatenate([jnp.where(lo, xdt, zero_b), jnp.where(lo, zero_b, xdt)], axis=0)
        s_p = s_ref[pair]
        s_b = s_p.astype(BF16)
        s_bd = jnp.concatenate([jnp.where(lo, s_b, zero_b), jnp.where(lo, zero_b, s_b)], axis=0)
        lhs, lhs_state, bts = [], [], []
        for r in (r0, r1):
            colb = jnp.broadcast_to(cs[:, r:r + 1], (q, q))
            rowb = jnp.broadcast_to(cst[r:r + 1, :], (q, q))
            decay = jnp.exp(jnp.where(tri, colb - rowb, NEG_INF))
            lhs.append((cb * decay).astype(BF16))
            lhs_state.append((c_g * jnp.exp(colb)).astype(BF16))
            bts.append((bt_g * jnp.broadcast_to(dst[r:r + 1, :], (SSD_STATE, q))).astype(BF16))
        y = jnp.dot(jnp.concatenate(lhs + lhs_state, axis=1), jnp.concatenate([xdt_bd, s_bd], axis=0),
                    preferred_element_type=F32)
        y = y + xs_p * dexp_ref[:, pair * LANES:(pair + 1) * LANES]
        dec_p = jnp.where(_lane_lt64((1, LANES)), jnp.broadcast_to(etot[:, r0:r0 + 1], (1, LANES)),
                          jnp.broadcast_to(etot[:, r1:r1 + 1], (1, LANES)))
        s_ref[pair] = s_p * dec_p + jnp.dot(jnp.concatenate(bts, axis=1), xdt_bd, preferred_element_type=F32)
        ys.append(y)
    return jnp.concatenate(ys, axis=1)


def _ssd_fwd_body(xs_ref, bc_ref, dt_ref, dtt_ref, pr_ref, pc_ref, dexp_ref, y_ref, s_ref):
    @pl.when(pl.program_id(1) == 0)
    def _():
        s_ref[...] = jnp.zeros_like(s_ref)

    y_ref[0] = _ssd_chunk(0, xs_ref, bc_ref, dt_ref, dtt_ref, pr_ref, pc_ref, dexp_ref, s_ref)


def _ssd_bwd_body(xs_ref, bc_ref, dt_ref, dtt_ref, pr_ref, pc_ref, dexp_ref, yf_ref, z_ref, g_ref, o_ref, s_ref):
    @pl.when(pl.program_id(1) == 0)
    def _():
        s_ref[...] = jnp.zeros_like(s_ref)

    y = _ssd_chunk(1, xs_ref, bc_ref, dt_ref, dtt_ref, pr_ref, pc_ref, dexp_ref, s_ref) + yf_ref[0]
    z = z_ref[0]
    gated = y * (z * jax.nn.sigmoid(z))
    gw = SSD_INNER // SSD_GROUPS
    parts = []
    for g in range(SSD_GROUPS):
        gg = gated[:, g * gw:(g + 1) * gw]
        parts.append(gg * lax.rsqrt(jnp.mean(gg * gg, axis=-1, keepdims=True) + EPS))
    o_ref[0] = (jnp.concatenate(parts, axis=1) * g_ref[...]).astype(o_ref.dtype)


def _ssd_call(u, dtt, dt_bias, a_log, d_skip, norm_g, batch, seq, n_ctx):
    q = SSD_CHUNK
    n_lat, n_c = seq // q, n_ctx // q
    n_ch = n_lat + n_c
    nh = SSD_HEADS
    fwd_chunk = lambda s: jnp.where(s < n_c, n_lat + s, s - n_c)
    bwd_chunk = lambda s: n_ch - 1 - s
    xs_blk = SSD_INNER // SSD_INNER
    bc_blk = (2 * SSD_INNER) // (2 * SSD_GROUPS * SSD_STATE)
    dt_blk = (SSD_INNER + SSD_XBC) // LANES
    dexp = [jnp.repeat(d_skip[d], SSD_HEAD_DIM).reshape(1, SSD_INNER) for d in range(2)]
    prow = [jnp.stack([dt_bias[d], a_log[d]], axis=0) for d in range(2)]
    pcol = [jnp.stack([dt_bias[d], a_log[d]], axis=1) for d in range(2)]

    def specs(chunk):
        return [pl.BlockSpec((1, q, SSD_INNER), lambda b, s: (b, chunk(s), xs_blk)),
                pl.BlockSpec((1, q, 2 * SSD_GROUPS * SSD_STATE), lambda b, s: (b, chunk(s), bc_blk)),
                pl.BlockSpec((1, q, LANES), lambda b, s: (b, chunk(s), dt_blk)),
                pl.BlockSpec((1, 2 * nh, q), lambda b, s: (b, 0, chunk(s))),
                pl.BlockSpec((2, nh), lambda b, s: (0, 0)),
                pl.BlockSpec((nh, 2), lambda b, s: (0, 0)),
                pl.BlockSpec((1, SSD_INNER), lambda b, s: (0, 0))]

    state = pltpu.VMEM((nh // 2, SSD_STATE, LANES), F32)
    yblk = lambda chunk: pl.BlockSpec((1, q, SSD_INNER), lambda b, s: (b, chunk(s), 0))
    y_f = pl.pallas_call(
        _ssd_fwd_body,
        out_shape=jax.ShapeDtypeStruct((batch, seq + n_ctx, SSD_INNER), F32),
        grid=(batch, n_ch), in_specs=specs(fwd_chunk), out_specs=yblk(fwd_chunk),
        scratch_shapes=[state], compiler_params=_params("parallel", "arbitrary"), name="ssd_fwd",
    )(u, u, u, dtt, prow[0], pcol[0], dexp[0])
    return pl.pallas_call(
        _ssd_bwd_body,
        out_shape=jax.ShapeDtypeStruct((batch, seq + n_ctx, SSD_INNER), BF16),
        grid=(batch, n_ch),
        in_specs=specs(bwd_chunk) + [yblk(bwd_chunk), yblk(bwd_chunk),
                                     pl.BlockSpec((1, SSD_INNER), lambda b, s: (0, 0))],
        out_specs=yblk(bwd_chunk),
        scratch_shapes=[state], compiler_params=_params("parallel", "arbitrary"), name="ssd_bwd",
    )(u, u, u, dtt, prow[1], pcol[1], dexp[1], y_f, u, norm_g.reshape(1, SSD_INNER))


def _residual_norm(x, f, gate, g_post, pre):
    x = x + gate * _rms(f, g_post)
    if pre is None:
        return x, None
    g_pre, shift, scale = pre
    return x, (_rms(x, g_pre) * (1.0 + scale) + shift).astype(BF16)


def _wout_body(na_ref, da_ref, ssd_ref, w_ref, x_ref, gate_ref, gpost_ref, gpre_ref, shift_ref, scale_ref,
               xo_ref, h_ref):
    k1 = na_ref.shape[1]
    k2 = k1 + da_ref.shape[1]
    acc = jnp.dot(na_ref[...], w_ref[:k1, :], preferred_element_type=F32)
    acc = acc + jnp.dot(da_ref[...], w_ref[k1:k2, :], preferred_element_type=F32)
    acc = acc + jnp.dot(ssd_ref[0], w_ref[k2:, :], preferred_element_type=F32)
    xo_ref[...], h_ref[...] = _residual_norm(x_ref[...], acc, gate_ref[0], gpost_ref[...],
                                             (gpre_ref[...], shift_ref[0], scale_ref[0]))


def _wout_call(na, da, ssd, w, x, sel, res, pre, ssd_tile, tm):
    m = na.shape[0]
    d = w.shape[1]
    row = pl.BlockSpec((tm, d), lambda i: (i, 0))
    vec = pl.BlockSpec((1, d), lambda i: (0, 0))
    tab = pl.BlockSpec((1, 1, d), lambda i: (sel(i), 0, 0))
    return pl.pallas_call(
        _wout_body,
        out_shape=(jax.ShapeDtypeStruct((m, d), F32), jax.ShapeDtypeStruct((m, d), BF16)),
        grid=(m // tm,),
        in_specs=[pl.BlockSpec((tm, na.shape[1]), lambda i: (i, 0)),
                  pl.BlockSpec((tm, da.shape[1]), lambda i: (i, 0)),
                  pl.BlockSpec((1, tm, ssd.shape[2]), lambda i: ssd_tile(i) + (0,)),
                  pl.BlockSpec(w.shape, lambda i: (0, 0)),
                  row, tab, vec, vec, tab, tab],
        out_specs=(row, row),
        compiler_params=_params("parallel"), name="w_out",
    )(na, da, ssd, w, x, res[0], res[1], *pre)


def _top_vals(x, k, with_rank=False):
    vals = []
    rank = jnp.full(x.shape, float(k), F32) if with_rank else None
    for i in range(k):
        m = jnp.max(x, axis=0, keepdims=True)
        vals.append(m)
        hit = x == m
        if with_rank:
            rank = jnp.where(hit, float(i), rank)
        x = jnp.where(hit, NEG_INF, x)
    vals = jnp.concatenate(vals, axis=0)
    return (vals, rank) if with_rank else vals


def _pair_candidates(v1, v2):
    k = PEER_TOPK
    half = k // 2
    row = lax.broadcasted_iota(jnp.int32, (half, v1.shape[1]), 0)
    parts = [v2 + v1[0:1, :]]
    for i in range(1, half):
        parts.append(jnp.where(row < k // (i + 1), v2[:half] + v1[i:i + 1, :], NEG_INF))
    parts.append(v1[half:] + v2[0:1, :])
    return jnp.concatenate(parts, axis=0)


def _peer_sel_body(h_ref, wq_ref, keys_ref, cnt_ref, e1_ref, rank_ref, e2_ref):
    qd = PEER_QDIM // 2
    k = PEER_TOPK
    qv = jnp.dot(h_ref[...], wq_ref[...], preferred_element_type=F32).astype(BF16)
    for hh in range(PEER_HEADS):
        c0 = hh * PEER_QDIM
        s1 = _nt_dot(keys_ref[2 * hh], qv[:, c0:c0 + qd])
        s2 = _nt_dot(keys_ref[2 * hh + 1], qv[:, c0 + qd:c0 + 2 * qd])
        v1 = _top_vals(s1, k)
        v2, rank = _top_vals(s2, k, with_rank=True)
        top = _top_vals(_pair_candidates(v1, v2), k)
        thr = top[k - 1:k, :]
        z = jnp.sum(jnp.exp(top - top[0:1, :]), axis=0, keepdims=True)
        cnt = jnp.zeros_like(s1)
        for r in range(k):
            cnt = cnt + jnp.where(s1 + v2[r:r + 1, :] >= thr, 1.0, 0.0)
        cnt_ref[hh] = cnt
        rank_ref[hh] = rank.astype(BF16)
        e1_ref[hh] = jnp.exp(s1 - v1[0:1, :])
        e2_ref[hh] = (jnp.exp(s2 - v2[0:1, :]) / z).astype(BF16)


def _peer_sel_call(h, wq, keys, tm=256):
    t, d = h.shape
    shape = (PEER_HEADS, PEER_N_KEYS, t)
    sblk = pl.BlockSpec((PEER_HEADS, PEER_N_KEYS, tm), lambda i: (0, 0, i))
    return pl.pallas_call(
        _peer_sel_body,
        out_shape=(jax.ShapeDtypeStruct(shape, F32), jax.ShapeDtypeStruct(shape, F32),
                   jax.ShapeDtypeStruct(shape, BF16), jax.ShapeDtypeStruct(shape, BF16)),
        grid=(t // tm,),
        in_specs=[pl.BlockSpec((tm, d), lambda i: (i, 0)),
                  pl.BlockSpec(wq.shape, lambda i: (0, 0)),
                  pl.BlockSpec(keys.shape, lambda i: (0, 0, 0))],
        out_specs=(sblk, sblk, sblk, sblk),
        compiler_params=_params("parallel"), name="peer_select",
    )(h, wq, keys)


def _gelu_tanh(x):
    return 0.5 * x * (1.0 + jnp.tanh(math.sqrt(2.0 / math.pi) * (x + 0.044715 * (x * x * x))))


def _peer_ffn_body(n_a, n_split, has_pre, h_ref, u_ref, vt_ref, cnt_ref, e1_ref, rank_ref, e2_ref,
                   x_ref, gate_ref, gpost_ref, *rest):
    if has_pre:
        gpre_ref, shift_ref, scale_ref, xo_ref, ho_ref, acc_ref, wt_ref = rest
    else:
        xo_ref, acc_ref, wt_ref = rest
    j = pl.program_id(1)

    @pl.when(j == 0)
    def _():
        acc_ref[...] = jnp.zeros_like(acc_ref)

    nk = PEER_N_KEYS
    a_part = n_a // n_split
    rows = a_part * nk
    h = h_ref[...]

    def act(s):
        return _gelu_tanh(_nt_dot(u_ref[s * rows:(s + 1) * rows, :], h)).astype(BF16)

    act_next = act(0)
    for s in range(n_split):
        act_s = act_next
        if s + 1 < n_split:
            act_next = act(s + 1)
        for al in range(a_part):
            a = s * a_part + al
            gate = None
            for hh in range(PEER_HEADS):
                cnt = cnt_ref[hh, a:a + 1, :].astype(BF16)
                e1 = e1_ref[hh, a:a + 1, :].astype(BF16)
                w = jnp.where(rank_ref[hh] < cnt, e2_ref[hh] * e1, jnp.zeros((), BF16))
                gate = w if gate is None else gate + w
            wt_ref[a * nk:(a + 1) * nk, :] = gate * act_s[al * nk:(al + 1) * nk, :]
        acc_ref[...] += jnp.dot(vt_ref[:, s * rows:(s + 1) * rows], wt_ref[s * rows:(s + 1) * rows, :],
                                preferred_element_type=F32)

    @pl.when(j == pl.num_programs(1) - 1)
    def _():
        pre = (gpre_ref[...], shift_ref[0], scale_ref[0]) if has_pre else None
        x_new, h_next = _residual_norm(x_ref[...], acc_ref[...].T, gate_ref[0], gpost_ref[...], pre)
        xo_ref[...] = x_new
        if has_pre:
            ho_ref[...] = h_next


def _peer_ffn_call(h, u, vt, cnt, e1, rank, e2, x, sel, res, pre, tm=512, n_a=8, n_split=2):
    t, d = h.shape
    e = u.shape[0]
    te = n_a * PEER_N_KEYS
    a_blk = pl.BlockSpec((PEER_HEADS, n_a, tm), lambda i, j: (0, j, i))
    b_blk = pl.BlockSpec((PEER_HEADS, PEER_N_KEYS, tm), lambda i, j: (0, 0, i))
    row = pl.BlockSpec((tm, d), lambda i, j: (i, 0))
    vec = pl.BlockSpec((1, d), lambda i, j: (0, 0))
    tab = pl.BlockSpec((1, 1, d), lambda i, j: (sel(i), 0, 0))
    x_row = pl.BlockSpec((tm, d), lambda i, j: (i, 0), pipeline_mode=pl.Buffered(1))
    args = [h, u, vt, cnt, e1, rank, e2, x, res[0], res[1]]
    specs = [row, pl.BlockSpec((te, d), lambda i, j: (j, 0)), pl.BlockSpec((d, te), lambda i, j: (0, j)),
             a_blk, a_blk, b_blk, b_blk, x_row, tab, vec]
    outs = [jax.ShapeDtypeStruct((t, d), F32)]
    ospecs = [row]
    if pre is not None:
        args += list(pre)
        specs += [vec, tab, tab]
        outs.append(jax.ShapeDtypeStruct((t, d), BF16))
        ospecs.append(row)
    out = pl.pallas_call(
        functools.partial(_peer_ffn_body, n_a, n_split, pre is not None),
        out_shape=tuple(outs), grid=(t // tm, e // te), in_specs=specs, out_specs=tuple(ospecs),
        scratch_shapes=[pltpu.VMEM((d, tm), F32), pltpu.VMEM((te, tm), BF16)],
        compiler_params=_params("parallel", "arbitrary"), name="peer_ffn",
    )(*args)
    return (out[0], out[1]) if pre is not None else (out[0], None)


def _peer(h, wq, keys, u, vt, x, sel, res, pre):
    cnt, e1, rank, e2 = _peer_sel_call(h, wq, keys)
    return _peer_ffn_call(h, u, vt, cnt, e1, rank, e2, x, sel, res, pre)


def kernel(x, c, ctx, c_ctx, ada_w, ada_b, norm_g, w_in, w_out, na_rpb, ssd_conv_w, ssd_conv_b, ssd_dt_bias,
           ssd_a_log, ssd_d, ssd_norm_g, da_lambda, da_sub_g, peer_w_q, peer_sub_keys, peer_u, peer_v):
    batch, seq, d = x.shape
    n_ctx = ctx.shape[1]
    depth = ada_w.shape[0]
    xl = x.reshape(batch * seq, d)
    xc = ctx.reshape(batch * n_ctx, d)
    tm_norm = 256
    sel_lat = lambda i: i // (seq // tm_norm)
    sel_ctx = lambda i: batch
    cos_t, sin_t = _rope_tables(seq)

    o_nq, o_nk, o_nv = 0, 512, 1024
    o_dq, o_dk, o_dv = 1536, 2048, 2560
    o_z = 3072
    o_end = o_z + SSD_INNER + SSD_XBC + 2 * SSD_HEADS
    scale_a = jnp.concatenate([jnp.full((512,), NA_DIM ** -0.5, F32), jnp.ones((1536,), F32)])
    scale_b = jnp.concatenate([jnp.full((512,), DA_DIM ** -0.5, F32), jnp.ones((512,), F32)])

    cond = jnp.concatenate([c, c_ctx[None, :], jnp.zeros((8 - batch - 1, d), F32)], axis=0)
    mods = [_ada_call(cond, ada_w[layer], ada_b[layer])[:batch + 1].reshape(batch + 1, N_MOD, 1, d)
            for layer in range(depth)]
    h_l = h_c = None
    for layer in range(depth):
        need_ctx = layer < depth - 1
        lam_init = 0.8 - 0.6 * math.exp(-0.3 * layer)
        tab = [mods[layer][:, k] for k in range(N_MOD)]
        g = norm_g[layer]
        if layer == 0:
            _, h_l = _norm_call(xl, sel_lat, pre=(g[0:1], tab[0], tab[1]), tm=tm_norm)
            _, h_c = _norm_call(xc, sel_ctx, pre=(g[0:1], tab[0], tab[1]), tm=tm_norm)

        wi = w_in[layer]
        w_a = jnp.concatenate([wi[:, o_nq:o_dq], wi[:, o_dv:o_z]], axis=1).astype(BF16)
        w_b = wi[:, o_dq:o_dv].astype(BF16)
        w_c = jnp.pad(wi[:, o_z:o_end], ((0, 0), (0, SSD_IN_W - (o_end - o_z)))).astype(BF16)
        a_lat = _mm_call(h_l, w_a, BF16, colscale=scale_a, name="proj_attn")
        a_ctx = _mm_call(h_c, w_a, BF16, colscale=scale_a, name="proj_attn")
        qk_lat = _mm_call(h_l, w_b, BF16, colscale=scale_b, rope=(cos_t, sin_t), name="proj_rope")
        qk_ctx = _mm_call(h_c, w_b, BF16, colscale=scale_b, name="proj_qk_ctx")
        s_lat = _mm_call(h_l, w_c, F32, tn=SSD_IN_W // 3, name="proj_ssd")
        s_ctx = _mm_call(h_c, w_c, F32, tn=SSD_IN_W // 3, name="proj_ssd")

        lam_p = da_lambda[layer]
        g_sub = (da_sub_g[layer] * (1.0 - lam_init)).reshape(1, LANES)
        na_l = _na_call(a_lat, a_ctx, _na_bias(na_rpb[layer]), batch, seq, n_ctx)
        da_l = _da_call(qk_lat, qk_ctx, a_lat, a_ctx, lam_p, g_sub, lam_init, batch, seq, n_ctx)

        u_ssd = _ssd_prep_call(s_lat, s_ctx, ssd_conv_w[layer], ssd_conv_b[layer], batch, seq, n_ctx)
        dtt = jnp.transpose(u_ssd[:, :, SSD_INNER + SSD_XBC:SSD_INNER + SSD_XBC + 2 * SSD_HEADS], (0, 2, 1))
        ssd_o = _ssd_call(u_ssd, dtt, ssd_dt_bias[layer], ssd_a_log[layer], ssd_d[layer], ssd_norm_g[layer],
                          batch, seq, n_ctx)

        wo = w_out[layer].astype(BF16)
        wq = peer_w_q[layer].astype(BF16)
        keys = peer_sub_keys[layer].reshape(2 * PEER_HEADS, PEER_N_KEYS, PEER_QDIM // 2).astype(BF16)
        pu = peer_u[layer].astype(BF16)
        pvt = peer_v[layer].T.astype(BF16)
        nxt = layer + 1 < depth

        lat_tiles = seq // TOKEN_TILE
        sel_lat_t = lambda i: i // lat_tiles
        res_mix, pre_ffn, res_ffn = (tab[2], g[1:2]), (g[2:3], tab[3], tab[4]), (tab[5], g[3:4])
        pre_n = (norm_g[layer + 1][0:1], mods[layer + 1][:, 0], mods[layer + 1][:, 1]) if nxt else None
        xl, h2_l = _wout_call(na_l, da_l, ssd_o, wo, xl, sel_lat_t, res_mix, pre_ffn,
                              lambda i: (i // lat_tiles, i % lat_tiles), TOKEN_TILE)
        xl, h_l = _peer(h2_l, wq, keys, pu, pvt, xl, sel_lat_t, res_ffn, pre_n)

        if need_ctx:
            na_c, da_c = _ctx_attn_call(a_ctx, qk_ctx, lam_p, g_sub, lam_init, batch, n_ctx)
            xc, h2_c = _wout_call(na_c, da_c, ssd_o, wo, xc, sel_ctx, res_mix, pre_ffn,
                                  lambda i: (i, seq // n_ctx), n_ctx)
            xc, h_c = _peer(h2_c, wq, keys, pu, pvt, xc, sel_ctx, res_ffn, pre_n)
    return xl.reshape(batch, seq, d)
```

```python
import functools
import math

import numpy as np
import jax
import jax.numpy as jnp
from jax import lax
from jax.experimental import pallas as pl
from jax.experimental.pallas import tpu as pltpu

F32 = jnp.float32
BF16 = jnp.bfloat16
EPS = 1e-6
NEG_INF = float("-inf")

VMEM_LIMIT_BYTES = 58 * 1024 * 1024
LANES = 128

GRID_W = 64
NA_DIM = 64
NA_HEADS = 8
NA_WIN_ROWS = 8
NA_WIN_COLS = 16
DA_DIM = 64
DA_HEADS = 4
ROPE_BASE = 10000.0
SSD_INNER = 1024
SSD_HEAD_DIM = 64
SSD_HEADS = 16
SSD_GROUPS = 2
SSD_STATE = 128
SSD_CONV = 5
SSD_CHUNK = 128
SSD_XBC = SSD_INNER + 2 * SSD_GROUPS * SSD_STATE
SSD_IN_W = SSD_INNER + SSD_XBC + LANES
PEER_HEADS = 8
PEER_QDIM = 256
PEER_N_KEYS = 128
PEER_TOPK = 16
N_MOD = 6
TOKEN_TILE = 512


def _params(*sem):
    return pltpu.CompilerParams(dimension_semantics=sem, vmem_limit_bytes=VMEM_LIMIT_BYTES)


def _nt_dot(a, b):
    return lax.dot_general(a, b, (((1,), (1,)), ((), ())), preferred_element_type=F32)


def _lane_lt64(shape):
    return lax.broadcasted_iota(jnp.int32, shape, len(shape) - 1) < (LANES // 2)


def _ada_body(c_ref, w_ref, b_ref, o_ref):
    c = c_ref[...]
    sc = (c * jax.nn.sigmoid(c)).astype(BF16)
    o_ref[...] = jnp.dot(sc, w_ref[...].astype(BF16), preferred_element_type=F32) + b_ref[...]


def _ada_call(cond, w, b, layer, tn=768):
    m, d = cond.shape
    depth, _, n = w.shape
    return pl.pallas_call(
        _ada_body,
        out_shape=jax.ShapeDtypeStruct((m, n), F32),
        grid=(n // tn,),
        in_specs=[pl.BlockSpec((m, d), lambda j: (0, 0)),
                  pl.BlockSpec((None, d, tn), lambda j: (layer, 0, j)),
                  pl.BlockSpec((None, 1, tn), lambda j: (layer, 0, j))],
        out_specs=pl.BlockSpec((m, tn), lambda j: (0, j)),
        compiler_params=_params("arbitrary"),
        name="ada_mod",
    )(cond, w, b.reshape(depth, 1, n))


def _rms(x, g):
    return x * lax.rsqrt(jnp.mean(x * x, axis=-1, keepdims=True) + EPS) * g


def _norm_body(has_res, has_pre, *refs):
    refs = list(refs)
    x = refs.pop(0)[...]
    if has_res:
        f_ref, gate_ref, gpost_ref = refs.pop(0), refs.pop(0), refs.pop(0)
        x = x + gate_ref[0] * _rms(f_ref[...], gpost_ref[...])
    if has_pre:
        gpre_ref, shift_ref, scale_ref = refs.pop(0), refs.pop(0), refs.pop(0)
    if has_res:
        refs.pop(0)[...] = x
    if has_pre:
        h = _rms(x, gpre_ref[...]) * (1.0 + scale_ref[0]) + shift_ref[0]
        refs.pop(0)[...] = h.astype(BF16)


def _norm_call(x, sel, *, res=None, pre=None, tm=256):
    m, d = x.shape
    row = pl.BlockSpec((tm, d), lambda i: (i, 0))
    vec = pl.BlockSpec((1, d), lambda i: (0, 0))
    tab = pl.BlockSpec((1, 1, d), lambda i: (sel(i), 0, 0))
    args, specs, outs, ospecs = [x], [row], [], []
    if res is not None:
        args += list(res)
        specs += [row, tab, vec]
        outs.append(jax.ShapeDtypeStruct((m, d), F32))
        ospecs.append(row)
    if pre is not None:
        args += list(pre)
        specs += [vec, tab, tab]
        outs.append(jax.ShapeDtypeStruct((m, d), BF16))
        ospecs.append(row)
    res_out = pl.pallas_call(
        functools.partial(_norm_body, res is not None, pre is not None),
        out_shape=tuple(outs), grid=(m // tm,), in_specs=specs, out_specs=tuple(ospecs),
        compiler_params=_params("parallel"), name="norm_mod",
    )(*args)
    res_out = list(res_out)
    x_new = res_out.pop(0) if res is not None else None
    h = res_out.pop(0) if pre is not None else None
    return x_new, h


def _rope_partner(x):
    lane = lax.broadcasted_iota(jnp.int32, x.shape, 1)
    first = (lane % 32) < 16
    return jnp.where(first, pltpu.roll(x, LANES - 16, 1), pltpu.roll(x, 16, 1))


def _mm_body(epi, h_ref, w_ref, *rest):
    acc = jnp.dot(h_ref[...], w_ref[...], preferred_element_type=F32)
    if epi == "plain":
        (o_ref,) = rest
    elif epi == "scale":
        cs_ref, o_ref = rest
        acc = acc * cs_ref[...]
    else:
        cos_ref, sin_ref, cs_ref, o_ref = rest
        cos = cos_ref[...]
        sin = sin_ref[...]
        n = acc.shape[1]
        blocks = []
        for c in range(n // LANES):
            xb = acc[:, c * LANES:(c + 1) * LANES]
            blocks.append(xb * cos + _rope_partner(xb) * sin)
        acc = jnp.concatenate(blocks, axis=1) * cs_ref[...]
    o_ref[...] = acc.astype(o_ref.dtype)


def _mm_call(h, w, out_dtype, *, colscale=None, rope=None, tm=512, tn=None, name="proj"):
    m, k = h.shape
    n = w.shape[1]
    tn = n if tn is None else tn
    args = [h, w]
    specs = [pl.BlockSpec((tm, k), lambda i, j: (i, 0)),
             pl.BlockSpec((k, tn), lambda i, j: (0, j))]
    epi = "plain"
    if rope is not None:
        cos_t, sin_t = rope
        nt = cos_t.shape[0] // tm
        args += [cos_t, sin_t]
        specs += [pl.BlockSpec((tm, LANES), lambda i, j: (i % nt, 0))] * 2
        epi = "rope"
    if colscale is not None:
        args.append(colscale.reshape(1, n))
        specs.append(pl.BlockSpec((1, tn), lambda i, j: (0, j)))
        epi = "scale" if epi == "plain" else epi
    return pl.pallas_call(
        functools.partial(_mm_body, epi),
        out_shape=jax.ShapeDtypeStruct((m, n), out_dtype),
        grid=(m // tm, n // tn), in_specs=specs,
        out_specs=pl.BlockSpec((tm, tn), lambda i, j: (i, j)),
        compiler_params=_params("parallel", "arbitrary"), name=name,
    )(*args)


def _rope_tables(seq):
    quarter = DA_DIM // 4
    inv_freq = 1.0 / (ROPE_BASE ** (jnp.arange(quarter, dtype=F32) / quarter))
    t = jnp.arange(seq)
    ang_r = (t // GRID_W).astype(F32)[:, None] * inv_freq
    ang_c = (t % GRID_W).astype(F32)[:, None] * inv_freq

    def unit(ang):
        return (jnp.concatenate([jnp.cos(ang), jnp.cos(ang)], -1),
                jnp.concatenate([-jnp.sin(ang), jnp.sin(ang)], -1))

    cr, sr = unit(ang_r)
    cc, sc = unit(ang_c)
    cos64 = jnp.concatenate([cr, cc], -1)
    sin64 = jnp.concatenate([sr, sc], -1)
    return jnp.tile(cos64, (1, 2)), jnp.tile(sin64, (1, 2))


def _stack_heads(q2):
    lo = _lane_lt64(q2.shape)
    zero = jnp.zeros_like(q2)
    return jnp.concatenate([jnp.where(lo, q2, zero), jnp.where(lo, zero, q2)], axis=0)


def _unstack_heads(o):
    r = o.shape[0] // 2
    return jnp.where(_lane_lt64((r, o.shape[1])), o[:r], o[r:])


NA_ROW_GROUP = 8


def _na_body(rows, q_ref, k_ref, v_ref, kc_ref, vc_ref, bias_ref, o_ref):
    kc = kc_ref[...]
    vc = vc_ref[...]
    kh = NA_WIN_ROWS

    def scores(r):
        rs = jnp.clip(r - kh // 2, 0, rows - kh)
        q0 = pl.multiple_of(r * GRID_W, GRID_W)
        k0 = pl.multiple_of(rs * GRID_W, GRID_W)
        qs = _stack_heads(q_ref[pl.ds(q0, GRID_W), :])
        s_loc = _nt_dot(qs, k_ref[pl.ds(k0, kh * GRID_W), :]) + bias_ref[0, r - rs]
        return q0, k0, s_loc, _nt_dot(qs, kc)

    def finish(q0, k0, p_loc, p_ctx, l):
        o = (jnp.dot(p_loc, v_ref[pl.ds(k0, kh * GRID_W), :], preferred_element_type=F32)
             + jnp.dot(p_ctx, vc, preferred_element_type=F32)) / l
        o_ref[pl.ds(q0, GRID_W), :] = _unstack_heads(o).astype(o_ref.dtype)

    def row_group(gi, carry):
        nxt = scores(gi * NA_ROW_GROUP)
        pending = None
        for j in range(NA_ROW_GROUP):
            q0, k0, s_loc, s_ctx = nxt
            if j + 1 < NA_ROW_GROUP:
                nxt = scores(gi * NA_ROW_GROUP + j + 1)
            m = jnp.maximum(jnp.max(s_loc, axis=-1, keepdims=True), jnp.max(s_ctx, axis=-1, keepdims=True))
            p_loc = jnp.exp(s_loc - m)
            p_ctx = jnp.exp(s_ctx - m)
            l = jnp.sum(p_loc, axis=-1, keepdims=True) + jnp.sum(p_ctx, axis=-1, keepdims=True)
            if pending is not None:
                finish(*pending)
            pending = (q0, k0, p_loc.astype(BF16), p_ctx.astype(BF16), l)
        finish(*pending)
        return carry

    lax.fori_loop(0, rows // NA_ROW_GROUP, row_group, 0)


def _na_bias(rpb):
    kh, w = NA_WIN_ROWS, GRID_W
    c = np.arange(w)
    wstart = np.clip(c - NA_WIN_COLS // 2, 0, w - NA_WIN_COLS)
    kc = np.arange(w)
    valid = (kc[None, :] >= wstart[:, None]) & (kc[None, :] < wstart[:, None] + NA_WIN_COLS)
    cidx = np.clip(kc[None, :] - c[:, None] + NA_WIN_COLS - 1, 0, 2 * NA_WIN_COLS - 2)
    typ = np.arange(kh)
    ridx = np.arange(kh)[None, :] - typ[:, None] + NA_WIN_ROWS - 1
    pick_r = (ridx[:, :, None] == np.arange(2 * NA_WIN_ROWS - 1)).astype(np.float32)
    pick_c = ((cidx[:, :, None] == np.arange(2 * NA_WIN_COLS - 1)) & valid[:, :, None]).astype(np.float32)
    b = jnp.einsum('hrc,tjr,qkc->htqjk', rpb.astype(F32), pick_r, pick_c, precision=lax.Precision.HIGHEST)
    b = b + np.where(valid, 0.0, NEG_INF).astype(np.float32)[None, None, :, None, :]
    b = b.reshape(NA_HEADS // 2, 2, kh, w, kh * w)
    return jnp.transpose(b, (0, 2, 1, 3, 4)).reshape(NA_HEADS // 2, kh, 2 * w, kh * w).astype(F32)


def _na_call(a_lat, a_ctx, bias, batch, seq, n_ctx):
    rows = seq // GRID_W
    pairs = NA_HEADS // 2
    lat = lambda c0: pl.BlockSpec((seq, LANES), lambda b, p: (b, c0 + p))
    ctx = lambda c0: pl.BlockSpec((n_ctx, LANES), lambda b, p: (b, c0 + p))
    return pl.pallas_call(
        functools.partial(_na_body, rows),
        out_shape=jax.ShapeDtypeStruct((batch * seq, NA_HEADS * NA_DIM), BF16),
        grid=(batch, pairs),
        in_specs=[lat(0), lat(pairs), lat(2 * pairs), ctx(pairs), ctx(2 * pairs),
                  pl.BlockSpec((1,) + bias.shape[1:], lambda b, p: (p, 0, 0, 0))],
        out_specs=pl.BlockSpec((seq, LANES), lambda b, p: (b, p)),
        compiler_params=_params("parallel", "parallel"), name="na_attn",
    )(a_lat, a_lat, a_lat, a_ctx, a_ctx, bias)


def _da_lambda(lam_ref, lam_init):
    lf = lam_ref[...]
    a = jnp.sum(lf[0:1] * lf[1:2], axis=-1, keepdims=True)
    b = jnp.sum(lf[2:3] * lf[3:4], axis=-1, keepdims=True)
    return jnp.exp(a) - jnp.exp(b) + lam_init


def _da_finish(acc, l, lam, g):
    tq = acc.shape[0] // 2
    o = acc / l
    d = o[:tq] - lam * o[tq:]
    return d * lax.rsqrt(jnp.mean(d * d, axis=-1, keepdims=True) + EPS) * g


def _flash_step(qs, k, v, m, l, acc):
    s = _nt_dot(qs, k)
    m_new = jnp.maximum(m, jnp.max(s, axis=-1, keepdims=True))
    alpha = jnp.exp(m - m_new)
    p = jnp.exp(s - m_new)
    l = alpha * l + jnp.sum(p, axis=-1, keepdims=True)
    acc = alpha * acc + jnp.dot(p.astype(BF16), v, preferred_element_type=F32)
    return m_new, l, acc


def _da_body(lam_init, n_chunks, tk, q_ref, k_ref, v_ref, kc_ref, vc_ref, lam_ref, g_ref, o_ref):
    qs = _stack_heads(q_ref[...])
    r = qs.shape[0]
    chunks = [(k_ref, v_ref, c * tk, tk) for c in range(n_chunks)] + [(kc_ref, vc_ref, 0, kc_ref.shape[0])]

    def scores(i):
        kr, _, k0, n = chunks[i]
        return _nt_dot(qs, kr[k0:k0 + n, :])

    m = jnp.full((r, 1), NEG_INF, F32)
    l = jnp.zeros((r, 1), F32)
    acc = jnp.zeros((r, LANES), F32)
    s_next = scores(0)
    pending = None
    for i in range(len(chunks)):
        s = s_next
        if i + 1 < len(chunks):
            s_next = scores(i + 1)
        m_new = jnp.maximum(m, jnp.max(s, axis=-1, keepdims=True))
        alpha = jnp.exp(m - m_new)
        p = jnp.exp(s - m_new)
        l = alpha * l + jnp.sum(p, axis=-1, keepdims=True)
        m = m_new
        if pending is not None:
            p_prev, alpha_prev, (_, vr, k0, n) = pending
            acc = alpha_prev * acc + jnp.dot(p_prev, vr[k0:k0 + n, :], preferred_element_type=F32)
        pending = (p.astype(BF16), alpha, chunks[i])
    p_prev, alpha_prev, (_, vr, k0, n) = pending
    acc = alpha_prev * acc + jnp.dot(p_prev, vr[k0:k0 + n, :], preferred_element_type=F32)
    o_ref[...] = _da_finish(acc, l, _da_lambda(lam_ref, lam_init), g_ref[...]).astype(o_ref.dtype)


def _da_call(qk_lat, qk_ctx, a_lat, a_ctx, lam_p, g_scaled, lam_init, batch, seq, n_ctx, tq=256, tk=512):
    nq = seq // tq
    h = DA_HEADS
    return pl.pallas_call(
        functools.partial(_da_body, lam_init, seq // tk, tk),
        out_shape=jax.ShapeDtypeStruct((batch * seq, h * 2 * DA_DIM), BF16),
        grid=(batch, h, nq),
        in_specs=[pl.BlockSpec((tq, LANES), lambda b, hh, i: (b * nq + i, hh)),
                  pl.BlockSpec((seq, LANES), lambda b, hh, i: (b, h + hh)),
                  pl.BlockSpec((seq, LANES), lambda b, hh, i: (b, 3 * h + hh)),
                  pl.BlockSpec((n_ctx, LANES), lambda b, hh, i: (b, h + hh)),
                  pl.BlockSpec((n_ctx, LANES), lambda b, hh, i: (b, 3 * h + hh)),
                  pl.BlockSpec(lam_p.shape, lambda b, hh, i: (0, 0)),
                  pl.BlockSpec((1, LANES), lambda b, hh, i: (0, 0))],
        out_specs=pl.BlockSpec((tq, LANES), lambda b, hh, i: (b * nq + i, hh)),
        compiler_params=_params("parallel", "parallel", "arbitrary"), name="da_attn",
    )(qk_lat, qk_lat, a_lat, qk_ctx, a_ctx, lam_p, g_scaled)


def _ctx_attn_body(lam_init, a_ref, qk_ref, lam_ref, g_ref, na_ref, da_ref):
    pairs = NA_HEADS // 2
    for p in range(pairs):
        qs = _stack_heads(a_ref[:, p * LANES:(p + 1) * LANES])
        k = a_ref[:, (pairs + p) * LANES:(pairs + p + 1) * LANES]
        v = a_ref[:, (2 * pairs + p) * LANES:(2 * pairs + p + 1) * LANES]
        s = _nt_dot(qs, k)
        e = jnp.exp(s - jnp.max(s, axis=-1, keepdims=True))
        o = jnp.dot(e.astype(BF16), v, preferred_element_type=F32) / jnp.sum(e, axis=-1, keepdims=True)
        na_ref[:, p * LANES:(p + 1) * LANES] = _unstack_heads(o).astype(na_ref.dtype)
    lam = _da_lambda(lam_ref, lam_init)
    for hh in range(DA_HEADS):
        qs = _stack_heads(qk_ref[:, hh * LANES:(hh + 1) * LANES])
        k = qk_ref[:, (DA_HEADS + hh) * LANES:(DA_HEADS + hh + 1) * LANES]
        v = a_ref[:, (3 * pairs + hh) * LANES:(3 * pairs + hh + 1) * LANES]
        r = qs.shape[0]
        init = (jnp.full((r, 1), NEG_INF, F32), jnp.zeros((r, 1), F32), jnp.zeros((r, LANES), F32))
        m, l, acc = _flash_step(qs, k, v, *init)
        da_ref[:, hh * LANES:(hh + 1) * LANES] = _da_finish(acc, l, lam, g_ref[...]).astype(da_ref.dtype)


def _ctx_attn_call(a_ctx, qk_ctx, lam_p, g_scaled, lam_init, batch, n_ctx):
    return pl.pallas_call(
        functools.partial(_ctx_attn_body, lam_init),
        out_shape=(jax.ShapeDtypeStruct((batch * n_ctx, NA_HEADS * NA_DIM), BF16),
                   jax.ShapeDtypeStruct((batch * n_ctx, DA_HEADS * 2 * DA_DIM), BF16)),
        grid=(batch,),
        in_specs=[pl.BlockSpec((n_ctx, a_ctx.shape[1]), lambda b: (b, 0)),
                  pl.BlockSpec((n_ctx, qk_ctx.shape[1]), lambda b: (b, 0)),
                  pl.BlockSpec(lam_p.shape, lambda b: (0, 0)),
                  pl.BlockSpec((1, LANES), lambda b: (0, 0))],
        out_specs=(pl.BlockSpec((n_ctx, NA_HEADS * NA_DIM), lambda b: (b, 0)),
                   pl.BlockSpec((n_ctx, DA_HEADS * 2 * DA_DIM), lambda b: (b, 0))),
        compiler_params=_params("parallel"), name="ctx_attn",
    )(a_ctx, qk_ctx, lam_p, g_scaled)


def _conv_seg(u, w, bias):
    n = u.shape[0]
    t = lax.broadcasted_iota(jnp.int32, u.shape, 0)
    acc = jnp.zeros_like(u) + bias
    for kk in range(SSD_CONV):
        d = kk - SSD_CONV // 2
        if d == 0:
            sh = u
        else:
            sh = pltpu.roll(u, (-d) % n, 0)
            sh = jnp.where((t + d >= 0) & (t + d < n), sh, 0.0)
        acc = acc + sh * w[kk:kk + 1, :]
    return acc * jax.nn.sigmoid(acc)


def _ssd_prep_body(seq, n_zc, lat_ref, ctx_ref, w_ref, b_ref, o_ref):
    j = pl.program_id(1)
    is_conv = (j >= n_zc) & (j < n_zc + SSD_XBC // LANES)

    @pl.when(is_conv)
    def _():
        w = w_ref[...]
        bias = b_ref[...]
        o_ref[0, :seq, :] = _conv_seg(lat_ref[...], w, bias)
        o_ref[0, seq:, :] = _conv_seg(ctx_ref[...], w, bias)

    @pl.when(jnp.logical_not(is_conv))
    def _():
        o_ref[0, :seq, :] = lat_ref[...]
        o_ref[0, seq:, :] = ctx_ref[...]


def _ssd_prep_call(s_lat, s_ctx, conv_w, conv_b, batch, seq, n_ctx):
    n_zc = SSD_INNER // LANES
    ncol = SSD_IN_W // LANES
    n_cw = SSD_XBC // LANES
    wcol = lambda b, j: (0, jnp.clip(j - n_zc, 0, n_cw - 1))
    return pl.pallas_call(
        functools.partial(_ssd_prep_body, seq, n_zc),
        out_shape=jax.ShapeDtypeStruct((batch, seq + n_ctx, SSD_IN_W), F32),
        grid=(batch, ncol),
        in_specs=[pl.BlockSpec((seq, LANES), lambda b, j: (b, j)),
                  pl.BlockSpec((n_ctx, LANES), lambda b, j: (b, j)),
                  pl.BlockSpec((SSD_CONV, LANES), wcol),
                  pl.BlockSpec((1, LANES), wcol)],
        out_specs=pl.BlockSpec((1, seq + n_ctx, LANES), lambda b, j: (b, 0, j)),
        compiler_params=_params("parallel", "parallel"), name="ssd_prep",
    )(s_lat, s_ctx, conv_w, conv_b.reshape(1, -1))


def _softplus(x):
    return jnp.maximum(x, 0.0) + jnp.log1p(jnp.exp(-jnp.abs(x)))


def _ssd_chunk(direction, xs_ref, bc_ref, dt_ref, dtt_ref, pr_ref, pc_ref, dexp_ref, s_ref):
    q = SSD_CHUNK
    nh = SSD_HEADS
    hi = lax.Precision.HIGHEST
    d0 = direction * nh
    dt = _softplus(dt_ref[0][:, d0:d0 + nh] + pr_ref[0:1, :])
    a = dt * (-jnp.exp(pr_ref[1:2, :]))
    dtt = _softplus(dtt_ref[0][d0:d0 + nh, :] + pc_ref[:, 0:1])
    at = dtt * (-jnp.exp(pc_ref[:, 1:2]))
    ii = lax.broadcasted_iota(jnp.int32, (q, q), 0)
    jj = lax.broadcasted_iota(jnp.int32, (q, q), 1)
    tri = (jj <= ii) if direction == 0 else (jj >= ii)
    tri_f = tri.astype(F32)
    cs = jnp.dot(tri_f, a, precision=hi, preferred_element_type=F32)
    cst = lax.dot_general(at, tri_f, (((1,), (1,)), ((), ())), precision=hi,
                          preferred_element_type=F32)
    edge = q - 1 if direction == 0 else 0
    tot_t = cst[:, edge:edge + 1]
    dst = jnp.exp(tot_t - cst)
    etot = jnp.exp(cs[edge:edge + 1, :])

    lo = _lane_lt64((q, LANES))
    groups = []
    for g in range(SSD_GROUPS):
        b_g = bc_ref[0][:, g * SSD_STATE:(g + 1) * SSD_STATE]
        c_g = bc_ref[0][:, (SSD_GROUPS + g) * SSD_STATE:(SSD_GROUPS + g + 1) * SSD_STATE]
        groups.append((c_g, _nt_dot(c_g.astype(BF16), b_g.astype(BF16)), b_g.T))
    ys = []
    for pair in range(nh // 2):
        c_g, cb, bt_g = groups[(2 * pair) // (nh // SSD_GROUPS)]
        xs_p = xs_ref[0][:, pair * LANES:(pair + 1) * LANES]
        r0, r1 = 2 * pair, 2 * pair + 1
        dt_p = jnp.where(lo, jnp.broadcast_to(dt[:, r0:r0 + 1], (q, LANES)),
                         jnp.broadcast_to(dt[:, r1:r1 + 1], (q, LANES)))
        xdt = (xs_p * dt_p).astype(BF16)
        zero_b = jnp.zeros_like(xdt)
        xdt_bd = jnp.concatenate([jnp.where(lo, xdt, zero_b), jnp.where(lo, zero_b, xdt)], axis=0)
        s_p = s_ref[pair]
        s_b = s_p.astype(BF16)
        s_bd = jnp.concatenate([jnp.where(lo, s_b, zero_b), jnp.where(lo, zero_b, s_b)], axis=0)
        lhs, lhs_state, bts = [], [], []
        for r in (r0, r1):
            colb = jnp.broadcast_to(cs[:, r:r + 1], (q, q))
            rowb = jnp.broadcast_to(cst[r:r + 1, :], (q, q))
            decay = jnp.exp(jnp.where(tri, colb - rowb, NEG_INF))
            lhs.append((cb * decay).astype(BF16))
            lhs_state.append((c_g * jnp.exp(colb)).astype(BF16))
            bts.append((bt_g * jnp.broadcast_to(dst[r:r + 1, :], (SSD_STATE, q))).astype(BF16))
        y = jnp.dot(jnp.concatenate(lhs + lhs_state, axis=1), jnp.concatenate([xdt_bd, s_bd], axis=0),
                    preferred_element_type=F32)
        y = y + xs_p * dexp_ref[:, pair * LANES:(pair + 1) * LANES]
        dec_p = jnp.where(_lane_lt64((1, LANES)), jnp.broadcast_to(etot[:, r0:r0 + 1], (1, LANES)),
                          jnp.broadcast_to(etot[:, r1:r1 + 1], (1, LANES)))
        s_ref[pair] = s_p * dec_p + jnp.dot(jnp.concatenate(bts, axis=1), xdt_bd, preferred_element_type=F32)
        ys.append(y)
    return jnp.concatenate(ys, axis=1)


def _ssd_fwd_body(xs_ref, bc_ref, dt_ref, dtt_ref, pr_ref, pc_ref, dexp_ref, y_ref, s_ref):
    @pl.when(pl.program_id(1) == 0)
    def _():
        s_ref[...] = jnp.zeros_like(s_ref)

    y_ref[0] = _ssd_chunk(0, xs_ref, bc_ref, dt_ref, dtt_ref, pr_ref, pc_ref, dexp_ref, s_ref)


def _ssd_bwd_body(xs_ref, bc_ref, dt_ref, dtt_ref, pr_ref, pc_ref, dexp_ref, yf_ref, z_ref, g_ref, o_ref, s_ref):
    @pl.when(pl.program_id(1) == 0)
    def _():
        s_ref[...] = jnp.zeros_like(s_ref)

    y = _ssd_chunk(1, xs_ref, bc_ref, dt_ref, dtt_ref, pr_ref, pc_ref, dexp_ref, s_ref) + yf_ref[0]
    z = z_ref[0]
    gated = y * (z * jax.nn.sigmoid(z))
    gw = SSD_INNER // SSD_GROUPS
    parts = []
    for g in range(SSD_GROUPS):
        gg = gated[:, g * gw:(g + 1) * gw]
        parts.append(gg * lax.rsqrt(jnp.mean(gg * gg, axis=-1, keepdims=True) + EPS))
    o_ref[0] = (jnp.concatenate(parts, axis=1) * g_ref[...]).astype(o_ref.dtype)


def _ssd_call(u, dtt, dt_bias, a_log, d_skip, norm_g, batch, seq, n_ctx):
    q = SSD_CHUNK
    n_lat, n_c = seq // q, n_ctx // q
    n_ch = n_lat + n_c
    nh = SSD_HEADS
    fwd_chunk = lambda s: jnp.where(s < n_c, n_lat + s, s - n_c)
    bwd_chunk = lambda s: n_ch - 1 - s
    xs_blk = SSD_INNER // SSD_INNER
    bc_blk = (2 * SSD_INNER) // (2 * SSD_GROUPS * SSD_STATE)
    dt_blk = (SSD_INNER + SSD_XBC) // LANES
    dexp = [jnp.repeat(d_skip[d], SSD_HEAD_DIM).reshape(1, SSD_INNER) for d in range(2)]
    prow = [jnp.stack([dt_bias[d], a_log[d]], axis=0) for d in range(2)]
    pcol = [jnp.stack([dt_bias[d], a_log[d]], axis=1) for d in range(2)]

    def specs(chunk):
        return [pl.BlockSpec((1, q, SSD_INNER), lambda b, s: (b, chunk(s), xs_blk)),
                pl.BlockSpec((1, q, 2 * SSD_GROUPS * SSD_STATE), lambda b, s: (b, chunk(s), bc_blk)),
                pl.BlockSpec((1, q, LANES), lambda b, s: (b, chunk(s), dt_blk)),
                pl.BlockSpec((1, 2 * nh, q), lambda b, s: (b, 0, chunk(s))),
                pl.BlockSpec((2, nh), lambda b, s: (0, 0)),
                pl.BlockSpec((nh, 2), lambda b, s: (0, 0)),
                pl.BlockSpec((1, SSD_INNER), lambda b, s: (0, 0))]

    state = pltpu.VMEM((nh // 2, SSD_STATE, LANES), F32)
    yblk = lambda chunk: pl.BlockSpec((1, q, SSD_INNER), lambda b, s: (b, chunk(s), 0))
    y_f = pl.pallas_call(
        _ssd_fwd_body,
        out_shape=jax.ShapeDtypeStruct((batch, seq + n_ctx, SSD_INNER), F32),
        grid=(batch, n_ch), in_specs=specs(fwd_chunk), out_specs=yblk(fwd_chunk),
        scratch_shapes=[state], compiler_params=_params("parallel", "arbitrary"), name="ssd_fwd",
    )(u, u, u, dtt, prow[0], pcol[0], dexp[0])
    return pl.pallas_call(
        _ssd_bwd_body,
        out_shape=jax.ShapeDtypeStruct((batch, seq + n_ctx, SSD_INNER), BF16),
        grid=(batch, n_ch),
        in_specs=specs(bwd_chunk) + [yblk(bwd_chunk), yblk(bwd_chunk),
                                     pl.BlockSpec((1, SSD_INNER), lambda b, s: (0, 0))],
        out_specs=yblk(bwd_chunk),
        scratch_shapes=[state], compiler_params=_params("parallel", "arbitrary"), name="ssd_bwd",
    )(u, u, u, dtt, prow[1], pcol[1], dexp[1], y_f, u, norm_g.reshape(1, SSD_INNER))


def _residual_norm(x, f, gate, g_post, pre):
    x = x + gate * _rms(f, g_post)
    if pre is None:
        return x, None
    g_pre, shift, scale = pre
    return x, (_rms(x, g_pre) * (1.0 + scale) + shift).astype(BF16)


def _wout_body(na_ref, da_ref, ssd_ref, w_ref, x_ref, gate_ref, gpost_ref, gpre_ref, shift_ref, scale_ref,
               xo_ref, h_ref):
    k1 = na_ref.shape[1]
    k2 = k1 + da_ref.shape[1]
    acc = jnp.dot(na_ref[...], w_ref[:k1, :], preferred_element_type=F32)
    acc = acc + jnp.dot(da_ref[...], w_ref[k1:k2, :], preferred_element_type=F32)
    acc = acc + jnp.dot(ssd_ref[0], w_ref[k2:, :], preferred_element_type=F32)
    xo_ref[...], h_ref[...] = _residual_norm(x_ref[...], acc, gate_ref[0], gpost_ref[...],
                                             (gpre_ref[...], shift_ref[0], scale_ref[0]))


def _wout_call(na, da, ssd, w, x, sel, res, pre, ssd_tile, tm):
    m = na.shape[0]
    d = w.shape[1]
    row = pl.BlockSpec((tm, d), lambda i: (i, 0))
    vec = pl.BlockSpec((1, d), lambda i: (0, 0))
    tab = pl.BlockSpec((1, 1, d), lambda i: (sel(i), 0, 0))
    return pl.pallas_call(
        _wout_body,
        out_shape=(jax.ShapeDtypeStruct((m, d), F32), jax.ShapeDtypeStruct((m, d), BF16)),
        grid=(m // tm,),
        in_specs=[pl.BlockSpec((tm, na.shape[1]), lambda i: (i, 0)),
                  pl.BlockSpec((tm, da.shape[1]), lambda i: (i, 0)),
                  pl.BlockSpec((1, tm, ssd.shape[2]), lambda i: ssd_tile(i) + (0,)),
                  pl.BlockSpec(w.shape, lambda i: (0, 0)),
                  row, tab, vec, vec, tab, tab],
        out_specs=(row, row),
        compiler_params=_params("parallel"), name="w_out",
    )(na, da, ssd, w, x, res[0], res[1], *pre)


def _top_vals(x, k, with_rank=False):
    vals = []
    rank = jnp.full(x.shape, float(k), F32) if with_rank else None
    for i in range(k):
        m = jnp.max(x, axis=0, keepdims=True)
        vals.append(m)
        hit = x == m
        if with_rank:
            rank = jnp.where(hit, float(i), rank)
        x = jnp.where(hit, NEG_INF, x)
    vals = jnp.concatenate(vals, axis=0)
    return (vals, rank) if with_rank else vals


def _pair_candidates(v1, v2):
    k = PEER_TOPK
    half = k // 2
    row = lax.broadcasted_iota(jnp.int32, (half, v1.shape[1]), 0)
    parts = [v2 + v1[0:1, :]]
    for i in range(1, half):
        parts.append(jnp.where(row < k // (i + 1), v2[:half] + v1[i:i + 1, :], NEG_INF))
    parts.append(v1[half:] + v2[0:1, :])
    return jnp.concatenate(parts, axis=0)


def _peer_sel_body(h_ref, wq_ref, keys_ref, cnt_ref, e1_ref, rank_ref, e2_ref):
    qd = PEER_QDIM // 2
    k = PEER_TOPK
    qv = jnp.dot(h_ref[...], wq_ref[...], preferred_element_type=F32).astype(BF16)
    for hh in range(PEER_HEADS):
        c0 = hh * PEER_QDIM
        s1 = _nt_dot(keys_ref[2 * hh], qv[:, c0:c0 + qd])
        s2 = _nt_dot(keys_ref[2 * hh + 1], qv[:, c0 + qd:c0 + 2 * qd])
        v1 = _top_vals(s1, k)
        v2, rank = _top_vals(s2, k, with_rank=True)
        top = _top_vals(_pair_candidates(v1, v2), k)
        thr = top[k - 1:k, :]
        z = jnp.sum(jnp.exp(top - top[0:1, :]), axis=0, keepdims=True)
        cnt = jnp.zeros_like(s1)
        for r in range(k):
            cnt = cnt + jnp.where(s1 + v2[r:r + 1, :] >= thr, 1.0, 0.0)
        cnt_ref[hh] = cnt
        rank_ref[hh] = rank.astype(BF16)
        e1_ref[hh] = jnp.exp(s1 - v1[0:1, :])
        e2_ref[hh] = (jnp.exp(s2 - v2[0:1, :]) / z).astype(BF16)


def _peer_sel_call(h, wq, keys, tm=256):
    t, d = h.shape
    shape = (PEER_HEADS, PEER_N_KEYS, t)
    sblk = pl.BlockSpec((PEER_HEADS, PEER_N_KEYS, tm), lambda i: (0, 0, i))
    return pl.pallas_call(
        _peer_sel_body,
        out_shape=(jax.ShapeDtypeStruct(shape, F32), jax.ShapeDtypeStruct(shape, F32),
                   jax.ShapeDtypeStruct(shape, BF16), jax.ShapeDtypeStruct(shape, BF16)),
        grid=(t // tm,),
        in_specs=[pl.BlockSpec((tm, d), lambda i: (i, 0)),
                  pl.BlockSpec(wq.shape, lambda i: (0, 0)),
                  pl.BlockSpec(keys.shape, lambda i: (0, 0, 0))],
        out_specs=(sblk, sblk, sblk, sblk),
        compiler_params=_params("parallel"), name="peer_select",
    )(h, wq, keys)


def _gelu_tanh(x):
    return 0.5 * x * (1.0 + jnp.tanh(math.sqrt(2.0 / math.pi) * (x + 0.044715 * (x * x * x))))


def _peer_ffn_body(n_a, n_split, has_pre, h_ref, u_ref, vt_ref, cnt_ref, e1_ref, rank_ref, e2_ref,
                   x_ref, gate_ref, gpost_ref, *rest):
    if has_pre:
        gpre_ref, shift_ref, scale_ref, xo_ref, ho_ref, acc_ref, wt_ref = rest
    else:
        xo_ref, acc_ref, wt_ref = rest
    j = pl.program_id(1)

    @pl.when(j == 0)
    def _():
        acc_ref[...] = jnp.zeros_like(acc_ref)

    nk = PEER_N_KEYS
    a_part = n_a // n_split
    rows = a_part * nk
    h = h_ref[...]

    def act(s):
        return _gelu_tanh(_nt_dot(u_ref[s * rows:(s + 1) * rows, :], h)).astype(BF16)

    act_next = act(0)
    for s in range(n_split):
        act_s = act_next
        if s + 1 < n_split:
            act_next = act(s + 1)
        for al in range(a_part):
            a = s * a_part + al
            gate = None
            for hh in range(PEER_HEADS):
                cnt = cnt_ref[hh, a:a + 1, :].astype(BF16)
                e1 = e1_ref[hh, a:a + 1, :].astype(BF16)
                w = jnp.where(rank_ref[hh] < cnt, e2_ref[hh], jnp.zeros((), BF16)) * e1
                gate = w if gate is None else gate + w
            wt_ref[a * nk:(a + 1) * nk, :] = gate * act_s[al * nk:(al + 1) * nk, :]
        acc_ref[...] += jnp.dot(vt_ref[:, s * rows:(s + 1) * rows], wt_ref[s * rows:(s + 1) * rows, :],
                                preferred_element_type=F32)

    @pl.when(j == pl.num_programs(1) - 1)
    def _():
        pre = (gpre_ref[...], shift_ref[0], scale_ref[0]) if has_pre else None
        x_new, h_next = _residual_norm(x_ref[...], acc_ref[...].T, gate_ref[0], gpost_ref[...], pre)
        xo_ref[...] = x_new
        if has_pre:
            ho_ref[...] = h_next


def _peer_ffn_call(h, u, vt, layer, cnt, e1, rank, e2, x, sel, res, pre, tm=TOKEN_TILE, n_a=8, n_split=2):
    t, d = h.shape
    e = u.shape[1]
    te = n_a * PEER_N_KEYS
    a_blk = pl.BlockSpec((PEER_HEADS, n_a, tm), lambda i, j: (0, j, i))
    b_blk = pl.BlockSpec((PEER_HEADS, PEER_N_KEYS, tm), lambda i, j: (0, 0, i))
    row = pl.BlockSpec((tm, d), lambda i, j: (i, 0))
    vec = pl.BlockSpec((1, d), lambda i, j: (0, 0))
    tab = pl.BlockSpec((1, 1, d), lambda i, j: (sel(i), 0, 0))
    args = [h, u, vt, cnt, e1, rank, e2, x, res[0], res[1]]
    specs = [row, pl.BlockSpec((None, te, d), lambda i, j: (layer, j, 0)),
             pl.BlockSpec((None, d, te), lambda i, j: (layer, 0, j)),
             a_blk, a_blk, b_blk, b_blk, row, tab, vec]
    outs = [jax.ShapeDtypeStruct((t, d), F32)]
    ospecs = [row]
    if pre is not None:
        args += list(pre)
        specs += [vec, tab, tab]
        outs.append(jax.ShapeDtypeStruct((t, d), BF16))
        ospecs.append(row)
    out = pl.pallas_call(
        functools.partial(_peer_ffn_body, n_a, n_split, pre is not None),
        out_shape=tuple(outs), grid=(t // tm, e // te), in_specs=specs, out_specs=tuple(ospecs),
        scratch_shapes=[pltpu.VMEM((d, tm), F32), pltpu.VMEM((te, tm), BF16)],
        compiler_params=_params("parallel", "arbitrary"), name="peer_ffn",
    )(*args)
    return (out[0], out[1]) if pre is not None else (out[0], None)


def _peer(h, wq, keys, u, vt, layer, x, sel, res, pre):
    cnt, e1, rank, e2 = _peer_sel_call(h, wq, keys)
    return _peer_ffn_call(h, u, vt, layer, cnt, e1, rank, e2, x, sel, res, pre)


def kernel(x, c, ctx, c_ctx, ada_w, ada_b, norm_g, w_in, w_out, na_rpb, ssd_conv_w, ssd_conv_b, ssd_dt_bias,
           ssd_a_log, ssd_d, ssd_norm_g, da_lambda, da_sub_g, peer_w_q, peer_sub_keys, peer_u, peer_v):
    batch, seq, d = x.shape
    n_ctx = ctx.shape[1]
    depth = ada_w.shape[0]
    xl = x.reshape(batch * seq, d)
    xc = ctx.reshape(batch * n_ctx, d)
    tm_norm = 256
    sel_lat = lambda i: i // (seq // tm_norm)
    sel_ctx = lambda i: batch
    cos_t, sin_t = _rope_tables(seq)

    o_nq, o_nk, o_nv = 0, 512, 1024
    o_dq, o_dk, o_dv = 1536, 2048, 2560
    o_z = 3072
    o_end = o_z + SSD_INNER + SSD_XBC + 2 * SSD_HEADS
    scale_a = jnp.concatenate([jnp.full((512,), NA_DIM ** -0.5, F32), jnp.ones((1536,), F32)])
    scale_b = jnp.concatenate([jnp.full((512,), DA_DIM ** -0.5, F32), jnp.ones((512,), F32)])

    cond = jnp.concatenate([c, c_ctx[None, :], jnp.zeros((8 - batch - 1, d), F32)], axis=0)
    mods = [_ada_call(cond, ada_w, ada_b, layer)[:batch + 1].reshape(batch + 1, N_MOD, 1, d)
            for layer in range(depth)]
    pu = peer_u.astype(BF16)
    pvt = jnp.swapaxes(peer_v, 1, 2).astype(BF16)
    h_l = h_c = None
    for layer in range(depth):
        need_ctx = layer < depth - 1
        lam_init = 0.8 - 0.6 * math.exp(-0.3 * layer)
        tab = [mods[layer][:, k] for k in range(N_MOD)]
        g = norm_g[layer]
        if layer == 0:
            _, h_l = _norm_call(xl, sel_lat, pre=(g[0:1], tab[0], tab[1]), tm=tm_norm)
            _, h_c = _norm_call(xc, sel_ctx, pre=(g[0:1], tab[0], tab[1]), tm=tm_norm)

        wi = w_in[layer]
        w_a = jnp.concatenate([wi[:, o_nq:o_dq], wi[:, o_dv:o_z]], axis=1).astype(BF16)
        w_b = wi[:, o_dq:o_dv].astype(BF16)
        w_c = jnp.pad(wi[:, o_z:o_end], ((0, 0), (0, SSD_IN_W - (o_end - o_z)))).astype(BF16)
        a_lat = _mm_call(h_l, w_a, BF16, colscale=scale_a, name="proj_attn")
        a_ctx = _mm_call(h_c, w_a, BF16, colscale=scale_a, name="proj_attn")
        qk_lat = _mm_call(h_l, w_b, BF16, colscale=scale_b, rope=(cos_t, sin_t), name="proj_rope")
        qk_ctx = _mm_call(h_c, w_b, BF16, colscale=scale_b, name="proj_qk_ctx")
        s_lat = _mm_call(h_l, w_c, F32, name="proj_ssd")
        s_ctx = _mm_call(h_c, w_c, F32, name="proj_ssd")

        lam_p = da_lambda[layer]
        g_sub = (da_sub_g[layer] * (1.0 - lam_init)).reshape(1, LANES)
        na_l = _na_call(a_lat, a_ctx, _na_bias(na_rpb[layer]), batch, seq, n_ctx)
        da_l = _da_call(qk_lat, qk_ctx, a_lat, a_ctx, lam_p, g_sub, lam_init, batch, seq, n_ctx)

        u_ssd = _ssd_prep_call(s_lat, s_ctx, ssd_conv_w[layer], ssd_conv_b[layer], batch, seq, n_ctx)
        dtt = jnp.transpose(u_ssd[:, :, SSD_INNER + SSD_XBC:SSD_INNER + SSD_XBC + 2 * SSD_HEADS], (0, 2, 1))
        ssd_o = _ssd_call(u_ssd, dtt, ssd_dt_bias[layer], ssd_a_log[layer], ssd_d[layer], ssd_norm_g[layer],
                          batch, seq, n_ctx)

        wo = w_out[layer].astype(BF16)
        wq = peer_w_q[layer].astype(BF16)
        keys = peer_sub_keys[layer].reshape(2 * PEER_HEADS, PEER_N_KEYS, PEER_QDIM // 2).astype(BF16)
        nxt = layer + 1 < depth

        lat_tiles = seq // TOKEN_TILE
        sel_lat_t = lambda i: i // lat_tiles
        res_mix, pre_ffn, res_ffn = (tab[2], g[1:2]), (g[2:3], tab[3], tab[4]), (tab[5], g[3:4])
        pre_n = (norm_g[layer + 1][0:1], mods[layer + 1][:, 0], mods[layer + 1][:, 1]) if nxt else None
        xl, h2_l = _wout_call(na_l, da_l, ssd_o, wo, xl, sel_lat_t, res_mix, pre_ffn,
                              lambda i: (i // lat_tiles, i % lat_tiles), TOKEN_TILE)
        xl, h_l = _peer(h2_l, wq, keys, pu, pvt, layer, xl, sel_lat_t, res_ffn, pre_n)

        if need_ctx:
            na_c, da_c = _ctx_attn_call(a_ctx, qk_ctx, lam_p, g_sub, lam_init, batch, n_ctx)
            xc, h2_c = _wout_call(na_c, da_c, ssd_o, wo, xc, sel_ctx, res_mix, pre_ffn,
                                  lambda i: (i, seq // n_ctx), n_ctx)
            xc, h_c = _peer(h2_c, wq, keys, pu, pvt, layer, xc, sel_ctx, res_ffn, pre_n)
    return xl.reshape(batch, seq, d)
```

```python
import functools
import math

import numpy as np
import jax
import jax.numpy as jnp
from jax import lax
from jax.experimental import pallas as pl
from jax.experimental.pallas import tpu as pltpu

F32 = jnp.float32
BF16 = jnp.bfloat16
EPS = 1e-6
NEG_INF = float("-inf")

VMEM_LIMIT_BYTES = 58 * 1024 * 1024
LANES = 128

GRID_W = 64
NA_DIM = 64
NA_HEADS = 8
NA_WIN_ROWS = 8
NA_WIN_COLS = 16
DA_DIM = 64
DA_HEADS = 4
ROPE_BASE = 10000.0
SSD_INNER = 1024
SSD_HEAD_DIM = 64
SSD_HEADS = 16
SSD_GROUPS = 2
SSD_STATE = 128
SSD_CONV = 5
SSD_CHUNK = 128
SSD_XBC = SSD_INNER + 2 * SSD_GROUPS * SSD_STATE
SSD_IN_W = SSD_INNER + SSD_XBC + LANES
PEER_HEADS = 8
PEER_QDIM = 256
PEER_N_KEYS = 128
PEER_TOPK = 16
N_MOD = 6
TOKEN_TILE = 512


def _params(*sem):
    return pltpu.CompilerParams(dimension_semantics=sem, vmem_limit_bytes=VMEM_LIMIT_BYTES)


def _nt_dot(a, b):
    return lax.dot_general(a, b, (((1,), (1,)), ((), ())), preferred_element_type=F32)


def _lane_lt64(shape):
    return lax.broadcasted_iota(jnp.int32, shape, len(shape) - 1) < (LANES // 2)


def _ada_body(c_ref, w_ref, b_ref, o_ref):
    c = c_ref[...]
    sc = (c * jax.nn.sigmoid(c)).astype(BF16)
    o_ref[...] = jnp.dot(sc, w_ref[...].astype(BF16), preferred_element_type=F32) + b_ref[...]


def _ada_call(cond, w, b, layer, tn=768):
    m, d = cond.shape
    depth, _, n = w.shape
    return pl.pallas_call(
        _ada_body,
        out_shape=jax.ShapeDtypeStruct((m, n), F32),
        grid=(n // tn,),
        in_specs=[pl.BlockSpec((m, d), lambda j: (0, 0)),
                  pl.BlockSpec((None, d, tn), lambda j: (layer, 0, j)),
                  pl.BlockSpec((None, 1, tn), lambda j: (layer, 0, j))],
        out_specs=pl.BlockSpec((m, tn), lambda j: (0, j)),
        compiler_params=_params("arbitrary"),
        name="ada_mod",
    )(cond, w, b.reshape(depth, 1, n))


def _rms(x, g):
    return x * lax.rsqrt(jnp.mean(x * x, axis=-1, keepdims=True) + EPS) * g


def _norm_body(has_res, has_pre, *refs):
    refs = list(refs)
    x = refs.pop(0)[...]
    if has_res:
        f_ref, gate_ref, gpost_ref = refs.pop(0), refs.pop(0), refs.pop(0)
        x = x + gate_ref[0] * _rms(f_ref[...], gpost_ref[...])
    if has_pre:
        gpre_ref, shift_ref, scale_ref = refs.pop(0), refs.pop(0), refs.pop(0)
    if has_res:
        refs.pop(0)[...] = x
    if has_pre:
        h = _rms(x, gpre_ref[...]) * (1.0 + scale_ref[0]) + shift_ref[0]
        refs.pop(0)[...] = h.astype(BF16)


def _norm_call(x, sel, *, res=None, pre=None, tm=256):
    m, d = x.shape
    row = pl.BlockSpec((tm, d), lambda i: (i, 0))
    vec = pl.BlockSpec((1, d), lambda i: (0, 0))
    tab = pl.BlockSpec((1, 1, d), lambda i: (sel(i), 0, 0))
    args, specs, outs, ospecs = [x], [row], [], []
    if res is not None:
        args += list(res)
        specs += [row, tab, vec]
        outs.append(jax.ShapeDtypeStruct((m, d), F32))
        ospecs.append(row)
    if pre is not None:
        args += list(pre)
        specs += [vec, tab, tab]
        outs.append(jax.ShapeDtypeStruct((m, d), BF16))
        ospecs.append(row)
    res_out = pl.pallas_call(
        functools.partial(_norm_body, res is not None, pre is not None),
        out_shape=tuple(outs), grid=(m // tm,), in_specs=specs, out_specs=tuple(ospecs),
        compiler_params=_params("parallel"), name="norm_mod",
    )(*args)
    res_out = list(res_out)
    x_new = res_out.pop(0) if res is not None else None
    h = res_out.pop(0) if pre is not None else None
    return x_new, h


def _rope_partner(x):
    lane = lax.broadcasted_iota(jnp.int32, x.shape, 1)
    first = (lane % 32) < 16
    return jnp.where(first, pltpu.roll(x, LANES - 16, 1), pltpu.roll(x, 16, 1))


def _mm_body(epi, h_ref, w_ref, *rest):
    acc = jnp.dot(h_ref[...], w_ref[...], preferred_element_type=F32)
    if epi == "plain":
        (o_ref,) = rest
    elif epi == "scale":
        cs_ref, o_ref = rest
        acc = acc * cs_ref[...]
    else:
        cos_ref, sin_ref, cs_ref, o_ref = rest
        cos = cos_ref[...]
        sin = sin_ref[...]
        n = acc.shape[1]
        blocks = []
        for c in range(n // LANES):
            xb = acc[:, c * LANES:(c + 1) * LANES]
            blocks.append(xb * cos + _rope_partner(xb) * sin)
        acc = jnp.concatenate(blocks, axis=1) * cs_ref[...]
    o_ref[...] = acc.astype(o_ref.dtype)


def _mm_call(h, w, out_dtype, *, colscale=None, rope=None, tm=512, tn=None, name="proj"):
    m, k = h.shape
    n = w.shape[1]
    tn = n if tn is None else tn
    args = [h, w]
    specs = [pl.BlockSpec((tm, k), lambda i, j: (i, 0)),
             pl.BlockSpec((k, tn), lambda i, j: (0, j))]
    epi = "plain"
    if rope is not None:
        cos_t, sin_t = rope
        nt = cos_t.shape[0] // tm
        args += [cos_t, sin_t]
        specs += [pl.BlockSpec((tm, LANES), lambda i, j: (i % nt, 0))] * 2
        epi = "rope"
    if colscale is not None:
        args.append(colscale.reshape(1, n))
        specs.append(pl.BlockSpec((1, tn), lambda i, j: (0, j)))
        epi = "scale" if epi == "plain" else epi
    return pl.pallas_call(
        functools.partial(_mm_body, epi),
        out_shape=jax.ShapeDtypeStruct((m, n), out_dtype),
        grid=(m // tm, n // tn), in_specs=specs,
        out_specs=pl.BlockSpec((tm, tn), lambda i, j: (i, j)),
        compiler_params=_params("parallel", "arbitrary"), name=name,
    )(*args)


def _rope_tables(seq):
    quarter = DA_DIM // 4
    inv_freq = 1.0 / (ROPE_BASE ** (jnp.arange(quarter, dtype=F32) / quarter))
    t = jnp.arange(seq)
    ang_r = (t // GRID_W).astype(F32)[:, None] * inv_freq
    ang_c = (t % GRID_W).astype(F32)[:, None] * inv_freq

    def unit(ang):
        return (jnp.concatenate([jnp.cos(ang), jnp.cos(ang)], -1),
                jnp.concatenate([-jnp.sin(ang), jnp.sin(ang)], -1))

    cr, sr = unit(ang_r)
    cc, sc = unit(ang_c)
    cos64 = jnp.concatenate([cr, cc], -1)
    sin64 = jnp.concatenate([sr, sc], -1)
    return jnp.tile(cos64, (1, 2)), jnp.tile(sin64, (1, 2))


def _stack_heads(q2):
    lo = _lane_lt64(q2.shape)
    zero = jnp.zeros_like(q2)
    return jnp.concatenate([jnp.where(lo, q2, zero), jnp.where(lo, zero, q2)], axis=0)


def _unstack_heads(o):
    r = o.shape[0] // 2
    return jnp.where(_lane_lt64((r, o.shape[1])), o[:r], o[r:])


NA_ROW_GROUP = 8


def _na_body(rows, q_ref, k_ref, v_ref, kc_ref, vc_ref, bias_ref, o_ref):
    kc = kc_ref[...]
    vc = vc_ref[...]
    kh = NA_WIN_ROWS

    def scores(r):
        rs = jnp.clip(r - kh // 2, 0, rows - kh)
        q0 = pl.multiple_of(r * GRID_W, GRID_W)
        k0 = pl.multiple_of(rs * GRID_W, GRID_W)
        qs = _stack_heads(q_ref[pl.ds(q0, GRID_W), :])
        s_loc = _nt_dot(qs, k_ref[pl.ds(k0, kh * GRID_W), :]) + bias_ref[0, r - rs]
        return q0, k0, s_loc, _nt_dot(qs, kc)

    def finish(q0, k0, p_loc, p_ctx, l):
        o = (jnp.dot(p_loc, v_ref[pl.ds(k0, kh * GRID_W), :], preferred_element_type=F32)
             + jnp.dot(p_ctx, vc, preferred_element_type=F32)) / l
        o_ref[pl.ds(q0, GRID_W), :] = _unstack_heads(o).astype(o_ref.dtype)

    def row_group(gi, carry):
        nxt = scores(gi * NA_ROW_GROUP)
        pending = None
        for j in range(NA_ROW_GROUP):
            q0, k0, s_loc, s_ctx = nxt
            if j + 1 < NA_ROW_GROUP:
                nxt = scores(gi * NA_ROW_GROUP + j + 1)
            m = jnp.maximum(jnp.max(s_loc, axis=-1, keepdims=True), jnp.max(s_ctx, axis=-1, keepdims=True))
            p_loc = jnp.exp(s_loc - m)
            p_ctx = jnp.exp(s_ctx - m)
            l = jnp.sum(p_loc, axis=-1, keepdims=True) + jnp.sum(p_ctx, axis=-1, keepdims=True)
            if pending is not None:
                finish(*pending)
            pending = (q0, k0, p_loc.astype(BF16), p_ctx.astype(BF16), l)
        finish(*pending)
        return carry

    lax.fori_loop(0, rows // NA_ROW_GROUP, row_group, 0)


def _na_bias(rpb):
    kh, w = NA_WIN_ROWS, GRID_W
    c = np.arange(w)
    wstart = np.clip(c - NA_WIN_COLS // 2, 0, w - NA_WIN_COLS)
    kc = np.arange(w)
    valid = (kc[None, :] >= wstart[:, None]) & (kc[None, :] < wstart[:, None] + NA_WIN_COLS)
    cidx = np.clip(kc[None, :] - c[:, None] + NA_WIN_COLS - 1, 0, 2 * NA_WIN_COLS - 2)
    typ = np.arange(kh)
    ridx = np.arange(kh)[None, :] - typ[:, None] + NA_WIN_ROWS - 1
    pick_r = (ridx[:, :, None] == np.arange(2 * NA_WIN_ROWS - 1)).astype(np.float32)
    pick_c = ((cidx[:, :, None] == np.arange(2 * NA_WIN_COLS - 1)) & valid[:, :, None]).astype(np.float32)
    b = jnp.einsum('hrc,tjr,qkc->htqjk', rpb.astype(F32), pick_r, pick_c, precision=lax.Precision.HIGHEST)
    b = b + np.where(valid, 0.0, NEG_INF).astype(np.float32)[None, None, :, None, :]
    b = b.reshape(NA_HEADS // 2, 2, kh, w, kh * w)
    return jnp.transpose(b, (0, 2, 1, 3, 4)).reshape(NA_HEADS // 2, kh, 2 * w, kh * w).astype(F32)


def _na_call(a_lat, a_ctx, bias, batch, seq, n_ctx):
    rows = seq // GRID_W
    pairs = NA_HEADS // 2
    lat = lambda c0: pl.BlockSpec((seq, LANES), lambda b, p: (b, c0 + p))
    ctx = lambda c0: pl.BlockSpec((n_ctx, LANES), lambda b, p: (b, c0 + p))
    return pl.pallas_call(
        functools.partial(_na_body, rows),
        out_shape=jax.ShapeDtypeStruct((batch * seq, NA_HEADS * NA_DIM), BF16),
        grid=(batch, pairs),
        in_specs=[lat(0), lat(pairs), lat(2 * pairs), ctx(pairs), ctx(2 * pairs),
                  pl.BlockSpec((1,) + bias.shape[1:], lambda b, p: (p, 0, 0, 0))],
        out_specs=pl.BlockSpec((seq, LANES), lambda b, p: (b, p)),
        compiler_params=_params("parallel", "parallel"), name="na_attn",
    )(a_lat, a_lat, a_lat, a_ctx, a_ctx, bias)


def _da_lambda(lam_ref, lam_init):
    lf = lam_ref[...]
    a = jnp.sum(lf[0:1] * lf[1:2], axis=-1, keepdims=True)
    b = jnp.sum(lf[2:3] * lf[3:4], axis=-1, keepdims=True)
    return jnp.exp(a) - jnp.exp(b) + lam_init


def _da_finish(acc, l, lam, g):
    tq = acc.shape[0] // 2
    o = acc / l
    d = o[:tq] - lam * o[tq:]
    return d * lax.rsqrt(jnp.mean(d * d, axis=-1, keepdims=True) + EPS) * g


def _flash_step(qs, k, v, m, l, acc):
    s = _nt_dot(qs, k)
    m_new = jnp.maximum(m, jnp.max(s, axis=-1, keepdims=True))
    alpha = jnp.exp(m - m_new)
    p = jnp.exp(s - m_new)
    l = alpha * l + jnp.sum(p, axis=-1, keepdims=True)
    acc = alpha * acc + jnp.dot(p.astype(BF16), v, preferred_element_type=F32)
    return m_new, l, acc


def _da_body(lam_init, n_chunks, tk, q_ref, k_ref, vt_ref, kc_ref, vtc_ref, lam_ref, g_ref, o_ref):
    qs = _stack_heads(q_ref[...])
    r = qs.shape[0]
    tq = r // 2
    chunks = [(k_ref, vt_ref, c * tk, tk) for c in range(n_chunks)] + [(kc_ref, vtc_ref, 0, kc_ref.shape[0])]

    def scores(i):
        kr, _, k0, n = chunks[i]
        return _nt_dot(kr[k0:k0 + n, :], qs)

    m = jnp.full((1, r), NEG_INF, F32)
    l = jnp.zeros((1, r), F32)
    acc = jnp.zeros((LANES, r), F32)
    s_next = scores(0)
    pending = None
    for i in range(len(chunks)):
        s = s_next
        if i + 1 < len(chunks):
            s_next = scores(i + 1)
        m_new = jnp.maximum(m, jnp.max(s, axis=0, keepdims=True))
        alpha = jnp.exp(m - m_new)
        p = jnp.exp(s - m_new)
        l = alpha * l + jnp.sum(p, axis=0, keepdims=True)
        m = m_new
        if pending is not None:
            p_prev, alpha_prev, (_, vr, k0, n) = pending
            acc = alpha_prev * acc + jnp.dot(vr[:, k0:k0 + n], p_prev, preferred_element_type=F32)
        pending = (p.astype(BF16), alpha, chunks[i])
    p_prev, alpha_prev, (_, vr, k0, n) = pending
    acc = alpha_prev * acc + jnp.dot(vr[:, k0:k0 + n], p_prev, preferred_element_type=F32)
    o = acc / l
    d = o[:, :tq] - _da_lambda(lam_ref, lam_init) * o[:, tq:]
    d = d * lax.rsqrt(jnp.mean(d * d, axis=0, keepdims=True) + EPS)
    o_ref[...] = (d.T * g_ref[...]).astype(o_ref.dtype)


def _da_call(qk_lat, qk_ctx, vt_lat, vt_ctx, lam_p, g_scaled, lam_init, batch, seq, n_ctx, tq=256, tk=512):
    nq = seq // tq
    h = DA_HEADS
    return pl.pallas_call(
        functools.partial(_da_body, lam_init, seq // tk, tk),
        out_shape=jax.ShapeDtypeStruct((batch * seq, h * 2 * DA_DIM), BF16),
        grid=(batch, h, nq),
        in_specs=[pl.BlockSpec((tq, LANES), lambda b, hh, i: (b * nq + i, hh)),
                  pl.BlockSpec((seq, LANES), lambda b, hh, i: (b, h + hh)),
                  pl.BlockSpec((LANES, seq), lambda b, hh, i: (hh, b)),
                  pl.BlockSpec((n_ctx, LANES), lambda b, hh, i: (b, h + hh)),
                  pl.BlockSpec((LANES, n_ctx), lambda b, hh, i: (hh, b)),
                  pl.BlockSpec(lam_p.shape, lambda b, hh, i: (0, 0)),
                  pl.BlockSpec((1, LANES), lambda b, hh, i: (0, 0))],
        out_specs=pl.BlockSpec((tq, LANES), lambda b, hh, i: (b * nq + i, hh)),
        compiler_params=_params("parallel", "parallel", "arbitrary"), name="da_attn",
    )(qk_lat, qk_lat, vt_lat, qk_ctx, vt_ctx, lam_p, g_scaled)


def _ctx_attn_body(lam_init, a_ref, qk_ref, lam_ref, g_ref, na_ref, da_ref):
    pairs = NA_HEADS // 2
    for p in range(pairs):
        qs = _stack_heads(a_ref[:, p * LANES:(p + 1) * LANES])
        k = a_ref[:, (pairs + p) * LANES:(pairs + p + 1) * LANES]
        v = a_ref[:, (2 * pairs + p) * LANES:(2 * pairs + p + 1) * LANES]
        s = _nt_dot(qs, k)
        e = jnp.exp(s - jnp.max(s, axis=-1, keepdims=True))
        o = jnp.dot(e.astype(BF16), v, preferred_element_type=F32) / jnp.sum(e, axis=-1, keepdims=True)
        na_ref[:, p * LANES:(p + 1) * LANES] = _unstack_heads(o).astype(na_ref.dtype)
    lam = _da_lambda(lam_ref, lam_init)
    for hh in range(DA_HEADS):
        qs = _stack_heads(qk_ref[:, hh * LANES:(hh + 1) * LANES])
        k = qk_ref[:, (DA_HEADS + hh) * LANES:(DA_HEADS + hh + 1) * LANES]
        v = a_ref[:, (3 * pairs + hh) * LANES:(3 * pairs + hh + 1) * LANES]
        r = qs.shape[0]
        init = (jnp.full((r, 1), NEG_INF, F32), jnp.zeros((r, 1), F32), jnp.zeros((r, LANES), F32))
        m, l, acc = _flash_step(qs, k, v, *init)
        da_ref[:, hh * LANES:(hh + 1) * LANES] = _da_finish(acc, l, lam, g_ref[...]).astype(da_ref.dtype)


def _ctx_attn_call(a_ctx, qk_ctx, lam_p, g_scaled, lam_init, batch, n_ctx):
    return pl.pallas_call(
        functools.partial(_ctx_attn_body, lam_init),
        out_shape=(jax.ShapeDtypeStruct((batch * n_ctx, NA_HEADS * NA_DIM), BF16),
                   jax.ShapeDtypeStruct((batch * n_ctx, DA_HEADS * 2 * DA_DIM), BF16)),
        grid=(batch,),
        in_specs=[pl.BlockSpec((n_ctx, a_ctx.shape[1]), lambda b: (b, 0)),
                  pl.BlockSpec((n_ctx, qk_ctx.shape[1]), lambda b: (b, 0)),
                  pl.BlockSpec(lam_p.shape, lambda b: (0, 0)),
                  pl.BlockSpec((1, LANES), lambda b: (0, 0))],
        out_specs=(pl.BlockSpec((n_ctx, NA_HEADS * NA_DIM), lambda b: (b, 0)),
                   pl.BlockSpec((n_ctx, DA_HEADS * 2 * DA_DIM), lambda b: (b, 0))),
        compiler_params=_params("parallel"), name="ctx_attn",
    )(a_ctx, qk_ctx, lam_p, g_scaled)


def _conv_seg(u, w, bias):
    n = u.shape[0]
    t = lax.broadcasted_iota(jnp.int32, u.shape, 0)
    acc = jnp.zeros_like(u) + bias
    for kk in range(SSD_CONV):
        d = kk - SSD_CONV // 2
        if d == 0:
            sh = u
        else:
            sh = pltpu.roll(u, (-d) % n, 0)
            sh = jnp.where((t + d >= 0) & (t + d < n), sh, 0.0)
        acc = acc + sh * w[kk:kk + 1, :]
    return acc * jax.nn.sigmoid(acc)


def _ssd_prep_body(seq, n_zc, lat_ref, ctx_ref, w_ref, b_ref, o_ref):
    j = pl.program_id(1)
    is_conv = (j >= n_zc) & (j < n_zc + SSD_XBC // LANES)

    @pl.when(is_conv)
    def _():
        w = w_ref[...]
        bias = b_ref[...]
        o_ref[0, :seq, :] = _conv_seg(lat_ref[...], w, bias)
        o_ref[0, seq:, :] = _conv_seg(ctx_ref[...], w, bias)

    @pl.when(jnp.logical_not(is_conv))
    def _():
        o_ref[0, :seq, :] = lat_ref[...]
        o_ref[0, seq:, :] = ctx_ref[...]


def _ssd_prep_call(s_lat, s_ctx, conv_w, conv_b, batch, seq, n_ctx):
    n_zc = SSD_INNER // LANES
    ncol = SSD_IN_W // LANES
    n_cw = SSD_XBC // LANES
    wcol = lambda b, j: (0, jnp.clip(j - n_zc, 0, n_cw - 1))
    return pl.pallas_call(
        functools.partial(_ssd_prep_body, seq, n_zc),
        out_shape=jax.ShapeDtypeStruct((batch, seq + n_ctx, SSD_IN_W), F32),
        grid=(batch, ncol),
        in_specs=[pl.BlockSpec((seq, LANES), lambda b, j: (b, j)),
                  pl.BlockSpec((n_ctx, LANES), lambda b, j: (b, j)),
                  pl.BlockSpec((SSD_CONV, LANES), wcol),
                  pl.BlockSpec((1, LANES), wcol)],
        out_specs=pl.BlockSpec((1, seq + n_ctx, LANES), lambda b, j: (b, 0, j)),
        compiler_params=_params("parallel", "parallel"), name="ssd_prep",
    )(s_lat, s_ctx, conv_w, conv_b.reshape(1, -1))


def _softplus(x):
    return jnp.maximum(x, 0.0) + jnp.log1p(jnp.exp(-jnp.abs(x)))


def _ssd_chunk(direction, xs_ref, bc_ref, dt_ref, dtt_ref, pr_ref, pc_ref, dexp_ref, s_ref):
    q = SSD_CHUNK
    nh = SSD_HEADS
    hi = lax.Precision.HIGHEST
    d0 = direction * nh
    dt = _softplus(dt_ref[0][:, d0:d0 + nh] + pr_ref[0:1, :])
    a = dt * (-jnp.exp(pr_ref[1:2, :]))
    dtt = _softplus(dtt_ref[0][d0:d0 + nh, :] + pc_ref[:, 0:1])
    at = dtt * (-jnp.exp(pc_ref[:, 1:2]))
    ii = lax.broadcasted_iota(jnp.int32, (q, q), 0)
    jj = lax.broadcasted_iota(jnp.int32, (q, q), 1)
    tri = (jj <= ii) if direction == 0 else (jj >= ii)
    tri_f = tri.astype(F32)
    cs = jnp.dot(tri_f, a, precision=hi, preferred_element_type=F32)
    cst = lax.dot_general(at, tri_f, (((1,), (1,)), ((), ())), precision=hi,
                          preferred_element_type=F32)
    edge = q - 1 if direction == 0 else 0
    tot_t = cst[:, edge:edge + 1]
    dst = jnp.exp(tot_t - cst) * dtt
    cst_dt = cst - jnp.log(dtt)
    etot = jnp.exp(cs[edge:edge + 1, :])

    lo = _lane_lt64((q, LANES))
    groups = []
    for g in range(SSD_GROUPS):
        b_g = bc_ref[0][:, g * SSD_STATE:(g + 1) * SSD_STATE]
        c_g = bc_ref[0][:, (SSD_GROUPS + g) * SSD_STATE:(SSD_GROUPS + g + 1) * SSD_STATE]
        groups.append((c_g, _nt_dot(c_g.astype(BF16), b_g.astype(BF16)), b_g.T))
    ys = []
    for pair in range(nh // 2):
        c_g, cb, bt_g = groups[(2 * pair) // (nh // SSD_GROUPS)]
        xs_p = xs_ref[0][:, pair * LANES:(pair + 1) * LANES]
        r0, r1 = 2 * pair, 2 * pair + 1
        xb = xs_p.astype(BF16)
        zero_b = jnp.zeros_like(xb)
        xb_bd = jnp.concatenate([jnp.where(lo, xb, zero_b), jnp.where(lo, zero_b, xb)], axis=0)
        s_p = s_ref[pair]
        s_b = s_p.astype(BF16)
        s_bd = jnp.concatenate([jnp.where(lo, s_b, zero_b), jnp.where(lo, zero_b, s_b)], axis=0)
        lhs, lhs_state, bts = [], [], []
        for r in (r0, r1):
            colb = jnp.broadcast_to(cs[:, r:r + 1], (q, q))
            rowb = jnp.broadcast_to(cst_dt[r:r + 1, :], (q, q))
            decay = jnp.exp(jnp.where(tri, colb - rowb, NEG_INF))
            lhs.append((cb * decay).astype(BF16))
            lhs_state.append((c_g * jnp.exp(colb)).astype(BF16))
            bts.append((bt_g * jnp.broadcast_to(dst[r:r + 1, :], (SSD_STATE, q))).astype(BF16))
        y = jnp.dot(jnp.concatenate(lhs + lhs_state, axis=1), jnp.concatenate([xb_bd, s_bd], axis=0),
                    preferred_element_type=F32)
        y = y + xs_p * dexp_ref[:, pair * LANES:(pair + 1) * LANES]
        dec_p = jnp.where(_lane_lt64((1, LANES)), jnp.broadcast_to(etot[:, r0:r0 + 1], (1, LANES)),
                          jnp.broadcast_to(etot[:, r1:r1 + 1], (1, LANES)))
        s_ref[pair] = s_p * dec_p + jnp.dot(jnp.concatenate(bts, axis=1), xb_bd, preferred_element_type=F32)
        ys.append(y)
    return jnp.concatenate(ys, axis=1)


def _ssd_fwd_body(xs_ref, bc_ref, dt_ref, dtt_ref, pr_ref, pc_ref, dexp_ref, y_ref, s_ref):
    @pl.when(pl.program_id(1) == 0)
    def _():
        s_ref[...] = jnp.zeros_like(s_ref)

    y_ref[0] = _ssd_chunk(0, xs_ref, bc_ref, dt_ref, dtt_ref, pr_ref, pc_ref, dexp_ref, s_ref)


def _ssd_bwd_body(xs_ref, bc_ref, dt_ref, dtt_ref, pr_ref, pc_ref, dexp_ref, yf_ref, z_ref, g_ref, o_ref, s_ref):
    @pl.when(pl.program_id(1) == 0)
    def _():
        s_ref[...] = jnp.zeros_like(s_ref)

    y = _ssd_chunk(1, xs_ref, bc_ref, dt_ref, dtt_ref, pr_ref, pc_ref, dexp_ref, s_ref) + yf_ref[0]
    z = z_ref[0]
    gated = y * (z * jax.nn.sigmoid(z))
    gw = SSD_INNER // SSD_GROUPS
    parts = []
    for g in range(SSD_GROUPS):
        gg = gated[:, g * gw:(g + 1) * gw]
        parts.append(gg * lax.rsqrt(jnp.mean(gg * gg, axis=-1, keepdims=True) + EPS))
    o_ref[0] = (jnp.concatenate(parts, axis=1) * g_ref[...]).astype(o_ref.dtype)


def _ssd_call(u, dtt, dt_bias, a_log, d_skip, norm_g, batch, seq, n_ctx):
    q = SSD_CHUNK
    n_lat, n_c = seq // q, n_ctx // q
    n_ch = n_lat + n_c
    nh = SSD_HEADS
    fwd_chunk = lambda s: jnp.where(s < n_c, n_lat + s, s - n_c)
    bwd_chunk = lambda s: n_ch - 1 - s
    xs_blk = SSD_INNER // SSD_INNER
    bc_blk = (2 * SSD_INNER) // (2 * SSD_GROUPS * SSD_STATE)
    dt_blk = (SSD_INNER + SSD_XBC) // LANES
    dexp = [jnp.repeat(d_skip[d], SSD_HEAD_DIM).reshape(1, SSD_INNER) for d in range(2)]
    prow = [jnp.stack([dt_bias[d], a_log[d]], axis=0) for d in range(2)]
    pcol = [jnp.stack([dt_bias[d], a_log[d]], axis=1) for d in range(2)]

    def specs(chunk):
        return [pl.BlockSpec((1, q, SSD_INNER), lambda b, s: (b, chunk(s), xs_blk)),
                pl.BlockSpec((1, q, 2 * SSD_GROUPS * SSD_STATE), lambda b, s: (b, chunk(s), bc_blk)),
                pl.BlockSpec((1, q, LANES), lambda b, s: (b, chunk(s), dt_blk)),
                pl.BlockSpec((1, 2 * nh, q), lambda b, s: (b, 0, chunk(s))),
                pl.BlockSpec((2, nh), lambda b, s: (0, 0)),
                pl.BlockSpec((nh, 2), lambda b, s: (0, 0)),
                pl.BlockSpec((1, SSD_INNER), lambda b, s: (0, 0))]

    state = pltpu.VMEM((nh // 2, SSD_STATE, LANES), F32)
    yblk = lambda chunk: pl.BlockSpec((1, q, SSD_INNER), lambda b, s: (b, chunk(s), 0))
    y_f = pl.pallas_call(
        _ssd_fwd_body,
        out_shape=jax.ShapeDtypeStruct((batch, seq + n_ctx, SSD_INNER), F32),
        grid=(batch, n_ch), in_specs=specs(fwd_chunk), out_specs=yblk(fwd_chunk),
        scratch_shapes=[state], compiler_params=_params("parallel", "arbitrary"), name="ssd_fwd",
    )(u, u, u, dtt, prow[0], pcol[0], dexp[0])
    return pl.pallas_call(
        _ssd_bwd_body,
        out_shape=jax.ShapeDtypeStruct((batch, seq + n_ctx, SSD_INNER), BF16),
        grid=(batch, n_ch),
        in_specs=specs(bwd_chunk) + [yblk(bwd_chunk), yblk(bwd_chunk),
                                     pl.BlockSpec((1, SSD_INNER), lambda b, s: (0, 0))],
        out_specs=yblk(bwd_chunk),
        scratch_shapes=[state], compiler_params=_params("parallel", "arbitrary"), name="ssd_bwd",
    )(u, u, u, dtt, prow[1], pcol[1], dexp[1], y_f, u, norm_g.reshape(1, SSD_INNER))


def _residual_norm(x, f, gate, g_post, pre):
    x = x + gate * _rms(f, g_post)
    if pre is None:
        return x, None
    g_pre, shift, scale = pre
    return x, (_rms(x, g_pre) * (1.0 + scale) + shift).astype(BF16)


def _wout_body(na_ref, da_ref, ssd_ref, w_ref, x_ref, gate_ref, gpost_ref, gpre_ref, shift_ref, scale_ref,
               xo_ref, h_ref):
    k1 = na_ref.shape[1]
    k2 = k1 + da_ref.shape[1]
    acc = jnp.dot(na_ref[...], w_ref[:k1, :], preferred_element_type=F32)
    acc = acc + jnp.dot(da_ref[...], w_ref[k1:k2, :], preferred_element_type=F32)
    acc = acc + jnp.dot(ssd_ref[0], w_ref[k2:, :], preferred_element_type=F32)
    xo_ref[...], h_ref[...] = _residual_norm(x_ref[...], acc, gate_ref[0], gpost_ref[...],
                                             (gpre_ref[...], shift_ref[0], scale_ref[0]))


def _wout_call(na, da, ssd, w, x, sel, res, pre, ssd_tile, tm):
    m = na.shape[0]
    d = w.shape[1]
    row = pl.BlockSpec((tm, d), lambda i: (i, 0))
    vec = pl.BlockSpec((1, d), lambda i: (0, 0))
    tab = pl.BlockSpec((1, 1, d), lambda i: (sel(i), 0, 0))
    return pl.pallas_call(
        _wout_body,
        out_shape=(jax.ShapeDtypeStruct((m, d), F32), jax.ShapeDtypeStruct((m, d), BF16)),
        grid=(m // tm,),
        in_specs=[pl.BlockSpec((tm, na.shape[1]), lambda i: (i, 0)),
                  pl.BlockSpec((tm, da.shape[1]), lambda i: (i, 0)),
                  pl.BlockSpec((1, tm, ssd.shape[2]), lambda i: ssd_tile(i) + (0,)),
                  pl.BlockSpec(w.shape, lambda i: (0, 0)),
                  row, tab, vec, vec, tab, tab],
        out_specs=(row, row),
        compiler_params=_params("parallel"), name="w_out",
    )(na, da, ssd, w, x, res[0], res[1], *pre)


def _top_vals(x, k, with_rank=False):
    vals = []
    rank = jnp.full(x.shape, float(k), F32) if with_rank else None
    for i in range(k):
        m = jnp.max(x, axis=0, keepdims=True)
        vals.append(m)
        hit = x == m
        if with_rank:
            rank = jnp.where(hit, float(i), rank)
        x = jnp.where(hit, NEG_INF, x)
    vals = jnp.concatenate(vals, axis=0)
    return (vals, rank) if with_rank else vals


def _pair_candidates(v1, v2):
    k = PEER_TOPK
    half = k // 2
    row = lax.broadcasted_iota(jnp.int32, (half, v1.shape[1]), 0)
    parts = [v2 + v1[0:1, :]]
    for i in range(1, half):
        parts.append(jnp.where(row < k // (i + 1), v2[:half] + v1[i:i + 1, :], NEG_INF))
    parts.append(v1[half:] + v2[0:1, :])
    return jnp.concatenate(parts, axis=0)


def _peer_sel_body(h_ref, wq_ref, keys_ref, cnt_ref, e1_ref, rank_ref, e2_ref):
    qd = PEER_QDIM // 2
    k = PEER_TOPK
    qv = jnp.dot(h_ref[...], wq_ref[...], preferred_element_type=F32).astype(BF16)
    for hh in range(PEER_HEADS):
        c0 = hh * PEER_QDIM
        s1 = _nt_dot(keys_ref[2 * hh], qv[:, c0:c0 + qd])
        s2 = _nt_dot(keys_ref[2 * hh + 1], qv[:, c0 + qd:c0 + 2 * qd])
        v1 = _top_vals(s1, k)
        v2, rank = _top_vals(s2, k, with_rank=True)
        top = _top_vals(_pair_candidates(v1, v2), k)
        thr = top[k - 1:k, :]
        z = jnp.sum(jnp.exp(top - top[0:1, :]), axis=0, keepdims=True)
        cnt_sorted = jnp.zeros_like(v1)
        for r in range(k):
            cnt_sorted = cnt_sorted + jnp.where(v1 + v2[r:r + 1, :] >= thr, 1.0, 0.0)
        cnt = jnp.zeros_like(s1)
        for r in range(k):
            cnt = jnp.where(s1 == v1[r:r + 1, :], cnt_sorted[r:r + 1, :], cnt)
        cnt_ref[hh] = cnt
        rank_ref[hh] = rank.astype(BF16)
        e1_ref[hh] = jnp.exp(s1 - v1[0:1, :])
        e2_ref[hh] = (jnp.exp(s2 - v2[0:1, :]) / z).astype(BF16)


def _peer_sel_call(h, wq, keys, tm=256):
    t, d = h.shape
    shape = (PEER_HEADS, PEER_N_KEYS, t)
    sblk = pl.BlockSpec((PEER_HEADS, PEER_N_KEYS, tm), lambda i: (0, 0, i))
    return pl.pallas_call(
        _peer_sel_body,
        out_shape=(jax.ShapeDtypeStruct(shape, F32), jax.ShapeDtypeStruct(shape, F32),
                   jax.ShapeDtypeStruct(shape, BF16), jax.ShapeDtypeStruct(shape, BF16)),
        grid=(t // tm,),
        in_specs=[pl.BlockSpec((tm, d), lambda i: (i, 0)),
                  pl.BlockSpec(wq.shape, lambda i: (0, 0)),
                  pl.BlockSpec(keys.shape, lambda i: (0, 0, 0))],
        out_specs=(sblk, sblk, sblk, sblk),
        compiler_params=_params("parallel"), name="peer_select",
    )(h, wq, keys)


def _gelu_tanh(x):
    return 0.5 * x * (1.0 + jnp.tanh(math.sqrt(2.0 / math.pi) * (x + 0.044715 * (x * x * x))))


def _peer_ffn_body(n_a, n_split, has_pre, h_ref, u_ref, vt_ref, cnt_ref, e1_ref, rank_ref, e2_ref,
                   x_ref, gate_ref, gpost_ref, *rest):
    if has_pre:
        gpre_ref, shift_ref, scale_ref, xo_ref, ho_ref, acc_ref, wt_ref = rest
    else:
        xo_ref, acc_ref, wt_ref = rest
    j = pl.program_id(1)

    @pl.when(j == 0)
    def _():
        acc_ref[...] = jnp.zeros_like(acc_ref)

    nk = PEER_N_KEYS
    a_part = n_a // n_split
    rows = a_part * nk
    h = h_ref[...]

    def act(s):
        return _gelu_tanh(_nt_dot(u_ref[s * rows:(s + 1) * rows, :], h)).astype(BF16)

    act_next = act(0)
    for s in range(n_split):
        act_s = act_next
        if s + 1 < n_split:
            act_next = act(s + 1)
        for al in range(a_part):
            a = s * a_part + al
            gate = None
            for hh in range(PEER_HEADS):
                cnt = cnt_ref[hh, a:a + 1, :].astype(BF16)
                e1 = e1_ref[hh, a:a + 1, :].astype(BF16)
                w = jnp.where(rank_ref[hh] < cnt, e2_ref[hh], jnp.zeros((), BF16)) * e1
                gate = w if gate is None else gate + w
            wt_ref[a * nk:(a + 1) * nk, :] = gate * act_s[al * nk:(al + 1) * nk, :]
        acc_ref[...] += jnp.dot(vt_ref[:, s * rows:(s + 1) * rows], wt_ref[s * rows:(s + 1) * rows, :],
                                preferred_element_type=F32)

    @pl.when(j == pl.num_programs(1) - 1)
    def _():
        pre = (gpre_ref[...], shift_ref[0], scale_ref[0]) if has_pre else None
        x_new, h_next = _residual_norm(x_ref[...], acc_ref[...].T, gate_ref[0], gpost_ref[...], pre)
        xo_ref[...] = x_new
        if has_pre:
            ho_ref[...] = h_next


def _peer_ffn_call(h, u, vt, layer, cnt, e1, rank, e2, x, sel, res, pre, tm=TOKEN_TILE, n_a=8, n_split=2):
    t, d = h.shape
    e = u.shape[1]
    te = n_a * PEER_N_KEYS
    a_blk = pl.BlockSpec((PEER_HEADS, n_a, tm), lambda i, j: (0, j, i))
    b_blk = pl.BlockSpec((PEER_HEADS, PEER_N_KEYS, tm), lambda i, j: (0, 0, i))
    row = pl.BlockSpec((tm, d), lambda i, j: (i, 0))
    vec = pl.BlockSpec((1, d), lambda i, j: (0, 0))
    tab = pl.BlockSpec((1, 1, d), lambda i, j: (sel(i), 0, 0))
    args = [h, u, vt, cnt, e1, rank, e2, x, res[0], res[1]]
    specs = [row, pl.BlockSpec((None, te, d), lambda i, j: (layer, j, 0)),
             pl.BlockSpec((None, d, te), lambda i, j: (layer, 0, j)),
             a_blk, a_blk, b_blk, b_blk, row, tab, vec]
    outs = [jax.ShapeDtypeStruct((t, d), F32)]
    ospecs = [row]
    if pre is not None:
        args += list(pre)
        specs += [vec, tab, tab]
        outs.append(jax.ShapeDtypeStruct((t, d), BF16))
        ospecs.append(row)
    out = pl.pallas_call(
        functools.partial(_peer_ffn_body, n_a, n_split, pre is not None),
        out_shape=tuple(outs), grid=(t // tm, e // te), in_specs=specs, out_specs=tuple(ospecs),
        scratch_shapes=[pltpu.VMEM((d, tm), F32), pltpu.VMEM((te, tm), BF16)],
        compiler_params=_params("parallel", "arbitrary"), name="peer_ffn",
    )(*args)
    return (out[0], out[1]) if pre is not None else (out[0], None)


def _peer(h, wq, keys, u, vt, layer, x, sel, res, pre):
    cnt, e1, rank, e2 = _peer_sel_call(h, wq, keys)
    return _peer_ffn_call(h, u, vt, layer, cnt, e1, rank, e2, x, sel, res, pre)


def kernel(x, c, ctx, c_ctx, ada_w, ada_b, norm_g, w_in, w_out, na_rpb, ssd_conv_w, ssd_conv_b, ssd_dt_bias,
           ssd_a_log, ssd_d, ssd_norm_g, da_lambda, da_sub_g, peer_w_q, peer_sub_keys, peer_u, peer_v):
    batch, seq, d = x.shape
    n_ctx = ctx.shape[1]
    depth = ada_w.shape[0]
    xl = x.reshape(batch * seq, d)
    xc = ctx.reshape(batch * n_ctx, d)
    tm_norm = 256
    sel_lat = lambda i: i // (seq // tm_norm)
    sel_ctx = lambda i: batch
    cos_t, sin_t = _rope_tables(seq)

    o_nq, o_nk, o_nv = 0, 512, 1024
    o_dq, o_dk, o_dv = 1536, 2048, 2560
    o_z = 3072
    o_end = o_z + SSD_INNER + SSD_XBC + 2 * SSD_HEADS
    scale_a = jnp.concatenate([jnp.full((512,), NA_DIM ** -0.5, F32), jnp.ones((1536,), F32)])
    scale_b = jnp.concatenate([jnp.full((512,), DA_DIM ** -0.5, F32), jnp.ones((512,), F32)])

    cond = jnp.concatenate([c, c_ctx[None, :], jnp.zeros((8 - batch - 1, d), F32)], axis=0)
    mods = [_ada_call(cond, ada_w, ada_b, layer)[:batch + 1].reshape(batch + 1, N_MOD, 1, d)
            for layer in range(depth)]
    pu = peer_u.astype(BF16)
    pvt = jnp.swapaxes(peer_v, 1, 2).astype(BF16)
    h_l = h_c = None
    for layer in range(depth):
        need_ctx = layer < depth - 1
        lam_init = 0.8 - 0.6 * math.exp(-0.3 * layer)
        tab = [mods[layer][:, k] for k in range(N_MOD)]
        g = norm_g[layer]
        if layer == 0:
            _, h_l = _norm_call(xl, sel_lat, pre=(g[0:1], tab[0], tab[1]), tm=tm_norm)
            _, h_c = _norm_call(xc, sel_ctx, pre=(g[0:1], tab[0], tab[1]), tm=tm_norm)

        wi = w_in[layer]
        w_a = jnp.concatenate([wi[:, o_nq:o_dq], wi[:, o_dv:o_z]], axis=1).astype(BF16)
        w_b = wi[:, o_dq:o_dv].astype(BF16)
        w_c = jnp.pad(wi[:, o_z:o_end], ((0, 0), (0, SSD_IN_W - (o_end - o_z)))).astype(BF16)
        a_lat = _mm_call(h_l, w_a, BF16, colscale=scale_a, name="proj_attn")
        a_ctx = _mm_call(h_c, w_a, BF16, colscale=scale_a, name="proj_attn")
        qk_lat = _mm_call(h_l, w_b, BF16, colscale=scale_b, rope=(cos_t, sin_t), name="proj_rope")
        qk_ctx = _mm_call(h_c, w_b, BF16, colscale=scale_b, name="proj_qk_ctx")
        s_lat = _mm_call(h_l, w_c, F32, name="proj_ssd")
        s_ctx = _mm_call(h_c, w_c, F32, name="proj_ssd")

        lam_p = da_lambda[layer]
        g_sub = (da_sub_g[layer] * (1.0 - lam_init)).reshape(1, LANES)
        na_l = _na_call(a_lat, a_ctx, _na_bias(na_rpb[layer]), batch, seq, n_ctx)
        dv0 = 3 * DA_HEADS * LANES
        da_l = _da_call(qk_lat, qk_ctx, a_lat[:, dv0:].T, a_ctx[:, dv0:].T, lam_p, g_sub, lam_init,
                        batch, seq, n_ctx)

        u_ssd = _ssd_prep_call(s_lat, s_ctx, ssd_conv_w[layer], ssd_conv_b[layer], batch, seq, n_ctx)
        dtt = jnp.transpose(u_ssd[:, :, SSD_INNER + SSD_XBC:SSD_INNER + SSD_XBC + 2 * SSD_HEADS], (0, 2, 1))
        ssd_o = _ssd_call(u_ssd, dtt, ssd_dt_bias[layer], ssd_a_log[layer], ssd_d[layer], ssd_norm_g[layer],
                          batch, seq, n_ctx)

        wo = w_out[layer].astype(BF16)
        wq = peer_w_q[layer].astype(BF16)
        keys = peer_sub_keys[layer].reshape(2 * PEER_HEADS, PEER_N_KEYS, PEER_QDIM // 2).astype(BF16)
        nxt = layer + 1 < depth

        lat_tiles = seq // TOKEN_TILE
        sel_lat_t = lambda i: i // lat_tiles
        res_mix, pre_ffn, res_ffn = (tab[2], g[1:2]), (g[2:3], tab[3], tab[4]), (tab[5], g[3:4])
        pre_n = (norm_g[layer + 1][0:1], mods[layer + 1][:, 0], mods[layer + 1][:, 1]) if nxt else None
        xl, h2_l = _wout_call(na_l, da_l, ssd_o, wo, xl, sel_lat_t, res_mix, pre_ffn,
                              lambda i: (i // lat_tiles, i % lat_tiles), TOKEN_TILE)
        xl, h_l = _peer(h2_l, wq, keys, pu, pvt, layer, xl, sel_lat_t, res_ffn, pre_n)

        if need_ctx:
            na_c, da_c = _ctx_attn_call(a_ctx, qk_ctx, lam_p, g_sub, lam_init, batch, n_ctx)
            xc, h2_c = _wout_call(na_c, da_c, ssd_o, wo, xc, sel_ctx, res_mix, pre_ffn,
                                  lambda i: (i, seq // n_ctx), n_ctx)
            xc, h_c = _peer(h2_c, wq, keys, pu, pvt, layer, xc, sel_ctx, res_ffn, pre_n)
    return xl.reshape(batch, seq, d)
```

```python
import functools
import math

import numpy as np
import jax
import jax.numpy as jnp
from jax import lax
from jax.experimental import pallas as pl
from jax.experimental.pallas import tpu as pltpu

F32 = jnp.float32
BF16 = jnp.bfloat16
EPS = 1e-6
NEG_INF = float("-inf")

VMEM_LIMIT_BYTES = 58 * 1024 * 1024
LANES = 128

GRID_W = 64
NA_DIM = 64
NA_HEADS = 8
NA_WIN_ROWS = 8
NA_WIN_COLS = 16
DA_DIM = 64
DA_HEADS = 4
ROPE_BASE = 10000.0
SSD_INNER = 1024
SSD_HEAD_DIM = 64
SSD_HEADS = 16
SSD_GROUPS = 2
SSD_STATE = 128
SSD_CONV = 5
SSD_CHUNK = 128
SSD_XBC = SSD_INNER + 2 * SSD_GROUPS * SSD_STATE
SSD_IN_W = SSD_INNER + SSD_XBC + LANES
PEER_HEADS = 8
PEER_QDIM = 256
PEER_N_KEYS = 128
PEER_TOPK = 16
N_MOD = 6
TOKEN_TILE = 512


def _params(*sem):
    return pltpu.CompilerParams(dimension_semantics=sem, vmem_limit_bytes=VMEM_LIMIT_BYTES)


def _nt_dot(a, b):
    return lax.dot_general(a, b, (((1,), (1,)), ((), ())), preferred_element_type=F32)


def _lane_lt64(shape):
    return lax.broadcasted_iota(jnp.int32, shape, len(shape) - 1) < (LANES // 2)


def _ada_body(c_ref, w_ref, b_ref, o_ref):
    c = c_ref[...]
    sc = (c * jax.nn.sigmoid(c)).astype(BF16)
    o_ref[...] = jnp.dot(sc, w_ref[...].astype(BF16), preferred_element_type=F32) + b_ref[...]


def _ada_call(cond, w, b, layer, tn=768):
    m, d = cond.shape
    depth, _, n = w.shape
    return pl.pallas_call(
        _ada_body,
        out_shape=jax.ShapeDtypeStruct((m, n), F32),
        grid=(n // tn,),
        in_specs=[pl.BlockSpec((m, d), lambda j: (0, 0)),
                  pl.BlockSpec((None, d, tn), lambda j: (layer, 0, j)),
                  pl.BlockSpec((None, 1, tn), lambda j: (layer, 0, j))],
        out_specs=pl.BlockSpec((m, tn), lambda j: (0, j)),
        compiler_params=_params("arbitrary"),
        name="ada_mod",
    )(cond, w, b.reshape(depth, 1, n))


def _rms(x, g):
    return x * lax.rsqrt(jnp.mean(x * x, axis=-1, keepdims=True) + EPS) * g


def _norm_body(has_res, has_pre, *refs):
    refs = list(refs)
    x = refs.pop(0)[...]
    if has_res:
        f_ref, gate_ref, gpost_ref = refs.pop(0), refs.pop(0), refs.pop(0)
        x = x + gate_ref[0] * _rms(f_ref[...], gpost_ref[...])
    if has_pre:
        gpre_ref, shift_ref, scale_ref = refs.pop(0), refs.pop(0), refs.pop(0)
    if has_res:
        refs.pop(0)[...] = x
    if has_pre:
        h = _rms(x, gpre_ref[...]) * (1.0 + scale_ref[0]) + shift_ref[0]
        refs.pop(0)[...] = h.astype(BF16)


def _norm_call(x, sel, *, res=None, pre=None, tm=256):
    m, d = x.shape
    row = pl.BlockSpec((tm, d), lambda i: (i, 0))
    vec = pl.BlockSpec((1, d), lambda i: (0, 0))
    tab = pl.BlockSpec((1, 1, d), lambda i: (sel(i), 0, 0))
    args, specs, outs, ospecs = [x], [row], [], []
    if res is not None:
        args += list(res)
        specs += [row, tab, vec]
        outs.append(jax.ShapeDtypeStruct((m, d), F32))
        ospecs.append(row)
    if pre is not None:
        args += list(pre)
        specs += [vec, tab, tab]
        outs.append(jax.ShapeDtypeStruct((m, d), BF16))
        ospecs.append(row)
    res_out = pl.pallas_call(
        functools.partial(_norm_body, res is not None, pre is not None),
        out_shape=tuple(outs), grid=(m // tm,), in_specs=specs, out_specs=tuple(ospecs),
        compiler_params=_params("parallel"), name="norm_mod",
    )(*args)
    res_out = list(res_out)
    x_new = res_out.pop(0) if res is not None else None
    h = res_out.pop(0) if pre is not None else None
    return x_new, h


def _rope_partner(x):
    lane = lax.broadcasted_iota(jnp.int32, x.shape, 1)
    first = (lane % 32) < 16
    return jnp.where(first, pltpu.roll(x, LANES - 16, 1), pltpu.roll(x, 16, 1))


def _mm_body(epi, h_ref, w_ref, *rest):
    acc = jnp.dot(h_ref[...], w_ref[...], preferred_element_type=F32)
    if epi == "plain":
        (o_ref,) = rest
    elif epi == "scale":
        cs_ref, o_ref = rest
        acc = acc * cs_ref[...]
    else:
        cos_ref, sin_ref, cs_ref, o_ref = rest
        cos = cos_ref[...]
        sin = sin_ref[...]
        n = acc.shape[1]
        blocks = []
        for c in range(n // LANES):
            xb = acc[:, c * LANES:(c + 1) * LANES]
            blocks.append(xb * cos + _rope_partner(xb) * sin)
        acc = jnp.concatenate(blocks, axis=1) * cs_ref[...]
    o_ref[...] = acc.astype(o_ref.dtype)


def _mm_call(h, w, out_dtype, *, colscale=None, rope=None, tm=512, tn=None, name="proj"):
    m, k = h.shape
    n = w.shape[1]
    tn = n if tn is None else tn
    args = [h, w]
    specs = [pl.BlockSpec((tm, k), lambda i, j: (i, 0)),
             pl.BlockSpec((k, tn), lambda i, j: (0, j))]
    epi = "plain"
    if rope is not None:
        cos_t, sin_t = rope
        nt = cos_t.shape[0] // tm
        args += [cos_t, sin_t]
        specs += [pl.BlockSpec((tm, LANES), lambda i, j: (i % nt, 0))] * 2
        epi = "rope"
    if colscale is not None:
        args.append(colscale.reshape(1, n))
        specs.append(pl.BlockSpec((1, tn), lambda i, j: (0, j)))
        epi = "scale" if epi == "plain" else epi
    return pl.pallas_call(
        functools.partial(_mm_body, epi),
        out_shape=jax.ShapeDtypeStruct((m, n), out_dtype),
        grid=(m // tm, n // tn), in_specs=specs,
        out_specs=pl.BlockSpec((tm, tn), lambda i, j: (i, j)),
        compiler_params=_params("parallel", "arbitrary"), name=name,
    )(*args)


def _rope_tables(seq):
    quarter = DA_DIM // 4
    inv_freq = 1.0 / (ROPE_BASE ** (jnp.arange(quarter, dtype=F32) / quarter))
    t = jnp.arange(seq)
    ang_r = (t // GRID_W).astype(F32)[:, None] * inv_freq
    ang_c = (t % GRID_W).astype(F32)[:, None] * inv_freq

    def unit(ang):
        return (jnp.concatenate([jnp.cos(ang), jnp.cos(ang)], -1),
                jnp.concatenate([-jnp.sin(ang), jnp.sin(ang)], -1))

    cr, sr = unit(ang_r)
    cc, sc = unit(ang_c)
    cos64 = jnp.concatenate([cr, cc], -1)
    sin64 = jnp.concatenate([sr, sc], -1)
    return jnp.tile(cos64, (1, 2)), jnp.tile(sin64, (1, 2))


def _stack_heads(q2):
    lo = _lane_lt64(q2.shape)
    zero = jnp.zeros_like(q2)
    return jnp.concatenate([jnp.where(lo, q2, zero), jnp.where(lo, zero, q2)], axis=0)


def _unstack_heads(o):
    r = o.shape[0] // 2
    return jnp.where(_lane_lt64((r, o.shape[1])), o[:r], o[r:])


NA_ROW_GROUP = 8


def _na_body(rows, q_ref, k_ref, v_ref, kc_ref, vc_ref, bias_ref, o_ref):
    kc = kc_ref[...]
    vc = vc_ref[...]
    kh = NA_WIN_ROWS

    def scores(r):
        rs = jnp.clip(r - kh // 2, 0, rows - kh)
        q0 = pl.multiple_of(r * GRID_W, GRID_W)
        k0 = pl.multiple_of(rs * GRID_W, GRID_W)
        qs = _stack_heads(q_ref[pl.ds(q0, GRID_W), :])
        s_loc = _nt_dot(qs, k_ref[pl.ds(k0, kh * GRID_W), :]) + bias_ref[0, r - rs]
        return q0, k0, s_loc, _nt_dot(qs, kc)

    def finish(q0, k0, p_loc, p_ctx, l):
        o = (jnp.dot(p_loc, v_ref[pl.ds(k0, kh * GRID_W), :], preferred_element_type=F32)
             + jnp.dot(p_ctx, vc, preferred_element_type=F32)) / l
        o_ref[pl.ds(q0, GRID_W), :] = _unstack_heads(o).astype(o_ref.dtype)

    def row_group(gi, carry):
        nxt = scores(gi * NA_ROW_GROUP)
        pending = None
        for j in range(NA_ROW_GROUP):
            q0, k0, s_loc, s_ctx = nxt
            if j + 1 < NA_ROW_GROUP:
                nxt = scores(gi * NA_ROW_GROUP + j + 1)
            m = jnp.maximum(jnp.max(s_loc, axis=-1, keepdims=True), jnp.max(s_ctx, axis=-1, keepdims=True))
            p_loc = jnp.exp(s_loc - m)
            p_ctx = jnp.exp(s_ctx - m)
            l = jnp.sum(p_loc, axis=-1, keepdims=True) + jnp.sum(p_ctx, axis=-1, keepdims=True)
            if pending is not None:
                finish(*pending)
            pending = (q0, k0, p_loc.astype(BF16), p_ctx.astype(BF16), l)
        finish(*pending)
        return carry

    lax.fori_loop(0, rows // NA_ROW_GROUP, row_group, 0)


def _na_bias(rpb):
    kh, w = NA_WIN_ROWS, GRID_W
    c = np.arange(w)
    wstart = np.clip(c - NA_WIN_COLS // 2, 0, w - NA_WIN_COLS)
    kc = np.arange(w)
    valid = (kc[None, :] >= wstart[:, None]) & (kc[None, :] < wstart[:, None] + NA_WIN_COLS)
    cidx = np.clip(kc[None, :] - c[:, None] + NA_WIN_COLS - 1, 0, 2 * NA_WIN_COLS - 2)
    typ = np.arange(kh)
    ridx = np.arange(kh)[None, :] - typ[:, None] + NA_WIN_ROWS - 1
    pick_r = (ridx[:, :, None] == np.arange(2 * NA_WIN_ROWS - 1)).astype(np.float32)
    pick_c = ((cidx[:, :, None] == np.arange(2 * NA_WIN_COLS - 1)) & valid[:, :, None]).astype(np.float32)
    b = jnp.einsum('hrc,tjr,qkc->htqjk', rpb.astype(F32), pick_r, pick_c, precision=lax.Precision.HIGHEST)
    b = b + np.where(valid, 0.0, NEG_INF).astype(np.float32)[None, None, :, None, :]
    b = b.reshape(NA_HEADS // 2, 2, kh, w, kh * w)
    return jnp.transpose(b, (0, 2, 1, 3, 4)).reshape(NA_HEADS // 2, kh, 2 * w, kh * w).astype(F32)


def _na_call(a_lat, a_ctx, bias, batch, seq, n_ctx):
    rows = seq // GRID_W
    pairs = NA_HEADS // 2
    lat = lambda c0: pl.BlockSpec((seq, LANES), lambda b, p: (b, c0 + p))
    ctx = lambda c0: pl.BlockSpec((n_ctx, LANES), lambda b, p: (b, c0 + p))
    return pl.pallas_call(
        functools.partial(_na_body, rows),
        out_shape=jax.ShapeDtypeStruct((batch * seq, NA_HEADS * NA_DIM), BF16),
        grid=(batch, pairs),
        in_specs=[lat(0), lat(pairs), lat(2 * pairs), ctx(pairs), ctx(2 * pairs),
                  pl.BlockSpec((1,) + bias.shape[1:], lambda b, p: (p, 0, 0, 0))],
        out_specs=pl.BlockSpec((seq, LANES), lambda b, p: (b, p)),
        compiler_params=_params("parallel", "parallel"), name="na_attn",
    )(a_lat, a_lat, a_lat, a_ctx, a_ctx, bias)


def _da_lambda(lam_ref, lam_init):
    lf = lam_ref[...]
    a = jnp.sum(lf[0:1] * lf[1:2], axis=-1, keepdims=True)
    b = jnp.sum(lf[2:3] * lf[3:4], axis=-1, keepdims=True)
    return jnp.exp(a) - jnp.exp(b) + lam_init


def _da_finish(acc, l, lam, g):
    tq = acc.shape[0] // 2
    o = acc / l
    d = o[:tq] - lam * o[tq:]
    return d * lax.rsqrt(jnp.mean(d * d, axis=-1, keepdims=True) + EPS) * g


def _flash_step(qs, k, v, m, l, acc):
    s = _nt_dot(qs, k)
    m_new = jnp.maximum(m, jnp.max(s, axis=-1, keepdims=True))
    alpha = jnp.exp(m - m_new)
    p = jnp.exp(s - m_new)
    l = alpha * l + jnp.sum(p, axis=-1, keepdims=True)
    acc = alpha * acc + jnp.dot(p.astype(BF16), v, preferred_element_type=F32)
    return m_new, l, acc


def _da_body(lam_init, n_chunks, tk, q_ref, k_ref, v_ref, kc_ref, vc_ref, lam_ref, g_ref, o_ref):
    qs = _stack_heads(q_ref[...])
    r = qs.shape[0]
    chunks = [(k_ref, v_ref, c * tk, tk) for c in range(n_chunks)] + [(kc_ref, vc_ref, 0, kc_ref.shape[0])]

    def scores(i):
        kr, _, k0, n = chunks[i]
        return _nt_dot(qs, kr[k0:k0 + n, :])

    m = jnp.full((r, 1), NEG_INF, F32)
    l = jnp.zeros((r, 1), F32)
    acc = jnp.zeros((r, LANES), F32)
    s_next = scores(0)
    pending = None
    for i in range(len(chunks)):
        s = s_next
        if i + 1 < len(chunks):
            s_next = scores(i + 1)
        m_new = jnp.maximum(m, jnp.max(s, axis=-1, keepdims=True))
        alpha = jnp.exp(m - m_new)
        p = jnp.exp(s - m_new)
        l = alpha * l + jnp.sum(p, axis=-1, keepdims=True)
        m = m_new
        if pending is not None:
            p_prev, alpha_prev, (_, vr, k0, n) = pending
            acc = alpha_prev * acc + jnp.dot(p_prev, vr[k0:k0 + n, :], preferred_element_type=F32)
        pending = (p.astype(BF16), alpha, chunks[i])
    p_prev, alpha_prev, (_, vr, k0, n) = pending
    acc = alpha_prev * acc + jnp.dot(p_prev, vr[k0:k0 + n, :], preferred_element_type=F32)
    o_ref[...] = _da_finish(acc, l, _da_lambda(lam_ref, lam_init), g_ref[...]).astype(o_ref.dtype)


def _da_call(qk_lat, qk_ctx, a_lat, a_ctx, lam_p, g_scaled, lam_init, batch, seq, n_ctx, tq=256, tk=512):
    nq = seq // tq
    h = DA_HEADS
    return pl.pallas_call(
        functools.partial(_da_body, lam_init, seq // tk, tk),
        out_shape=jax.ShapeDtypeStruct((batch * seq, h * 2 * DA_DIM), BF16),
        grid=(batch, h, nq),
        in_specs=[pl.BlockSpec((tq, LANES), lambda b, hh, i: (b * nq + i, hh)),
                  pl.BlockSpec((seq, LANES), lambda b, hh, i: (b, h + hh)),
                  pl.BlockSpec((seq, LANES), lambda b, hh, i: (b, 3 * h + hh)),
                  pl.BlockSpec((n_ctx, LANES), lambda b, hh, i: (b, h + hh)),
                  pl.BlockSpec((n_ctx, LANES), lambda b, hh, i: (b, 3 * h + hh)),
                  pl.BlockSpec(lam_p.shape, lambda b, hh, i: (0, 0)),
                  pl.BlockSpec((1, LANES), lambda b, hh, i: (0, 0))],
        out_specs=pl.BlockSpec((tq, LANES), lambda b, hh, i: (b * nq + i, hh)),
        compiler_params=_params("parallel", "parallel", "arbitrary"), name="da_attn",
    )(qk_lat, qk_lat, a_lat, qk_ctx, a_ctx, lam_p, g_scaled)


def _ctx_attn_body(lam_init, a_ref, qk_ref, lam_ref, g_ref, na_ref, da_ref):
    pairs = NA_HEADS // 2
    for p in range(pairs):
        qs = _stack_heads(a_ref[:, p * LANES:(p + 1) * LANES])
        k = a_ref[:, (pairs + p) * LANES:(pairs + p + 1) * LANES]
        v = a_ref[:, (2 * pairs + p) * LANES:(2 * pairs + p + 1) * LANES]
        s = _nt_dot(qs, k)
        e = jnp.exp(s - jnp.max(s, axis=-1, keepdims=True))
        o = jnp.dot(e.astype(BF16), v, preferred_element_type=F32) / jnp.sum(e, axis=-1, keepdims=True)
        na_ref[:, p * LANES:(p + 1) * LANES] = _unstack_heads(o).astype(na_ref.dtype)
    lam = _da_lambda(lam_ref, lam_init)
    for hh in range(DA_HEADS):
        qs = _stack_heads(qk_ref[:, hh * LANES:(hh + 1) * LANES])
        k = qk_ref[:, (DA_HEADS + hh) * LANES:(DA_HEADS + hh + 1) * LANES]
        v = a_ref[:, (3 * pairs + hh) * LANES:(3 * pairs + hh + 1) * LANES]
        r = qs.shape[0]
        init = (jnp.full((r, 1), NEG_INF, F32), jnp.zeros((r, 1), F32), jnp.zeros((r, LANES), F32))
        m, l, acc = _flash_step(qs, k, v, *init)
        da_ref[:, hh * LANES:(hh + 1) * LANES] = _da_finish(acc, l, lam, g_ref[...]).astype(da_ref.dtype)


def _ctx_attn_call(a_ctx, qk_ctx, lam_p, g_scaled, lam_init, batch, n_ctx):
    return pl.pallas_call(
        functools.partial(_ctx_attn_body, lam_init),
        out_shape=(jax.ShapeDtypeStruct((batch * n_ctx, NA_HEADS * NA_DIM), BF16),
                   jax.ShapeDtypeStruct((batch * n_ctx, DA_HEADS * 2 * DA_DIM), BF16)),
        grid=(batch,),
        in_specs=[pl.BlockSpec((n_ctx, a_ctx.shape[1]), lambda b: (b, 0)),
                  pl.BlockSpec((n_ctx, qk_ctx.shape[1]), lambda b: (b, 0)),
                  pl.BlockSpec(lam_p.shape, lambda b: (0, 0)),
                  pl.BlockSpec((1, LANES), lambda b: (0, 0))],
        out_specs=(pl.BlockSpec((n_ctx, NA_HEADS * NA_DIM), lambda b: (b, 0)),
                   pl.BlockSpec((n_ctx, DA_HEADS * 2 * DA_DIM), lambda b: (b, 0))),
        compiler_params=_params("parallel"), name="ctx_attn",
    )(a_ctx, qk_ctx, lam_p, g_scaled)


def _conv_seg(u, w, bias):
    n = u.shape[0]
    t = lax.broadcasted_iota(jnp.int32, u.shape, 0)
    acc = jnp.zeros_like(u) + bias
    for kk in range(SSD_CONV):
        d = kk - SSD_CONV // 2
        if d == 0:
            sh = u
        else:
            sh = pltpu.roll(u, (-d) % n, 0)
            sh = jnp.where((t + d >= 0) & (t + d < n), sh, 0.0)
        acc = acc + sh * w[kk:kk + 1, :]
    return acc * jax.nn.sigmoid(acc)


def _ssd_prep_body(seq, n_zc, lat_ref, ctx_ref, w_ref, b_ref, o_ref):
    j = pl.program_id(1)
    is_conv = (j >= n_zc) & (j < n_zc + SSD_XBC // LANES)

    @pl.when(is_conv)
    def _():
        w = w_ref[...]
        bias = b_ref[...]
        o_ref[0, :seq, :] = _conv_seg(lat_ref[...], w, bias)
        o_ref[0, seq:, :] = _conv_seg(ctx_ref[...], w, bias)

    @pl.when(jnp.logical_not(is_conv))
    def _():
        o_ref[0, :seq, :] = lat_ref[...]
        o_ref[0, seq:, :] = ctx_ref[...]


def _ssd_prep_call(s_lat, s_ctx, conv_w, conv_b, batch, seq, n_ctx):
    n_zc = SSD_INNER // LANES
    ncol = SSD_IN_W // LANES
    n_cw = SSD_XBC // LANES
    wcol = lambda b, j: (0, jnp.clip(j - n_zc, 0, n_cw - 1))
    return pl.pallas_call(
        functools.partial(_ssd_prep_body, seq, n_zc),
        out_shape=jax.ShapeDtypeStruct((batch, seq + n_ctx, SSD_IN_W), F32),
        grid=(batch, ncol),
        in_specs=[pl.BlockSpec((seq, LANES), lambda b, j: (b, j)),
                  pl.BlockSpec((n_ctx, LANES), lambda b, j: (b, j)),
                  pl.BlockSpec((SSD_CONV, LANES), wcol),
                  pl.BlockSpec((1, LANES), wcol)],
        out_specs=pl.BlockSpec((1, seq + n_ctx, LANES), lambda b, j: (b, 0, j)),
        compiler_params=_params("parallel", "parallel"), name="ssd_prep",
    )(s_lat, s_ctx, conv_w, conv_b.reshape(1, -1))


def _softplus(x):
    return jnp.maximum(x, 0.0) + jnp.log1p(jnp.exp(-jnp.abs(x)))


def _ssd_chunk(direction, xs_ref, bc_ref, dt_ref, dtt_ref, pr_ref, pc_ref, dexp_ref, s_ref):
    q = SSD_CHUNK
    nh = SSD_HEADS
    hi = lax.Precision.HIGHEST
    d0 = direction * nh
    dt = _softplus(dt_ref[0][:, d0:d0 + nh] + pr_ref[0:1, :])
    a = dt * (-jnp.exp(pr_ref[1:2, :]))
    dtt = _softplus(dtt_ref[0][d0:d0 + nh, :] + pc_ref[:, 0:1])
    at = dtt * (-jnp.exp(pc_ref[:, 1:2]))
    ii = lax.broadcasted_iota(jnp.int32, (q, q), 0)
    jj = lax.broadcasted_iota(jnp.int32, (q, q), 1)
    tri = (jj <= ii) if direction == 0 else (jj >= ii)
    tri_f = tri.astype(F32)
    cs = jnp.dot(tri_f, a, precision=hi, preferred_element_type=F32)
    cst = lax.dot_general(at, tri_f, (((1,), (1,)), ((), ())), precision=hi,
                          preferred_element_type=F32)
    edge = q - 1 if direction == 0 else 0
    tot_t = cst[:, edge:edge + 1]
    dst = jnp.exp(tot_t - cst) * dtt
    cst_dt = cst - jnp.log(dtt)
    etot = jnp.exp(cs[edge:edge + 1, :])

    lo = _lane_lt64((q, LANES))
    groups = []
    for g in range(SSD_GROUPS):
        b_g = bc_ref[0][:, g * SSD_STATE:(g + 1) * SSD_STATE]
        c_g = bc_ref[0][:, (SSD_GROUPS + g) * SSD_STATE:(SSD_GROUPS + g + 1) * SSD_STATE]
        groups.append((c_g, _nt_dot(c_g.astype(BF16), b_g.astype(BF16)), b_g.T))
    ys = []
    for pair in range(nh // 2):
        c_g, cb, bt_g = groups[(2 * pair) // (nh // SSD_GROUPS)]
        xs_p = xs_ref[0][:, pair * LANES:(pair + 1) * LANES]
        r0, r1 = 2 * pair, 2 * pair + 1
        xb = xs_p.astype(BF16)
        zero_b = jnp.zeros_like(xb)
        xb_bd = jnp.concatenate([jnp.where(lo, xb, zero_b), jnp.where(lo, zero_b, xb)], axis=0)
        s_p = s_ref[pair]
        s_b = s_p.astype(BF16)
        s_bd = jnp.concatenate([jnp.where(lo, s_b, zero_b), jnp.where(lo, zero_b, s_b)], axis=0)
        lhs, lhs_state, bts = [], [], []
        for r in (r0, r1):
            colb = jnp.broadcast_to(cs[:, r:r + 1], (q, q))
            rowb = jnp.broadcast_to(cst_dt[r:r + 1, :], (q, q))
            decay = jnp.exp(jnp.where(tri, colb - rowb, NEG_INF))
            lhs.append((cb * decay).astype(BF16))
            lhs_state.append((c_g * jnp.exp(colb)).astype(BF16))
            bts.append((bt_g * jnp.broadcast_to(dst[r:r + 1, :], (SSD_STATE, q))).astype(BF16))
        y = jnp.dot(jnp.concatenate(lhs + lhs_state, axis=1), jnp.concatenate([xb_bd, s_bd], axis=0),
                    preferred_element_type=F32)
        y = y + xs_p * dexp_ref[:, pair * LANES:(pair + 1) * LANES]
        dec_p = jnp.where(_lane_lt64((1, LANES)), jnp.broadcast_to(etot[:, r0:r0 + 1], (1, LANES)),
                          jnp.broadcast_to(etot[:, r1:r1 + 1], (1, LANES)))
        s_ref[pair] = s_p * dec_p + jnp.dot(jnp.concatenate(bts, axis=1), xb_bd, preferred_element_type=F32)
        ys.append(y)
    return jnp.concatenate(ys, axis=1)


def _ssd_fwd_body(xs_ref, bc_ref, dt_ref, dtt_ref, pr_ref, pc_ref, dexp_ref, y_ref, s_ref):
    @pl.when(pl.program_id(1) == 0)
    def _():
        s_ref[...] = jnp.zeros_like(s_ref)

    y_ref[0] = _ssd_chunk(0, xs_ref, bc_ref, dt_ref, dtt_ref, pr_ref, pc_ref, dexp_ref, s_ref)


def _ssd_bwd_body(xs_ref, bc_ref, dt_ref, dtt_ref, pr_ref, pc_ref, dexp_ref, yf_ref, z_ref, g_ref, o_ref, s_ref):
    @pl.when(pl.program_id(1) == 0)
    def _():
        s_ref[...] = jnp.zeros_like(s_ref)

    y = _ssd_chunk(1, xs_ref, bc_ref, dt_ref, dtt_ref, pr_ref, pc_ref, dexp_ref, s_ref) + yf_ref[0]
    z = z_ref[0]
    gated = y * (z * jax.nn.sigmoid(z))
    gw = SSD_INNER // SSD_GROUPS
    parts = []
    for g in range(SSD_GROUPS):
        gg = gated[:, g * gw:(g + 1) * gw]
        parts.append(gg * lax.rsqrt(jnp.mean(gg * gg, axis=-1, keepdims=True) + EPS))
    o_ref[0] = (jnp.concatenate(parts, axis=1) * g_ref[...]).astype(o_ref.dtype)


def _ssd_call(u, dtt, dt_bias, a_log, d_skip, norm_g, batch, seq, n_ctx):
    q = SSD_CHUNK
    n_lat, n_c = seq // q, n_ctx // q
    n_ch = n_lat + n_c
    nh = SSD_HEADS
    fwd_chunk = lambda s: jnp.where(s < n_c, n_lat + s, s - n_c)
    bwd_chunk = lambda s: n_ch - 1 - s
    xs_blk = SSD_INNER // SSD_INNER
    bc_blk = (2 * SSD_INNER) // (2 * SSD_GROUPS * SSD_STATE)
    dt_blk = (SSD_INNER + SSD_XBC) // LANES
    dexp = [jnp.repeat(d_skip[d], SSD_HEAD_DIM).reshape(1, SSD_INNER) for d in range(2)]
    prow = [jnp.stack([dt_bias[d], a_log[d]], axis=0) for d in range(2)]
    pcol = [jnp.stack([dt_bias[d], a_log[d]], axis=1) for d in range(2)]

    def specs(chunk):
        return [pl.BlockSpec((1, q, SSD_INNER), lambda b, s: (b, chunk(s), xs_blk)),
                pl.BlockSpec((1, q, 2 * SSD_GROUPS * SSD_STATE), lambda b, s: (b, chunk(s), bc_blk)),
                pl.BlockSpec((1, q, LANES), lambda b, s: (b, chunk(s), dt_blk)),
                pl.BlockSpec((1, 2 * nh, q), lambda b, s: (b, 0, chunk(s))),
                pl.BlockSpec((2, nh), lambda b, s: (0, 0)),
                pl.BlockSpec((nh, 2), lambda b, s: (0, 0)),
                pl.BlockSpec((1, SSD_INNER), lambda b, s: (0, 0))]

    state = pltpu.VMEM((nh // 2, SSD_STATE, LANES), F32)
    yblk = lambda chunk: pl.BlockSpec((1, q, SSD_INNER), lambda b, s: (b, chunk(s), 0))
    y_f = pl.pallas_call(
        _ssd_fwd_body,
        out_shape=jax.ShapeDtypeStruct((batch, seq + n_ctx, SSD_INNER), F32),
        grid=(batch, n_ch), in_specs=specs(fwd_chunk), out_specs=yblk(fwd_chunk),
        scratch_shapes=[state], compiler_params=_params("parallel", "arbitrary"), name="ssd_fwd",
    )(u, u, u, dtt, prow[0], pcol[0], dexp[0])
    return pl.pallas_call(
        _ssd_bwd_body,
        out_shape=jax.ShapeDtypeStruct((batch, seq + n_ctx, SSD_INNER), BF16),
        grid=(batch, n_ch),
        in_specs=specs(bwd_chunk) + [yblk(bwd_chunk), yblk(bwd_chunk),
                                     pl.BlockSpec((1, SSD_INNER), lambda b, s: (0, 0))],
        out_specs=yblk(bwd_chunk),
        scratch_shapes=[state], compiler_params=_params("parallel", "arbitrary"), name="ssd_bwd",
    )(u, u, u, dtt, prow[1], pcol[1], dexp[1], y_f, u, norm_g.reshape(1, SSD_INNER))


def _residual_norm(x, f, gate, g_post, pre):
    x = x + gate * _rms(f, g_post)
    if pre is None:
        return x, None
    g_pre, shift, scale = pre
    return x, (_rms(x, g_pre) * (1.0 + scale) + shift).astype(BF16)


def _wout_body(na_ref, da_ref, ssd_ref, w_ref, x_ref, gate_ref, gpost_ref, gpre_ref, shift_ref, scale_ref,
               xo_ref, h_ref):
    k1 = na_ref.shape[1]
    k2 = k1 + da_ref.shape[1]
    acc = jnp.dot(na_ref[...], w_ref[:k1, :], preferred_element_type=F32)
    acc = acc + jnp.dot(da_ref[...], w_ref[k1:k2, :], preferred_element_type=F32)
    acc = acc + jnp.dot(ssd_ref[0], w_ref[k2:, :], preferred_element_type=F32)
    xo_ref[...], h_ref[...] = _residual_norm(x_ref[...], acc, gate_ref[0], gpost_ref[...],
                                             (gpre_ref[...], shift_ref[0], scale_ref[0]))


def _wout_call(na, da, ssd, w, x, sel, res, pre, ssd_tile, tm):
    m = na.shape[0]
    d = w.shape[1]
    row = pl.BlockSpec((tm, d), lambda i: (i, 0))
    vec = pl.BlockSpec((1, d), lambda i: (0, 0))
    tab = pl.BlockSpec((1, 1, d), lambda i: (sel(i), 0, 0))
    return pl.pallas_call(
        _wout_body,
        out_shape=(jax.ShapeDtypeStruct((m, d), F32), jax.ShapeDtypeStruct((m, d), BF16)),
        grid=(m // tm,),
        in_specs=[pl.BlockSpec((tm, na.shape[1]), lambda i: (i, 0)),
                  pl.BlockSpec((tm, da.shape[1]), lambda i: (i, 0)),
                  pl.BlockSpec((1, tm, ssd.shape[2]), lambda i: ssd_tile(i) + (0,)),
                  pl.BlockSpec(w.shape, lambda i: (0, 0)),
                  row, tab, vec, vec, tab, tab],
        out_specs=(row, row),
        compiler_params=_params("parallel"), name="w_out",
    )(na, da, ssd, w, x, res[0], res[1], *pre)


def _batcher_pairs(n):
    pairs = []
    p = 1
    while p < n:
        k = p
        while k >= 1:
            for j in range(k % p, n - k, 2 * k):
                for i in range(min(k, n - j - k)):
                    if (i + j) // (2 * p) == (i + j + k) // (2 * p):
                        pairs.append((i + j, i + j + k))
            k //= 2
        p *= 2
    return pairs


def _top_vals_network(x, k):
    sub = 8
    tiles = [x[v * sub:(v + 1) * sub, :] for v in range(k)]

    def exchange(i, j):
        hi, lo = jnp.maximum(tiles[i], tiles[j]), jnp.minimum(tiles[i], tiles[j])
        tiles[i], tiles[j] = hi, lo

    for i, j in _batcher_pairs(k):
        exchange(i, j)
    shift = sub // 2
    while shift >= 1:
        other = [pltpu.roll(t, shift, 0) for t in tiles]
        tiles = [jnp.maximum(tiles[i], other[k - 1 - i]) for i in range(k)]
        d = k // 2
        while d >= 1:
            for i in range(k):
                if i & d == 0:
                    exchange(i, i + d)
            d //= 2
        shift //= 2
    return jnp.concatenate([t[0:1, :] for t in tiles], axis=0)


def _pair_candidates(v1, v2):
    k = PEER_TOPK
    half = k // 2
    row = lax.broadcasted_iota(jnp.int32, (half, v1.shape[1]), 0)
    parts = [v2 + v1[0:1, :]]
    for i in range(1, half):
        parts.append(jnp.where(row < k // (i + 1), v2[:half] + v1[i:i + 1, :], NEG_INF))
    parts.append(v1[half:] + v2[0:1, :])
    return jnp.concatenate(parts, axis=0)


def _peer_sel_body(h_ref, wq_ref, keys_ref, cnt_ref, e1_ref, rank_ref, e2_ref):
    qd = PEER_QDIM // 2
    k = PEER_TOPK
    qv = jnp.dot(h_ref[...], wq_ref[...], preferred_element_type=F32).astype(BF16)
    for hh in range(PEER_HEADS):
        c0 = hh * PEER_QDIM
        s1 = _nt_dot(keys_ref[2 * hh], qv[:, c0:c0 + qd])
        s2 = _nt_dot(keys_ref[2 * hh + 1], qv[:, c0 + qd:c0 + 2 * qd])
        v1 = _top_vals_network(s1, k)
        v2 = _top_vals_network(s2, k)
        rank = jnp.zeros_like(s2)
        for r in range(k):
            rank = jnp.where(v2[r:r + 1, :] > s2, float(r + 1), rank)
        cand = _pair_candidates(v1, v2)
        pad = jnp.full((8 * k - cand.shape[0], cand.shape[1]), NEG_INF, F32)
        top = _top_vals_network(jnp.concatenate([cand, pad], axis=0), k)
        thr = top[k - 1:k, :]
        z = jnp.sum(jnp.exp(top - top[0:1, :]), axis=0, keepdims=True)
        cnt_sorted = jnp.zeros_like(v1)
        for r in range(k):
            cnt_sorted = cnt_sorted + jnp.where(v1 + v2[r:r + 1, :] >= thr, 1.0, 0.0)
        cnt = jnp.zeros_like(s1)
        for r in range(k):
            cnt = jnp.where(s1 == v1[r:r + 1, :], cnt_sorted[r:r + 1, :], cnt)
        cnt_ref[hh] = cnt
        rank_ref[hh] = rank.astype(BF16)
        e1_ref[hh] = jnp.exp(s1 - v1[0:1, :])
        e2_ref[hh] = (jnp.exp(s2 - v2[0:1, :]) / z).astype(BF16)


def _peer_sel_call(h, wq, keys, tm=256):
    t, d = h.shape
    shape = (PEER_HEADS, PEER_N_KEYS, t)
    sblk = pl.BlockSpec((PEER_HEADS, PEER_N_KEYS, tm), lambda i: (0, 0, i))
    return pl.pallas_call(
        _peer_sel_body,
        out_shape=(jax.ShapeDtypeStruct(shape, F32), jax.ShapeDtypeStruct(shape, F32),
                   jax.ShapeDtypeStruct(shape, BF16), jax.ShapeDtypeStruct(shape, BF16)),
        grid=(t // tm,),
        in_specs=[pl.BlockSpec((tm, d), lambda i: (i, 0)),
                  pl.BlockSpec(wq.shape, lambda i: (0, 0)),
                  pl.BlockSpec(keys.shape, lambda i: (0, 0, 0))],
        out_specs=(sblk, sblk, sblk, sblk),
        compiler_params=_params("parallel"), name="peer_select",
    )(h, wq, keys)


def _gelu_tanh(x):
    return 0.5 * x * (1.0 + jnp.tanh(math.sqrt(2.0 / math.pi) * (x + 0.044715 * (x * x * x))))


def _peer_ffn_body(n_a, n_split, has_pre, h_ref, u_ref, vt_ref, cnt_ref, e1_ref, rank_ref, e2_ref,
                   x_ref, gate_ref, gpost_ref, *rest):
    if has_pre:
        gpre_ref, shift_ref, scale_ref, xo_ref, ho_ref, acc_ref, wt_ref = rest
    else:
        xo_ref, acc_ref, wt_ref = rest
    j = pl.program_id(1)

    @pl.when(j == 0)
    def _():
        acc_ref[...] = jnp.zeros_like(acc_ref)

    nk = PEER_N_KEYS
    a_part = n_a // n_split
    rows = a_part * nk
    h = h_ref[...]

    def act(s):
        return _gelu_tanh(_nt_dot(u_ref[s * rows:(s + 1) * rows, :], h)).astype(BF16)

    act_next = act(0)
    for s in range(n_split):
        act_s = act_next
        if s + 1 < n_split:
            act_next = act(s + 1)
        for al in range(a_part):
            a = s * a_part + al
            gate = None
            for hh in range(PEER_HEADS):
                cnt = cnt_ref[hh, a:a + 1, :].astype(BF16)
                e1 = e1_ref[hh, a:a + 1, :].astype(BF16)
                w = jnp.where(rank_ref[hh] < cnt, e2_ref[hh], jnp.zeros((), BF16)) * e1
                gate = w if gate is None else gate + w
            wt_ref[a * nk:(a + 1) * nk, :] = gate * act_s[al * nk:(al + 1) * nk, :]
        acc_ref[...] += jnp.dot(vt_ref[:, s * rows:(s + 1) * rows], wt_ref[s * rows:(s + 1) * rows, :],
                                preferred_element_type=F32)

    @pl.when(j == pl.num_programs(1) - 1)
    def _():
        pre = (gpre_ref[...], shift_ref[0], scale_ref[0]) if has_pre else None
        x_new, h_next = _residual_norm(x_ref[...], acc_ref[...].T, gate_ref[0], gpost_ref[...], pre)
        xo_ref[...] = x_new
        if has_pre:
            ho_ref[...] = h_next


def _peer_ffn_call(h, u, vt, layer, cnt, e1, rank, e2, x, sel, res, pre, tm=TOKEN_TILE, n_a=8, n_split=2):
    t, d = h.shape
    e = u.shape[1]
    te = n_a * PEER_N_KEYS
    a_blk = pl.BlockSpec((PEER_HEADS, n_a, tm), lambda i, j: (0, j, i))
    b_blk = pl.BlockSpec((PEER_HEADS, PEER_N_KEYS, tm), lambda i, j: (0, 0, i))
    row = pl.BlockSpec((tm, d), lambda i, j: (i, 0))
    vec = pl.BlockSpec((1, d), lambda i, j: (0, 0))
    tab = pl.BlockSpec((1, 1, d), lambda i, j: (sel(i), 0, 0))
    args = [h, u, vt, cnt, e1, rank, e2, x, res[0], res[1]]
    specs = [row, pl.BlockSpec((None, te, d), lambda i, j: (layer, j, 0)),
             pl.BlockSpec((None, d, te), lambda i, j: (layer, 0, j)),
             a_blk, a_blk, b_blk, b_blk, row, tab, vec]
    outs = [jax.ShapeDtypeStruct((t, d), F32)]
    ospecs = [row]
    if pre is not None:
        args += list(pre)
        specs += [vec, tab, tab]
        outs.append(jax.ShapeDtypeStruct((t, d), BF16))
        ospecs.append(row)
    out = pl.pallas_call(
        functools.partial(_peer_ffn_body, n_a, n_split, pre is not None),
        out_shape=tuple(outs), grid=(t // tm, e // te), in_specs=specs, out_specs=tuple(ospecs),
        scratch_shapes=[pltpu.VMEM((d, tm), F32), pltpu.VMEM((te, tm), BF16)],
        compiler_params=_params("parallel", "arbitrary"), name="peer_ffn",
    )(*args)
    return (out[0], out[1]) if pre is not None else (out[0], None)


def _peer(h, wq, keys, u, vt, layer, x, sel, res, pre):
    cnt, e1, rank, e2 = _peer_sel_call(h, wq, keys)
    return _peer_ffn_call(h, u, vt, layer, cnt, e1, rank, e2, x, sel, res, pre)


def kernel(x, c, ctx, c_ctx, ada_w, ada_b, norm_g, w_in, w_out, na_rpb, ssd_conv_w, ssd_conv_b, ssd_dt_bias,
           ssd_a_log, ssd_d, ssd_norm_g, da_lambda, da_sub_g, peer_w_q, peer_sub_keys, peer_u, peer_v):
    batch, seq, d = x.shape
    n_ctx = ctx.shape[1]
    depth = ada_w.shape[0]
    xl = x.reshape(batch * seq, d)
    xc = ctx.reshape(batch * n_ctx, d)
    tm_norm = 256
    sel_lat = lambda i: i // (seq // tm_norm)
    sel_ctx = lambda i: batch
    cos_t, sin_t = _rope_tables(seq)

    o_nq, o_nk, o_nv = 0, 512, 1024
    o_dq, o_dk, o_dv = 1536, 2048, 2560
    o_z = 3072
    o_end = o_z + SSD_INNER + SSD_XBC + 2 * SSD_HEADS
    scale_a = jnp.concatenate([jnp.full((512,), NA_DIM ** -0.5, F32), jnp.ones((1536,), F32)])
    scale_b = jnp.concatenate([jnp.full((512,), DA_DIM ** -0.5, F32), jnp.ones((512,), F32)])

    cond = jnp.concatenate([c, c_ctx[None, :], jnp.zeros((8 - batch - 1, d), F32)], axis=0)
    mods = [_ada_call(cond, ada_w, ada_b, layer)[:batch + 1].reshape(batch + 1, N_MOD, 1, d)
            for layer in range(depth)]
    pu = peer_u.astype(BF16)
    pvt = jnp.swapaxes(peer_v, 1, 2).astype(BF16)
    h_l = h_c = None
    for layer in range(depth):
        need_ctx = layer < depth - 1
        lam_init = 0.8 - 0.6 * math.exp(-0.3 * layer)
        tab = [mods[layer][:, k] for k in range(N_MOD)]
        g = norm_g[layer]
        if layer == 0:
            _, h_l = _norm_call(xl, sel_lat, pre=(g[0:1], tab[0], tab[1]), tm=tm_norm)
            _, h_c = _norm_call(xc, sel_ctx, pre=(g[0:1], tab[0], tab[1]), tm=tm_norm)

        wi = w_in[layer]
        w_a = jnp.concatenate([wi[:, o_nq:o_dq], wi[:, o_dv:o_z]], axis=1).astype(BF16)
        w_b = wi[:, o_dq:o_dv].astype(BF16)
        w_c = jnp.pad(wi[:, o_z:o_end], ((0, 0), (0, SSD_IN_W - (o_end - o_z)))).astype(BF16)
        a_lat = _mm_call(h_l, w_a, BF16, colscale=scale_a, name="proj_attn")
        a_ctx = _mm_call(h_c, w_a, BF16, colscale=scale_a, name="proj_attn")
        qk_lat = _mm_call(h_l, w_b, BF16, colscale=scale_b, rope=(cos_t, sin_t), name="proj_rope")
        qk_ctx = _mm_call(h_c, w_b, BF16, colscale=scale_b, name="proj_qk_ctx")
        s_lat = _mm_call(h_l, w_c, F32, name="proj_ssd")
        s_ctx = _mm_call(h_c, w_c, F32, name="proj_ssd")

        lam_p = da_lambda[layer]
        g_sub = (da_sub_g[layer] * (1.0 - lam_init)).reshape(1, LANES)
        na_l = _na_call(a_lat, a_ctx, _na_bias(na_rpb[layer]), batch, seq, n_ctx)
        da_l = _da_call(qk_lat, qk_ctx, a_lat, a_ctx, lam_p, g_sub, lam_init, batch, seq, n_ctx)

        u_ssd = _ssd_prep_call(s_lat, s_ctx, ssd_conv_w[layer], ssd_conv_b[layer], batch, seq, n_ctx)
        dtt = jnp.transpose(u_ssd[:, :, SSD_INNER + SSD_XBC:SSD_INNER + SSD_XBC + 2 * SSD_HEADS], (0, 2, 1))
        ssd_o = _ssd_call(u_ssd, dtt, ssd_dt_bias[layer], ssd_a_log[layer], ssd_d[layer], ssd_norm_g[layer],
                          batch, seq, n_ctx)

        wo = w_out[layer].astype(BF16)
        wq = peer_w_q[layer].astype(BF16)
        keys = peer_sub_keys[layer].reshape(2 * PEER_HEADS, PEER_N_KEYS, PEER_QDIM // 2).astype(BF16)
        nxt = layer + 1 < depth

        lat_tiles = seq // TOKEN_TILE
        sel_lat_t = lambda i: i // lat_tiles
        res_mix, pre_ffn, res_ffn = (tab[2], g[1:2]), (g[2:3], tab[3], tab[4]), (tab[5], g[3:4])
        pre_n = (norm_g[layer + 1][0:1], mods[layer + 1][:, 0], mods[layer + 1][:, 1]) if nxt else None
        xl, h2_l = _wout_call(na_l, da_l, ssd_o, wo, xl, sel_lat_t, res_mix, pre_ffn,
                              lambda i: (i // lat_tiles, i % lat_tiles), TOKEN_TILE)
        xl, h_l = _peer(h2_l, wq, keys, pu, pvt, layer, xl, sel_lat_t, res_ffn, pre_n)

        if need_ctx:
            na_c, da_c = _ctx_attn_call(a_ctx, qk_ctx, lam_p, g_sub, lam_init, batch, n_ctx)
            xc, h2_c = _wout_call(na_c, da_c, ssd_o, wo, xc, sel_ctx, res_mix, pre_ffn,
                                  lambda i: (i, seq // n_ctx), n_ctx)
            xc, h_c = _peer(h2_c, wq, keys, pu, pvt, layer, xc, sel_ctx, res_ffn, pre_n)
    return xl.reshape(batch, seq, d)
```

```python
import functools
import math

import numpy as np
import jax
import jax.numpy as jnp
from jax import lax
from jax.experimental import pallas as pl
from jax.experimental.pallas import tpu as pltpu

F32 = jnp.float32
BF16 = jnp.bfloat16
EPS = 1e-6
NEG_INF = float("-inf")

VMEM_LIMIT_BYTES = 58 * 1024 * 1024
LANES = 128

GRID_W = 64
NA_DIM = 64
NA_HEADS = 8
NA_WIN_ROWS = 8
NA_WIN_COLS = 16
DA_DIM = 64
DA_HEADS = 4
ROPE_BASE = 10000.0
SSD_INNER = 1024
SSD_HEAD_DIM = 64
SSD_HEADS = 16
SSD_GROUPS = 2
SSD_STATE = 128
SSD_CONV = 5
SSD_CHUNK = 128
SSD_XBC = SSD_INNER + 2 * SSD_GROUPS * SSD_STATE
SSD_IN_W = SSD_INNER + SSD_XBC + LANES
PEER_HEADS = 8
PEER_QDIM = 256
PEER_N_KEYS = 128
PEER_TOPK = 16
N_MOD = 6
TOKEN_TILE = 512


def _params(*sem):
    return pltpu.CompilerParams(dimension_semantics=sem, vmem_limit_bytes=VMEM_LIMIT_BYTES)


def _nt_dot(a, b):
    return lax.dot_general(a, b, (((1,), (1,)), ((), ())), preferred_element_type=F32)


def _lane_lt64(shape):
    return lax.broadcasted_iota(jnp.int32, shape, len(shape) - 1) < (LANES // 2)


def _ada_body(c_ref, w_ref, b_ref, o_ref):
    c = c_ref[...]
    sc = (c * jax.nn.sigmoid(c)).astype(BF16)
    o_ref[...] = jnp.dot(sc, w_ref[...].astype(BF16), preferred_element_type=F32) + b_ref[...]


def _ada_call(cond, w, b, layer, tn=768):
    m, d = cond.shape
    depth, _, n = w.shape
    return pl.pallas_call(
        _ada_body,
        out_shape=jax.ShapeDtypeStruct((m, n), F32),
        grid=(n // tn,),
        in_specs=[pl.BlockSpec((m, d), lambda j: (0, 0)),
                  pl.BlockSpec((None, d, tn), lambda j: (layer, 0, j)),
                  pl.BlockSpec((None, 1, tn), lambda j: (layer, 0, j))],
        out_specs=pl.BlockSpec((m, tn), lambda j: (0, j)),
        compiler_params=_params("arbitrary"),
        name="ada_mod",
    )(cond, w, b.reshape(depth, 1, n))


def _rms(x, g):
    return x * lax.rsqrt(jnp.mean(x * x, axis=-1, keepdims=True) + EPS) * g


def _norm_body(has_res, has_pre, *refs):
    refs = list(refs)
    x = refs.pop(0)[...]
    if has_res:
        f_ref, gate_ref, gpost_ref = refs.pop(0), refs.pop(0), refs.pop(0)
        x = x + gate_ref[0] * _rms(f_ref[...], gpost_ref[...])
    if has_pre:
        gpre_ref, shift_ref, scale_ref = refs.pop(0), refs.pop(0), refs.pop(0)
    if has_res:
        refs.pop(0)[...] = x
    if has_pre:
        h = _rms(x, gpre_ref[...]) * (1.0 + scale_ref[0]) + shift_ref[0]
        refs.pop(0)[...] = h.astype(BF16)


def _norm_call(x, sel, *, res=None, pre=None, tm=256):
    m, d = x.shape
    row = pl.BlockSpec((tm, d), lambda i: (i, 0))
    vec = pl.BlockSpec((1, d), lambda i: (0, 0))
    tab = pl.BlockSpec((1, 1, d), lambda i: (sel(i), 0, 0))
    args, specs, outs, ospecs = [x], [row], [], []
    if res is not None:
        args += list(res)
        specs += [row, tab, vec]
        outs.append(jax.ShapeDtypeStruct((m, d), F32))
        ospecs.append(row)
    if pre is not None:
        args += list(pre)
        specs += [vec, tab, tab]
        outs.append(jax.ShapeDtypeStruct((m, d), BF16))
        ospecs.append(row)
    res_out = pl.pallas_call(
        functools.partial(_norm_body, res is not None, pre is not None),
        out_shape=tuple(outs), grid=(m // tm,), in_specs=specs, out_specs=tuple(ospecs),
        compiler_params=_params("parallel"), name="norm_mod",
    )(*args)
    res_out = list(res_out)
    x_new = res_out.pop(0) if res is not None else None
    h = res_out.pop(0) if pre is not None else None
    return x_new, h


def _rope_partner(x):
    lane = lax.broadcasted_iota(jnp.int32, x.shape, 1)
    first = (lane % 32) < 16
    return jnp.where(first, pltpu.roll(x, LANES - 16, 1), pltpu.roll(x, 16, 1))


def _mm_body(epi, h_ref, w_ref, *rest):
    acc = jnp.dot(h_ref[...], w_ref[...], preferred_element_type=F32)
    if epi == "plain":
        (o_ref,) = rest
    elif epi == "scale":
        cs_ref, o_ref = rest
        acc = acc * cs_ref[...]
    else:
        cos_ref, sin_ref, cs_ref, o_ref = rest
        cos = cos_ref[...]
        sin = sin_ref[...]
        n = acc.shape[1]
        blocks = []
        for c in range(n // LANES):
            xb = acc[:, c * LANES:(c + 1) * LANES]
            blocks.append(xb * cos + _rope_partner(xb) * sin)
        acc = jnp.concatenate(blocks, axis=1) * cs_ref[...]
    o_ref[...] = acc.astype(o_ref.dtype)


def _mm_call(h, w, out_dtype, *, colscale=None, rope=None, tm=512, tn=None, name="proj"):
    m, k = h.shape
    n = w.shape[1]
    tn = n if tn is None else tn
    args = [h, w]
    specs = [pl.BlockSpec((tm, k), lambda i, j: (i, 0)),
             pl.BlockSpec((k, tn), lambda i, j: (0, j))]
    epi = "plain"
    if rope is not None:
        cos_t, sin_t = rope
        nt = cos_t.shape[0] // tm
        args += [cos_t, sin_t]
        specs += [pl.BlockSpec((tm, LANES), lambda i, j: (i % nt, 0))] * 2
        epi = "rope"
    if colscale is not None:
        args.append(colscale.reshape(1, n))
        specs.append(pl.BlockSpec((1, tn), lambda i, j: (0, j)))
        epi = "scale" if epi == "plain" else epi
    return pl.pallas_call(
        functools.partial(_mm_body, epi),
        out_shape=jax.ShapeDtypeStruct((m, n), out_dtype),
        grid=(m // tm, n // tn), in_specs=specs,
        out_specs=pl.BlockSpec((tm, tn), lambda i, j: (i, j)),
        compiler_params=_params("parallel", "arbitrary"), name=name,
    )(*args)


def _rope_tables(seq):
    quarter = DA_DIM // 4
    inv_freq = 1.0 / (ROPE_BASE ** (jnp.arange(quarter, dtype=F32) / quarter))
    t = jnp.arange(seq)
    ang_r = (t // GRID_W).astype(F32)[:, None] * inv_freq
    ang_c = (t % GRID_W).astype(F32)[:, None] * inv_freq

    def unit(ang):
        return (jnp.concatenate([jnp.cos(ang), jnp.cos(ang)], -1),
                jnp.concatenate([-jnp.sin(ang), jnp.sin(ang)], -1))

    cr, sr = unit(ang_r)
    cc, sc = unit(ang_c)
    cos64 = jnp.concatenate([cr, cc], -1)
    sin64 = jnp.concatenate([sr, sc], -1)
    return jnp.tile(cos64, (1, 2)), jnp.tile(sin64, (1, 2))


def _stack_heads(q2):
    lo = _lane_lt64(q2.shape)
    zero = jnp.zeros_like(q2)
    return jnp.concatenate([jnp.where(lo, q2, zero), jnp.where(lo, zero, q2)], axis=0)


def _unstack_heads(o):
    r = o.shape[0] // 2
    return jnp.where(_lane_lt64((r, o.shape[1])), o[:r], o[r:])


NA_ROW_GROUP = 8


def _na_body(rows, q_ref, k_ref, v_ref, kc_ref, vc_ref, bias_ref, o_ref):
    kc = kc_ref[...]
    vc = vc_ref[...]
    kh = NA_WIN_ROWS

    def scores(r):
        rs = jnp.clip(r - kh // 2, 0, rows - kh)
        q0 = pl.multiple_of(r * GRID_W, GRID_W)
        k0 = pl.multiple_of(rs * GRID_W, GRID_W)
        qs = _stack_heads(q_ref[pl.ds(q0, GRID_W), :])
        s_loc = _nt_dot(qs, k_ref[pl.ds(k0, kh * GRID_W), :]) + bias_ref[0, r - rs]
        return q0, k0, s_loc, _nt_dot(qs, kc)

    def finish(q0, k0, p_loc, p_ctx, l):
        o = (jnp.dot(p_loc, v_ref[pl.ds(k0, kh * GRID_W), :], preferred_element_type=F32)
             + jnp.dot(p_ctx, vc, preferred_element_type=F32)) / l
        o_ref[pl.ds(q0, GRID_W), :] = _unstack_heads(o).astype(o_ref.dtype)

    def row_group(gi, carry):
        nxt = scores(gi * NA_ROW_GROUP)
        pending = None
        for j in range(NA_ROW_GROUP):
            q0, k0, s_loc, s_ctx = nxt
            if j + 1 < NA_ROW_GROUP:
                nxt = scores(gi * NA_ROW_GROUP + j + 1)
            m = jnp.maximum(jnp.max(s_loc, axis=-1, keepdims=True), jnp.max(s_ctx, axis=-1, keepdims=True))
            p_loc = jnp.exp(s_loc - m)
            p_ctx = jnp.exp(s_ctx - m)
            l = jnp.sum(p_loc, axis=-1, keepdims=True) + jnp.sum(p_ctx, axis=-1, keepdims=True)
            if pending is not None:
                finish(*pending)
            pending = (q0, k0, p_loc.astype(BF16), p_ctx.astype(BF16), l)
        finish(*pending)
        return carry

    lax.fori_loop(0, rows // NA_ROW_GROUP, row_group, 0)


def _na_bias(rpb):
    kh, w = NA_WIN_ROWS, GRID_W
    c = np.arange(w)
    wstart = np.clip(c - NA_WIN_COLS // 2, 0, w - NA_WIN_COLS)
    kc = np.arange(w)
    valid = (kc[None, :] >= wstart[:, None]) & (kc[None, :] < wstart[:, None] + NA_WIN_COLS)
    cidx = np.clip(kc[None, :] - c[:, None] + NA_WIN_COLS - 1, 0, 2 * NA_WIN_COLS - 2)
    typ = np.arange(kh)
    ridx = np.arange(kh)[None, :] - typ[:, None] + NA_WIN_ROWS - 1
    pick_r = (ridx[:, :, None] == np.arange(2 * NA_WIN_ROWS - 1)).astype(np.float32)
    pick_c = ((cidx[:, :, None] == np.arange(2 * NA_WIN_COLS - 1)) & valid[:, :, None]).astype(np.float32)
    b = jnp.einsum('hrc,tjr,qkc->htqjk', rpb.astype(F32), pick_r, pick_c, precision=lax.Precision.HIGHEST)
    b = b + np.where(valid, 0.0, NEG_INF).astype(np.float32)[None, None, :, None, :]
    b = b.reshape(NA_HEADS // 2, 2, kh, w, kh * w)
    return jnp.transpose(b, (0, 2, 1, 3, 4)).reshape(NA_HEADS // 2, kh, 2 * w, kh * w).astype(F32)


def _na_call(a_lat, a_ctx, bias, batch, seq, n_ctx):
    rows = seq // GRID_W
    pairs = NA_HEADS // 2
    lat = lambda c0: pl.BlockSpec((seq, LANES), lambda b, p: (b, c0 + p))
    ctx = lambda c0: pl.BlockSpec((n_ctx, LANES), lambda b, p: (b, c0 + p))
    return pl.pallas_call(
        functools.partial(_na_body, rows),
        out_shape=jax.ShapeDtypeStruct((batch * seq, NA_HEADS * NA_DIM), BF16),
        grid=(batch, pairs),
        in_specs=[lat(0), lat(pairs), lat(2 * pairs), ctx(pairs), ctx(2 * pairs),
                  pl.BlockSpec((1,) + bias.shape[1:], lambda b, p: (p, 0, 0, 0))],
        out_specs=pl.BlockSpec((seq, LANES), lambda b, p: (b, p)),
        compiler_params=_params("parallel", "parallel"), name="na_attn",
    )(a_lat, a_lat, a_lat, a_ctx, a_ctx, bias)


def _da_lambda(lam_ref, lam_init):
    lf = lam_ref[...]
    a = jnp.sum(lf[0:1] * lf[1:2], axis=-1, keepdims=True)
    b = jnp.sum(lf[2:3] * lf[3:4], axis=-1, keepdims=True)
    return jnp.exp(a) - jnp.exp(b) + lam_init


def _da_finish(acc, l, lam, g):
    tq = acc.shape[0] // 2
    o = acc / l
    d = o[:tq] - lam * o[tq:]
    return d * lax.rsqrt(jnp.mean(d * d, axis=-1, keepdims=True) + EPS) * g


def _flash_step(qs, k, v, m, l, acc):
    s = _nt_dot(qs, k)
    m_new = jnp.maximum(m, jnp.max(s, axis=-1, keepdims=True))
    alpha = jnp.exp(m - m_new)
    p = jnp.exp(s - m_new)
    l = alpha * l + jnp.sum(p, axis=-1, keepdims=True)
    acc = alpha * acc + jnp.dot(p.astype(BF16), v, preferred_element_type=F32)
    return m_new, l, acc


def _da_body(lam_init, n_chunks, tk, q_ref, k_ref, v_ref, kc_ref, vc_ref, lam_ref, g_ref, o_ref):
    qs = _stack_heads(q_ref[...])
    r = qs.shape[0]
    chunks = [(k_ref, v_ref, c * tk, tk) for c in range(n_chunks)] + [(kc_ref, vc_ref, 0, kc_ref.shape[0])]

    def scores(i):
        kr, _, k0, n = chunks[i]
        return _nt_dot(qs, kr[k0:k0 + n, :])

    m = jnp.full((r, 1), NEG_INF, F32)
    l = jnp.zeros((r, 1), F32)
    acc = jnp.zeros((r, LANES), F32)
    s_next = scores(0)
    pending = None
    for i in range(len(chunks)):
        s = s_next
        if i + 1 < len(chunks):
            s_next = scores(i + 1)
        m_new = jnp.maximum(m, jnp.max(s, axis=-1, keepdims=True))
        alpha = jnp.exp(m - m_new)
        p = jnp.exp(s - m_new)
        l = alpha * l + jnp.sum(p, axis=-1, keepdims=True)
        m = m_new
        if pending is not None:
            p_prev, alpha_prev, (_, vr, k0, n) = pending
            acc = alpha_prev * acc + jnp.dot(p_prev, vr[k0:k0 + n, :], preferred_element_type=F32)
        pending = (p.astype(BF16), alpha, chunks[i])
    p_prev, alpha_prev, (_, vr, k0, n) = pending
    acc = alpha_prev * acc + jnp.dot(p_prev, vr[k0:k0 + n, :], preferred_element_type=F32)
    o_ref[...] = _da_finish(acc, l, _da_lambda(lam_ref, lam_init), g_ref[...]).astype(o_ref.dtype)


def _da_call(qk_lat, qk_ctx, a_lat, a_ctx, lam_p, g_scaled, lam_init, batch, seq, n_ctx, tq=256, tk=512):
    nq = seq // tq
    h = DA_HEADS
    return pl.pallas_call(
        functools.partial(_da_body, lam_init, seq // tk, tk),
        out_shape=jax.ShapeDtypeStruct((batch * seq, h * 2 * DA_DIM), BF16),
        grid=(batch, h, nq),
        in_specs=[pl.BlockSpec((tq, LANES), lambda b, hh, i: (b * nq + i, hh)),
                  pl.BlockSpec((seq, LANES), lambda b, hh, i: (b, h + hh)),
                  pl.BlockSpec((seq, LANES), lambda b, hh, i: (b, 3 * h + hh)),
                  pl.BlockSpec((n_ctx, LANES), lambda b, hh, i: (b, h + hh)),
                  pl.BlockSpec((n_ctx, LANES), lambda b, hh, i: (b, 3 * h + hh)),
                  pl.BlockSpec(lam_p.shape, lambda b, hh, i: (0, 0)),
                  pl.BlockSpec((1, LANES), lambda b, hh, i: (0, 0))],
        out_specs=pl.BlockSpec((tq, LANES), lambda b, hh, i: (b * nq + i, hh)),
        compiler_params=_params("parallel", "parallel", "arbitrary"), name="da_attn",
    )(qk_lat, qk_lat, a_lat, qk_ctx, a_ctx, lam_p, g_scaled)


def _ctx_attn_body(lam_init, a_ref, qk_ref, lam_ref, g_ref, na_ref, da_ref):
    pairs = NA_HEADS // 2
    for p in range(pairs):
        qs = _stack_heads(a_ref[:, p * LANES:(p + 1) * LANES])
        k = a_ref[:, (pairs + p) * LANES:(pairs + p + 1) * LANES]
        v = a_ref[:, (2 * pairs + p) * LANES:(2 * pairs + p + 1) * LANES]
        s = _nt_dot(qs, k)
        e = jnp.exp(s - jnp.max(s, axis=-1, keepdims=True))
        o = jnp.dot(e.astype(BF16), v, preferred_element_type=F32) / jnp.sum(e, axis=-1, keepdims=True)
        na_ref[:, p * LANES:(p + 1) * LANES] = _unstack_heads(o).astype(na_ref.dtype)
    lam = _da_lambda(lam_ref, lam_init)
    for hh in range(DA_HEADS):
        qs = _stack_heads(qk_ref[:, hh * LANES:(hh + 1) * LANES])
        k = qk_ref[:, (DA_HEADS + hh) * LANES:(DA_HEADS + hh + 1) * LANES]
        v = a_ref[:, (3 * pairs + hh) * LANES:(3 * pairs + hh + 1) * LANES]
        r = qs.shape[0]
        init = (jnp.full((r, 1), NEG_INF, F32), jnp.zeros((r, 1), F32), jnp.zeros((r, LANES), F32))
        m, l, acc = _flash_step(qs, k, v, *init)
        da_ref[:, hh * LANES:(hh + 1) * LANES] = _da_finish(acc, l, lam, g_ref[...]).astype(da_ref.dtype)


def _ctx_attn_call(a_ctx, qk_ctx, lam_p, g_scaled, lam_init, batch, n_ctx):
    return pl.pallas_call(
        functools.partial(_ctx_attn_body, lam_init),
        out_shape=(jax.ShapeDtypeStruct((batch * n_ctx, NA_HEADS * NA_DIM), BF16),
                   jax.ShapeDtypeStruct((batch * n_ctx, DA_HEADS * 2 * DA_DIM), BF16)),
        grid=(batch,),
        in_specs=[pl.BlockSpec((n_ctx, a_ctx.shape[1]), lambda b: (b, 0)),
                  pl.BlockSpec((n_ctx, qk_ctx.shape[1]), lambda b: (b, 0)),
                  pl.BlockSpec(lam_p.shape, lambda b: (0, 0)),
                  pl.BlockSpec((1, LANES), lambda b: (0, 0))],
        out_specs=(pl.BlockSpec((n_ctx, NA_HEADS * NA_DIM), lambda b: (b, 0)),
                   pl.BlockSpec((n_ctx, DA_HEADS * 2 * DA_DIM), lambda b: (b, 0))),
        compiler_params=_params("parallel"), name="ctx_attn",
    )(a_ctx, qk_ctx, lam_p, g_scaled)


def _conv_seg(u, w, bias):
    n = u.shape[0]
    t = lax.broadcasted_iota(jnp.int32, u.shape, 0)
    acc = jnp.zeros_like(u) + bias
    for kk in range(SSD_CONV):
        d = kk - SSD_CONV // 2
        if d == 0:
            sh = u
        else:
            sh = pltpu.roll(u, (-d) % n, 0)
            sh = jnp.where((t + d >= 0) & (t + d < n), sh, 0.0)
        acc = acc + sh * w[kk:kk + 1, :]
    return acc * jax.nn.sigmoid(acc)


def _ssd_prep_body(seq, lat_ref, ctx_ref, w_ref, b_ref, o_ref):
    w = w_ref[...]
    bias = b_ref[...]
    o_ref[0, :seq, :] = _conv_seg(lat_ref[...], w, bias)
    o_ref[0, seq:, :] = _conv_seg(ctx_ref[...], w, bias)


def _ssd_prep_call(s_lat, s_ctx, conv_w, conv_b, batch, seq, n_ctx):
    n_zc = SSD_INNER // LANES
    return pl.pallas_call(
        functools.partial(_ssd_prep_body, seq),
        out_shape=jax.ShapeDtypeStruct((batch, seq + n_ctx, SSD_XBC), F32),
        grid=(batch, SSD_XBC // LANES),
        in_specs=[pl.BlockSpec((seq, LANES), lambda b, j: (b, n_zc + j)),
                  pl.BlockSpec((n_ctx, LANES), lambda b, j: (b, n_zc + j)),
                  pl.BlockSpec((SSD_CONV, LANES), lambda b, j: (0, j)),
                  pl.BlockSpec((1, LANES), lambda b, j: (0, j))],
        out_specs=pl.BlockSpec((1, seq + n_ctx, LANES), lambda b, j: (b, 0, j)),
        compiler_params=_params("parallel", "parallel"), name="ssd_prep",
    )(s_lat, s_ctx, conv_w, conv_b.reshape(1, -1))


def _softplus(x):
    return jnp.maximum(x, 0.0) + jnp.log1p(jnp.exp(-jnp.abs(x)))


def _ssd_chunk(direction, xs_ref, bc_ref, dt_raw, dtt_ref, pr_ref, pc_ref, dexp_ref, s_ref):
    q = SSD_CHUNK
    nh = SSD_HEADS
    hi = lax.Precision.HIGHEST
    d0 = direction * nh
    dt = _softplus(dt_raw[:, d0:d0 + nh] + pr_ref[0:1, :])
    a = dt * (-jnp.exp(pr_ref[1:2, :]))
    dtt = _softplus(dtt_ref[0][d0:d0 + nh, :] + pc_ref[:, 0:1])
    at = dtt * (-jnp.exp(pc_ref[:, 1:2]))
    ii = lax.broadcasted_iota(jnp.int32, (q, q), 0)
    jj = lax.broadcasted_iota(jnp.int32, (q, q), 1)
    tri = (jj <= ii) if direction == 0 else (jj >= ii)
    tri_f = tri.astype(F32)
    cs = jnp.dot(tri_f, a, precision=hi, preferred_element_type=F32)
    cst = lax.dot_general(at, tri_f, (((1,), (1,)), ((), ())), precision=hi,
                          preferred_element_type=F32)
    edge = q - 1 if direction == 0 else 0
    tot_t = cst[:, edge:edge + 1]
    dst = jnp.exp(tot_t - cst) * dtt
    cst_dt = cst - jnp.log(dtt)
    etot = jnp.exp(cs[edge:edge + 1, :])

    lo = _lane_lt64((q, LANES))
    groups = []
    for g in range(SSD_GROUPS):
        b_g = bc_ref[0][:, g * SSD_STATE:(g + 1) * SSD_STATE]
        c_g = bc_ref[0][:, (SSD_GROUPS + g) * SSD_STATE:(SSD_GROUPS + g + 1) * SSD_STATE]
        groups.append((c_g, _nt_dot(c_g.astype(BF16), b_g.astype(BF16)), b_g.T))
    ys = []
    for pair in range(nh // 2):
        c_g, cb, bt_g = groups[(2 * pair) // (nh // SSD_GROUPS)]
        xs_p = xs_ref[0][:, pair * LANES:(pair + 1) * LANES]
        r0, r1 = 2 * pair, 2 * pair + 1
        xb = xs_p.astype(BF16)
        zero_b = jnp.zeros_like(xb)
        xb_bd = jnp.concatenate([jnp.where(lo, xb, zero_b), jnp.where(lo, zero_b, xb)], axis=0)
        s_p = s_ref[pair]
        s_b = s_p.astype(BF16)
        s_bd = jnp.concatenate([jnp.where(lo, s_b, zero_b), jnp.where(lo, zero_b, s_b)], axis=0)
        lhs, lhs_state, bts = [], [], []
        for r in (r0, r1):
            colb = jnp.broadcast_to(cs[:, r:r + 1], (q, q))
            rowb = jnp.broadcast_to(cst_dt[r:r + 1, :], (q, q))
            decay = jnp.exp(jnp.where(tri, colb - rowb, NEG_INF))
            lhs.append((cb * decay).astype(BF16))
            lhs_state.append((c_g * jnp.exp(colb)).astype(BF16))
            bts.append((bt_g * jnp.broadcast_to(dst[r:r + 1, :], (SSD_STATE, q))).astype(BF16))
        y = jnp.dot(jnp.concatenate(lhs + lhs_state, axis=1), jnp.concatenate([xb_bd, s_bd], axis=0),
                    preferred_element_type=F32)
        y = y + xs_p * dexp_ref[:, pair * LANES:(pair + 1) * LANES]
        dec_p = jnp.where(_lane_lt64((1, LANES)), jnp.broadcast_to(etot[:, r0:r0 + 1], (1, LANES)),
                          jnp.broadcast_to(etot[:, r1:r1 + 1], (1, LANES)))
        s_ref[pair] = s_p * dec_p + jnp.dot(jnp.concatenate(bts, axis=1), xb_bd, preferred_element_type=F32)
        ys.append(y)
    return jnp.concatenate(ys, axis=1)


def _ssd_fwd_body(n_c, xs_ref, bc_ref, dtl_ref, dtc_ref, dtt_ref, pr_ref, pc_ref, dexp_ref, y_ref, s_ref):
    step = pl.program_id(1)

    @pl.when(step == 0)
    def _():
        s_ref[...] = jnp.zeros_like(s_ref)

    dt_raw = jnp.where(step < n_c, dtc_ref[...], dtl_ref[...])
    y_ref[0] = _ssd_chunk(0, xs_ref, bc_ref, dt_raw, dtt_ref, pr_ref, pc_ref, dexp_ref, s_ref)


def _ssd_bwd_body(n_c, xs_ref, bc_ref, dtl_ref, dtc_ref, dtt_ref, pr_ref, pc_ref, dexp_ref, yf_ref, zl_ref, zc_ref,
                  g_ref, o_ref, s_ref):
    step = pl.program_id(1)

    @pl.when(step == 0)
    def _():
        s_ref[...] = jnp.zeros_like(s_ref)

    is_ctx = step < n_c
    dt_raw = jnp.where(is_ctx, dtc_ref[...], dtl_ref[...])
    y = _ssd_chunk(1, xs_ref, bc_ref, dt_raw, dtt_ref, pr_ref, pc_ref, dexp_ref, s_ref) + yf_ref[0]
    z = jnp.where(is_ctx, zc_ref[...], zl_ref[...])
    gated = y * (z * jax.nn.sigmoid(z))
    gw = SSD_INNER // SSD_GROUPS
    parts = []
    for g in range(SSD_GROUPS):
        gg = gated[:, g * gw:(g + 1) * gw]
        parts.append(gg * lax.rsqrt(jnp.mean(gg * gg, axis=-1, keepdims=True) + EPS))
    o_ref[0] = (jnp.concatenate(parts, axis=1) * g_ref[...]).astype(o_ref.dtype)


def _ssd_call(ux, s_lat, s_ctx, dtt, dt_bias, a_log, d_skip, norm_g, batch, seq, n_ctx):
    q = SSD_CHUNK
    n_lat, n_c = seq // q, n_ctx // q
    n_ch = n_lat + n_c
    nh = SSD_HEADS
    fwd_chunk = lambda s: jnp.where(s < n_c, n_lat + s, s - n_c)
    bwd_chunk = lambda s: n_ch - 1 - s
    bc_blk = SSD_INNER // (2 * SSD_GROUPS * SSD_STATE)
    dt_blk = (SSD_INNER + SSD_XBC) // LANES
    dexp = [jnp.repeat(d_skip[d], SSD_HEAD_DIM).reshape(1, SSD_INNER) for d in range(2)]
    prow = [jnp.stack([dt_bias[d], a_log[d]], axis=0) for d in range(2)]
    pcol = [jnp.stack([dt_bias[d], a_log[d]], axis=1) for d in range(2)]
    lat_row = lambda chunk: (lambda b, s: b * n_lat + jnp.clip(chunk(s), 0, n_lat - 1))
    ctx_row = lambda chunk: (lambda b, s: b * n_c + jnp.clip(chunk(s) - n_lat, 0, n_c - 1))

    def specs(chunk):
        lr, cr = lat_row(chunk), ctx_row(chunk)
        return [pl.BlockSpec((1, q, SSD_INNER), lambda b, s: (b, chunk(s), 0)),
                pl.BlockSpec((1, q, 2 * SSD_GROUPS * SSD_STATE), lambda b, s: (b, chunk(s), bc_blk)),
                pl.BlockSpec((q, LANES), lambda b, s: (lr(b, s), dt_blk)),
                pl.BlockSpec((q, LANES), lambda b, s: (cr(b, s), dt_blk)),
                pl.BlockSpec((1, 2 * nh, q), lambda b, s: (b, 0, chunk(s))),
                pl.BlockSpec((2, nh), lambda b, s: (0, 0)),
                pl.BlockSpec((nh, 2), lambda b, s: (0, 0)),
                pl.BlockSpec((1, SSD_INNER), lambda b, s: (0, 0))]

    state = pltpu.VMEM((nh // 2, SSD_STATE, LANES), F32)
    yblk = lambda chunk: pl.BlockSpec((1, q, SSD_INNER), lambda b, s: (b, chunk(s), 0))
    y_f = pl.pallas_call(
        functools.partial(_ssd_fwd_body, n_c),
        out_shape=jax.ShapeDtypeStruct((batch, seq + n_ctx, SSD_INNER), F32),
        grid=(batch, n_ch), in_specs=specs(fwd_chunk), out_specs=yblk(fwd_chunk),
        scratch_shapes=[state], compiler_params=_params("parallel", "arbitrary"), name="ssd_fwd",
    )(ux, ux, s_lat, s_ctx, dtt, prow[0], pcol[0], dexp[0])
    lr, cr = lat_row(bwd_chunk), ctx_row(bwd_chunk)
    return pl.pallas_call(
        functools.partial(_ssd_bwd_body, n_c),
        out_shape=jax.ShapeDtypeStruct((batch, seq + n_ctx, SSD_INNER), BF16),
        grid=(batch, n_ch),
        in_specs=specs(bwd_chunk) + [yblk(bwd_chunk),
                                     pl.BlockSpec((q, SSD_INNER), lambda b, s: (lr(b, s), 0)),
                                     pl.BlockSpec((q, SSD_INNER), lambda b, s: (cr(b, s), 0)),
                                     pl.BlockSpec((1, SSD_INNER), lambda b, s: (0, 0))],
        out_specs=yblk(bwd_chunk),
        scratch_shapes=[state], compiler_params=_params("parallel", "arbitrary"), name="ssd_bwd",
    )(ux, ux, s_lat, s_ctx, dtt, prow[1], pcol[1], dexp[1], y_f, s_lat, s_ctx, norm_g.reshape(1, SSD_INNER))


def _residual_norm(x, f, gate, g_post, pre):
    x = x + gate * _rms(f, g_post)
    if pre is None:
        return x, None
    g_pre, shift, scale = pre
    return x, (_rms(x, g_pre) * (1.0 + scale) + shift).astype(BF16)


def _wout_body(na_ref, da_ref, ssd_ref, w_ref, x_ref, gate_ref, gpost_ref, gpre_ref, shift_ref, scale_ref,
               xo_ref, h_ref):
    k1 = na_ref.shape[1]
    k2 = k1 + da_ref.shape[1]
    acc = jnp.dot(na_ref[...], w_ref[:k1, :], preferred_element_type=F32)
    acc = acc + jnp.dot(da_ref[...], w_ref[k1:k2, :], preferred_element_type=F32)
    acc = acc + jnp.dot(ssd_ref[0], w_ref[k2:, :], preferred_element_type=F32)
    xo_ref[...], h_ref[...] = _residual_norm(x_ref[...], acc, gate_ref[0], gpost_ref[...],
                                             (gpre_ref[...], shift_ref[0], scale_ref[0]))


def _wout_call(na, da, ssd, w, x, sel, res, pre, ssd_tile, tm):
    m = na.shape[0]
    d = w.shape[1]
    row = pl.BlockSpec((tm, d), lambda i: (i, 0))
    vec = pl.BlockSpec((1, d), lambda i: (0, 0))
    tab = pl.BlockSpec((1, 1, d), lambda i: (sel(i), 0, 0))
    return pl.pallas_call(
        _wout_body,
        out_shape=(jax.ShapeDtypeStruct((m, d), F32), jax.ShapeDtypeStruct((m, d), BF16)),
        grid=(m // tm,),
        in_specs=[pl.BlockSpec((tm, na.shape[1]), lambda i: (i, 0)),
                  pl.BlockSpec((tm, da.shape[1]), lambda i: (i, 0)),
                  pl.BlockSpec((1, tm, ssd.shape[2]), lambda i: ssd_tile(i) + (0,)),
                  pl.BlockSpec(w.shape, lambda i: (0, 0)),
                  row, tab, vec, vec, tab, tab],
        out_specs=(row, row),
        compiler_params=_params("parallel"), name="w_out",
    )(na, da, ssd, w, x, res[0], res[1], *pre)


def _batcher_pairs(n):
    pairs = []
    p = 1
    while p < n:
        k = p
        while k >= 1:
            for j in range(k % p, n - k, 2 * k):
                for i in range(min(k, n - j - k)):
                    if (i + j) // (2 * p) == (i + j + k) // (2 * p):
                        pairs.append((i + j, i + j + k))
            k //= 2
        p *= 2
    return pairs


def _top_vals_network(x, k):
    sub = 8
    tiles = [x[v * sub:(v + 1) * sub, :] for v in range(k)]

    def exchange(i, j):
        hi, lo = jnp.maximum(tiles[i], tiles[j]), jnp.minimum(tiles[i], tiles[j])
        tiles[i], tiles[j] = hi, lo

    for i, j in _batcher_pairs(k):
        exchange(i, j)
    shift = sub // 2
    while shift >= 1:
        other = [pltpu.roll(t, shift, 0) for t in tiles]
        tiles = [jnp.maximum(tiles[i], other[k - 1 - i]) for i in range(k)]
        d = k // 2
        while d >= 1:
            for i in range(k):
                if i & d == 0:
                    exchange(i, i + d)
            d //= 2
        shift //= 2
    return jnp.concatenate([t[0:1, :] for t in tiles], axis=0)


def _pair_candidates(v1, v2):
    k = PEER_TOPK
    half = k // 2
    row = lax.broadcasted_iota(jnp.int32, (half, v1.shape[1]), 0)
    parts = [v2 + v1[0:1, :]]
    for i in range(1, half):
        parts.append(jnp.where(row < k // (i + 1), v2[:half] + v1[i:i + 1, :], NEG_INF))
    parts.append(v1[half:] + v2[0:1, :])
    return jnp.concatenate(parts, axis=0)


def _peer_sel_body(h_ref, wq_ref, keys_ref, cnt_ref, e1_ref, rank_ref, e2_ref):
    qd = PEER_QDIM // 2
    k = PEER_TOPK
    qv = jnp.dot(h_ref[...], wq_ref[...], preferred_element_type=F32).astype(BF16)
    for hh in range(PEER_HEADS):
        c0 = hh * PEER_QDIM
        s1 = _nt_dot(keys_ref[2 * hh], qv[:, c0:c0 + qd])
        s2 = _nt_dot(keys_ref[2 * hh + 1], qv[:, c0 + qd:c0 + 2 * qd])
        v1 = _top_vals_network(s1, k)
        v2 = _top_vals_network(s2, k)
        rank = jnp.zeros_like(s2)
        for r in range(k):
            rank = jnp.where(v2[r:r + 1, :] > s2, float(r + 1), rank)
        cand = _pair_candidates(v1, v2)
        pad = jnp.full((8 * k - cand.shape[0], cand.shape[1]), NEG_INF, F32)
        top = _top_vals_network(jnp.concatenate([cand, pad], axis=0), k)
        thr = top[k - 1:k, :]
        z = jnp.sum(jnp.exp(top - top[0:1, :]), axis=0, keepdims=True)
        cnt_sorted = jnp.zeros_like(v1)
        for r in range(k):
            cnt_sorted = cnt_sorted + jnp.where(v1 + v2[r:r + 1, :] >= thr, 1.0, 0.0)
        cnt = jnp.zeros_like(s1)
        for r in range(k):
            cnt = jnp.where(s1 == v1[r:r + 1, :], cnt_sorted[r:r + 1, :], cnt)
        cnt_ref[hh] = cnt
        rank_ref[hh] = rank.astype(BF16)
        e1_ref[hh] = jnp.exp(s1 - v1[0:1, :])
        e2_ref[hh] = (jnp.exp(s2 - v2[0:1, :]) / z).astype(BF16)


def _peer_sel_call(h, wq, keys, tm=256):
    t, d = h.shape
    shape = (PEER_HEADS, PEER_N_KEYS, t)
    sblk = pl.BlockSpec((PEER_HEADS, PEER_N_KEYS, tm), lambda i: (0, 0, i))
    return pl.pallas_call(
        _peer_sel_body,
        out_shape=(jax.ShapeDtypeStruct(shape, F32), jax.ShapeDtypeStruct(shape, F32),
                   jax.ShapeDtypeStruct(shape, BF16), jax.ShapeDtypeStruct(shape, BF16)),
        grid=(t // tm,),
        in_specs=[pl.BlockSpec((tm, d), lambda i: (i, 0)),
                  pl.BlockSpec(wq.shape, lambda i: (0, 0)),
                  pl.BlockSpec(keys.shape, lambda i: (0, 0, 0))],
        out_specs=(sblk, sblk, sblk, sblk),
        compiler_params=_params("parallel"), name="peer_select",
    )(h, wq, keys)


def _gelu_tanh(x):
    c = math.sqrt(2.0 / math.pi)
    half = 0.5 * x
    return half + half * jnp.tanh(x * (c + (c * 0.044715) * (x * x)))


def _peer_ffn_body(n_a, n_split, has_pre, h_ref, u_ref, vt_ref, cnt_ref, e1_ref, rank_ref, e2_ref,
                   x_ref, gate_ref, gpost_ref, *rest):
    if has_pre:
        gpre_ref, shift_ref, scale_ref, xo_ref, ho_ref, acc_ref, wt_ref = rest
    else:
        xo_ref, acc_ref, wt_ref = rest
    j = pl.program_id(1)

    @pl.when(j == 0)
    def _():
        acc_ref[...] = jnp.zeros_like(acc_ref)

    nk = PEER_N_KEYS
    a_part = n_a // n_split
    rows = a_part * nk
    h = h_ref[...]

    def act(s):
        return _gelu_tanh(_nt_dot(u_ref[s * rows:(s + 1) * rows, :], h)).astype(BF16)

    act_next = act(0)
    for s in range(n_split):
        act_s = act_next
        if s + 1 < n_split:
            act_next = act(s + 1)
        for al in range(a_part):
            a = s * a_part + al
            gate = None
            for hh in range(PEER_HEADS):
                cnt = cnt_ref[hh, a:a + 1, :].astype(BF16)
                e1 = e1_ref[hh, a:a + 1, :].astype(BF16)
                w = jnp.where(rank_ref[hh] < cnt, e2_ref[hh], jnp.zeros((), BF16)) * e1
                gate = w if gate is None else gate + w
            wt_ref[a * nk:(a + 1) * nk, :] = gate * act_s[al * nk:(al + 1) * nk, :]
        acc_ref[...] += jnp.dot(vt_ref[:, s * rows:(s + 1) * rows], wt_ref[s * rows:(s + 1) * rows, :],
                                preferred_element_type=F32)

    @pl.when(j == pl.num_programs(1) - 1)
    def _():
        pre = (gpre_ref[...], shift_ref[0], scale_ref[0]) if has_pre else None
        x_new, h_next = _residual_norm(x_ref[...], acc_ref[...].T, gate_ref[0], gpost_ref[...], pre)
        xo_ref[...] = x_new
        if has_pre:
            ho_ref[...] = h_next


def _peer_ffn_call(h, u, vt, layer, cnt, e1, rank, e2, x, sel, res, pre, tm=TOKEN_TILE, n_a=8, n_split=2):
    t, d = h.shape
    e = u.shape[1]
    te = n_a * PEER_N_KEYS
    a_blk = pl.BlockSpec((PEER_HEADS, n_a, tm), lambda i, j: (0, j, i))
    b_blk = pl.BlockSpec((PEER_HEADS, PEER_N_KEYS, tm), lambda i, j: (0, 0, i))
    row = pl.BlockSpec((tm, d), lambda i, j: (i, 0))
    vec = pl.BlockSpec((1, d), lambda i, j: (0, 0))
    tab = pl.BlockSpec((1, 1, d), lambda i, j: (sel(i), 0, 0))
    args = [h, u, vt, cnt, e1, rank, e2, x, res[0], res[1]]
    specs = [row, pl.BlockSpec((None, te, d), lambda i, j: (layer, j, 0)),
             pl.BlockSpec((None, d, te), lambda i, j: (layer, 0, j)),
             a_blk, a_blk, b_blk, b_blk, row, tab, vec]
    outs = [jax.ShapeDtypeStruct((t, d), F32)]
    ospecs = [row]
    if pre is not None:
        args += list(pre)
        specs += [vec, tab, tab]
        outs.append(jax.ShapeDtypeStruct((t, d), BF16))
        ospecs.append(row)
    out = pl.pallas_call(
        functools.partial(_peer_ffn_body, n_a, n_split, pre is not None),
        out_shape=tuple(outs), grid=(t // tm, e // te), in_specs=specs, out_specs=tuple(ospecs),
        scratch_shapes=[pltpu.VMEM((d, tm), F32), pltpu.VMEM((te, tm), BF16)],
        compiler_params=_params("parallel", "arbitrary"), name="peer_ffn",
    )(*args)
    return (out[0], out[1]) if pre is not None else (out[0], None)


def _peer(h, wq, keys, u, vt, layer, x, sel, res, pre):
    cnt, e1, rank, e2 = _peer_sel_call(h, wq, keys)
    return _peer_ffn_call(h, u, vt, layer, cnt, e1, rank, e2, x, sel, res, pre)


def kernel(x, c, ctx, c_ctx, ada_w, ada_b, norm_g, w_in, w_out, na_rpb, ssd_conv_w, ssd_conv_b, ssd_dt_bias,
           ssd_a_log, ssd_d, ssd_norm_g, da_lambda, da_sub_g, peer_w_q, peer_sub_keys, peer_u, peer_v):
    batch, seq, d = x.shape
    n_ctx = ctx.shape[1]
    depth = ada_w.shape[0]
    xl = x.reshape(batch * seq, d)
    xc = ctx.reshape(batch * n_ctx, d)
    tm_norm = 256
    sel_lat = lambda i: i // (seq // tm_norm)
    sel_ctx = lambda i: batch
    cos_t, sin_t = _rope_tables(seq)

    o_nq, o_nk, o_nv = 0, 512, 1024
    o_dq, o_dk, o_dv = 1536, 2048, 2560
    o_z = 3072
    o_end = o_z + SSD_INNER + SSD_XBC + 2 * SSD_HEADS
    scale_a = jnp.concatenate([jnp.full((512,), NA_DIM ** -0.5, F32), jnp.ones((1536,), F32)])
    scale_b = jnp.concatenate([jnp.full((512,), DA_DIM ** -0.5, F32), jnp.ones((512,), F32)])

    cond = jnp.concatenate([c, c_ctx[None, :], jnp.zeros((8 - batch - 1, d), F32)], axis=0)
    mods = [_ada_call(cond, ada_w, ada_b, layer)[:batch + 1].reshape(batch + 1, N_MOD, 1, d)
            for layer in range(depth)]
    pu = peer_u.astype(BF16)
    pvt = jnp.swapaxes(peer_v, 1, 2).astype(BF16)
    h_l = h_c = None
    for layer in range(depth):
        need_ctx = layer < depth - 1
        lam_init = 0.8 - 0.6 * math.exp(-0.3 * layer)
        tab = [mods[layer][:, k] for k in range(N_MOD)]
        g = norm_g[layer]
        if layer == 0:
            _, h_l = _norm_call(xl, sel_lat, pre=(g[0:1], tab[0], tab[1]), tm=tm_norm)
            _, h_c = _norm_call(xc, sel_ctx, pre=(g[0:1], tab[0], tab[1]), tm=tm_norm)

        wi = w_in[layer]
        w_a = jnp.concatenate([wi[:, o_nq:o_dq], wi[:, o_dv:o_z]], axis=1).astype(BF16)
        w_b = wi[:, o_dq:o_dv].astype(BF16)
        w_c = jnp.pad(wi[:, o_z:o_end], ((0, 0), (0, SSD_IN_W - (o_end - o_z)))).astype(BF16)
        a_lat = _mm_call(h_l, w_a, BF16, colscale=scale_a, name="proj_attn")
        a_ctx = _mm_call(h_c, w_a, BF16, colscale=scale_a, name="proj_attn")
        qk_lat = _mm_call(h_l, w_b, BF16, colscale=scale_b, rope=(cos_t, sin_t), name="proj_rope")
        qk_ctx = _mm_call(h_c, w_b, BF16, colscale=scale_b, name="proj_qk_ctx")
        s_lat = _mm_call(h_l, w_c, F32, name="proj_ssd")
        s_ctx = _mm_call(h_c, w_c, F32, name="proj_ssd")

        lam_p = da_lambda[layer]
        g_sub = (da_sub_g[layer] * (1.0 - lam_init)).reshape(1, LANES)
        na_l = _na_call(a_lat, a_ctx, _na_bias(na_rpb[layer]), batch, seq, n_ctx)
        da_l = _da_call(qk_lat, qk_ctx, a_lat, a_ctx, lam_p, g_sub, lam_init, batch, seq, n_ctx)

        ux = _ssd_prep_call(s_lat, s_ctx, ssd_conv_w[layer], ssd_conv_b[layer], batch, seq, n_ctx)
        dt0 = SSD_INNER + SSD_XBC
        dt_cols = lambda s, n: s[:, dt0:dt0 + 2 * SSD_HEADS].reshape(batch, n, 2 * SSD_HEADS)
        dtt = jnp.transpose(jnp.concatenate([dt_cols(s_lat, seq), dt_cols(s_ctx, n_ctx)], axis=1), (0, 2, 1))
        ssd_o = _ssd_call(ux, s_lat, s_ctx, dtt, ssd_dt_bias[layer], ssd_a_log[layer], ssd_d[layer],
                          ssd_norm_g[layer], batch, seq, n_ctx)

        wo = w_out[layer].astype(BF16)
        wq = peer_w_q[layer].astype(BF16)
        keys = peer_sub_keys[layer].reshape(2 * PEER_HEADS, PEER_N_KEYS, PEER_QDIM // 2).astype(BF16)
        nxt = layer + 1 < depth

        lat_tiles = seq // TOKEN_TILE
        sel_lat_t = lambda i: i // lat_tiles
        res_mix, pre_ffn, res_ffn = (tab[2], g[1:2]), (g[2:3], tab[3], tab[4]), (tab[5], g[3:4])
        pre_n = (norm_g[layer + 1][0:1], mods[layer + 1][:, 0], mods[layer + 1][:, 1]) if nxt else None
        xl, h2_l = _wout_call(na_l, da_l, ssd_o, wo, xl, sel_lat_t, res_mix, pre_ffn,
                              lambda i: (i // lat_tiles, i % lat_tiles), TOKEN_TILE)
        xl, h_l = _peer(h2_l, wq, keys, pu, pvt, layer, xl, sel_lat_t, res_ffn, pre_n)

        if need_ctx:
            na_c, da_c = _ctx_attn_call(a_ctx, qk_ctx, lam_p, g_sub, lam_init, batch, n_ctx)
            xc, h2_c = _wout_call(na_c, da_c, ssd_o, wo, xc, sel_ctx, res_mix, pre_ffn,
                                  lambda i: (i, seq // n_ctx), n_ctx)
            xc, h_c = _peer(h2_c, wq, keys, pu, pvt, layer, xc, sel_ctx, res_ffn, pre_n)
    return xl.reshape(batch, seq, d)
```

```python
import functools
import math

import numpy as np
import jax
import jax.numpy as jnp
from jax import lax
from jax.experimental import pallas as pl
from jax.experimental.pallas import tpu as pltpu

F32 = jnp.float32
BF16 = jnp.bfloat16
EPS = 1e-6
NEG_INF = float("-inf")

VMEM_LIMIT_BYTES = 58 * 1024 * 1024
LANES = 128

GRID_W = 64
NA_DIM = 64
NA_HEADS = 8
NA_WIN_ROWS = 8
NA_WIN_COLS = 16
DA_DIM = 64
DA_HEADS = 4
ROPE_BASE = 10000.0
SSD_INNER = 1024
SSD_HEAD_DIM = 64
SSD_HEADS = 16
SSD_GROUPS = 2
SSD_STATE = 128
SSD_CONV = 5
SSD_CHUNK = 128
SSD_XBC = SSD_INNER + 2 * SSD_GROUPS * SSD_STATE
SSD_IN_W = SSD_INNER + SSD_XBC + LANES
PEER_HEADS = 8
PEER_QDIM = 256
PEER_N_KEYS = 128
PEER_TOPK = 16
N_MOD = 6
TOKEN_TILE = 512


def _params(*sem):
    return pltpu.CompilerParams(dimension_semantics=sem, vmem_limit_bytes=VMEM_LIMIT_BYTES)


def _nt_dot(a, b):
    return lax.dot_general(a, b, (((1,), (1,)), ((), ())), preferred_element_type=F32)


def _lane_lt64(shape):
    return lax.broadcasted_iota(jnp.int32, shape, len(shape) - 1) < (LANES // 2)


def _ada_body(c_ref, w_ref, b_ref, o_ref):
    c = c_ref[...]
    sc = (c * jax.nn.sigmoid(c)).astype(BF16)
    o_ref[...] = jnp.dot(sc, w_ref[...].astype(BF16), preferred_element_type=F32) + b_ref[...]


def _ada_call(cond, w, b, layer, tn=768):
    m, d = cond.shape
    depth, _, n = w.shape
    return pl.pallas_call(
        _ada_body,
        out_shape=jax.ShapeDtypeStruct((m, n), F32),
        grid=(n // tn,),
        in_specs=[pl.BlockSpec((m, d), lambda j: (0, 0)),
                  pl.BlockSpec((None, d, tn), lambda j: (layer, 0, j)),
                  pl.BlockSpec((None, 1, tn), lambda j: (layer, 0, j))],
        out_specs=pl.BlockSpec((m, tn), lambda j: (0, j)),
        compiler_params=_params("arbitrary"),
        name="ada_mod",
    )(cond, w, b.reshape(depth, 1, n))


def _rms(x, g):
    return x * lax.rsqrt(jnp.mean(x * x, axis=-1, keepdims=True) + EPS) * g


def _norm_body(has_res, has_pre, *refs):
    refs = list(refs)
    x = refs.pop(0)[...]
    if has_res:
        f_ref, gate_ref, gpost_ref = refs.pop(0), refs.pop(0), refs.pop(0)
        x = x + gate_ref[0] * _rms(f_ref[...], gpost_ref[...])
    if has_pre:
        gpre_ref, shift_ref, scale_ref = refs.pop(0), refs.pop(0), refs.pop(0)
    if has_res:
        refs.pop(0)[...] = x
    if has_pre:
        h = _rms(x, gpre_ref[...]) * (1.0 + scale_ref[0]) + shift_ref[0]
        refs.pop(0)[...] = h.astype(BF16)


def _norm_call(x, sel, *, res=None, pre=None, tm=256):
    m, d = x.shape
    row = pl.BlockSpec((tm, d), lambda i: (i, 0))
    vec = pl.BlockSpec((1, d), lambda i: (0, 0))
    tab = pl.BlockSpec((1, 1, d), lambda i: (sel(i), 0, 0))
    args, specs, outs, ospecs = [x], [row], [], []
    if res is not None:
        args += list(res)
        specs += [row, tab, vec]
        outs.append(jax.ShapeDtypeStruct((m, d), F32))
        ospecs.append(row)
    if pre is not None:
        args += list(pre)
        specs += [vec, tab, tab]
        outs.append(jax.ShapeDtypeStruct((m, d), BF16))
        ospecs.append(row)
    res_out = pl.pallas_call(
        functools.partial(_norm_body, res is not None, pre is not None),
        out_shape=tuple(outs), grid=(m // tm,), in_specs=specs, out_specs=tuple(ospecs),
        compiler_params=_params("parallel"), name="norm_mod",
    )(*args)
    res_out = list(res_out)
    x_new = res_out.pop(0) if res is not None else None
    h = res_out.pop(0) if pre is not None else None
    return x_new, h


def _rope_partner(x):
    lane = lax.broadcasted_iota(jnp.int32, x.shape, 1)
    first = (lane % 32) < 16
    return jnp.where(first, pltpu.roll(x, LANES - 16, 1), pltpu.roll(x, 16, 1))


def _mm_body(epi, h_ref, w_ref, *rest):
    o_ref = rest[-1]
    tm = h_ref.shape[0]
    n_part = 2 if (epi == "rope" and tm % 32 == 0) else 1
    part = tm // n_part

    def project(p):
        return jnp.dot(h_ref[p * part:(p + 1) * part, :], w_ref[...], preferred_element_type=F32)

    acc_next = project(0)
    for p in range(n_part):
        acc = acc_next
        if p + 1 < n_part:
            acc_next = project(p + 1)
        rows = slice(p * part, (p + 1) * part)
        if epi == "scale":
            acc = acc * rest[0][...]
        elif epi == "rope":
            cos_ref, sin_ref, cs_ref = rest[:3]
            cos = cos_ref[rows, :]
            sin = sin_ref[rows, :]
            blocks = []
            for c in range(acc.shape[1] // LANES):
                xb = acc[:, c * LANES:(c + 1) * LANES]
                blocks.append(xb * cos + _rope_partner(xb) * sin)
            acc = jnp.concatenate(blocks, axis=1) * cs_ref[...]
        o_ref[rows, :] = acc.astype(o_ref.dtype)


def _mm_call(h, w, out_dtype, *, colscale=None, rope=None, tm=512, tn=None, name="proj"):
    m, k = h.shape
    n = w.shape[1]
    tn = n if tn is None else tn
    args = [h, w]
    specs = [pl.BlockSpec((tm, k), lambda i, j: (i, 0)),
             pl.BlockSpec((k, tn), lambda i, j: (0, j))]
    epi = "plain"
    if rope is not None:
        cos_t, sin_t = rope
        nt = cos_t.shape[0] // tm
        args += [cos_t, sin_t]
        specs += [pl.BlockSpec((tm, LANES), lambda i, j: (i % nt, 0))] * 2
        epi = "rope"
    if colscale is not None:
        args.append(colscale.reshape(1, n))
        specs.append(pl.BlockSpec((1, tn), lambda i, j: (0, j)))
        epi = "scale" if epi == "plain" else epi
    return pl.pallas_call(
        functools.partial(_mm_body, epi),
        out_shape=jax.ShapeDtypeStruct((m, n), out_dtype),
        grid=(m // tm, n // tn), in_specs=specs,
        out_specs=pl.BlockSpec((tm, tn), lambda i, j: (i, j)),
        compiler_params=_params("parallel", "arbitrary"), name=name,
    )(*args)


def _rope_tables(seq):
    quarter = DA_DIM // 4
    inv_freq = 1.0 / (ROPE_BASE ** (jnp.arange(quarter, dtype=F32) / quarter))
    t = jnp.arange(seq)
    ang_r = (t // GRID_W).astype(F32)[:, None] * inv_freq
    ang_c = (t % GRID_W).astype(F32)[:, None] * inv_freq

    def unit(ang):
        return (jnp.concatenate([jnp.cos(ang), jnp.cos(ang)], -1),
                jnp.concatenate([-jnp.sin(ang), jnp.sin(ang)], -1))

    cr, sr = unit(ang_r)
    cc, sc = unit(ang_c)
    cos64 = jnp.concatenate([cr, cc], -1)
    sin64 = jnp.concatenate([sr, sc], -1)
    return jnp.tile(cos64, (1, 2)), jnp.tile(sin64, (1, 2))


def _stack_heads(q2):
    lo = _lane_lt64(q2.shape)
    zero = jnp.zeros_like(q2)
    return jnp.concatenate([jnp.where(lo, q2, zero), jnp.where(lo, zero, q2)], axis=0)


def _unstack_heads(o):
    r = o.shape[0] // 2
    return jnp.where(_lane_lt64((r, o.shape[1])), o[:r], o[r:])


NA_ROW_GROUP = 32


def _na_body(rows, q_ref, k_ref, v_ref, kc_ref, vc_ref, bias_ref, o_ref):
    kc = kc_ref[...]
    vc = vc_ref[...]
    kh = NA_WIN_ROWS

    def scores(r):
        rs = jnp.clip(r - kh // 2, 0, rows - kh)
        q0 = pl.multiple_of(r * GRID_W, GRID_W)
        k0 = pl.multiple_of(rs * GRID_W, GRID_W)
        qs = _stack_heads(q_ref[pl.ds(q0, GRID_W), :])
        s_loc = _nt_dot(qs, k_ref[pl.ds(k0, kh * GRID_W), :]) + bias_ref[0, r - rs]
        return q0, k0, s_loc, _nt_dot(qs, kc)

    def finish(q0, k0, p_loc, p_ctx, l):
        o = (jnp.dot(p_loc, v_ref[pl.ds(k0, kh * GRID_W), :], preferred_element_type=F32)
             + jnp.dot(p_ctx, vc, preferred_element_type=F32)) / l
        o_ref[pl.ds(q0, GRID_W), :] = _unstack_heads(o).astype(o_ref.dtype)

    group = math.gcd(rows, NA_ROW_GROUP)

    def row_group(gi, carry):
        nxt = scores(gi * group)
        pending = None
        for j in range(group):
            q0, k0, s_loc, s_ctx = nxt
            if j + 1 < group:
                nxt = scores(gi * group + j + 1)
            m = jnp.maximum(jnp.max(s_loc, axis=-1, keepdims=True), jnp.max(s_ctx, axis=-1, keepdims=True))
            p_loc = jnp.exp(s_loc - m)
            p_ctx = jnp.exp(s_ctx - m)
            l = jnp.sum(p_loc, axis=-1, keepdims=True) + jnp.sum(p_ctx, axis=-1, keepdims=True)
            if pending is not None:
                finish(*pending)
            pending = (q0, k0, p_loc.astype(BF16), p_ctx.astype(BF16), l)
        finish(*pending)
        return carry

    lax.fori_loop(0, rows // group, row_group, 0)


def _na_bias(rpb):
    kh, w = NA_WIN_ROWS, GRID_W
    c = np.arange(w)
    wstart = np.clip(c - NA_WIN_COLS // 2, 0, w - NA_WIN_COLS)
    kc = np.arange(w)
    valid = (kc[None, :] >= wstart[:, None]) & (kc[None, :] < wstart[:, None] + NA_WIN_COLS)
    cidx = np.clip(kc[None, :] - c[:, None] + NA_WIN_COLS - 1, 0, 2 * NA_WIN_COLS - 2)
    typ = np.arange(kh)
    ridx = np.arange(kh)[None, :] - typ[:, None] + NA_WIN_ROWS - 1
    pick_r = (ridx[:, :, None] == np.arange(2 * NA_WIN_ROWS - 1)).astype(np.float32)
    pick_c = ((cidx[:, :, None] == np.arange(2 * NA_WIN_COLS - 1)) & valid[:, :, None]).astype(np.float32)
    b = jnp.einsum('hrc,tjr,qkc->htqjk', rpb.astype(F32), pick_r, pick_c, precision=lax.Precision.HIGHEST)
    b = b + np.where(valid, 0.0, NEG_INF).astype(np.float32)[None, None, :, None, :]
    b = b.reshape(NA_HEADS // 2, 2, kh, w, kh * w)
    return jnp.transpose(b, (0, 2, 1, 3, 4)).reshape(NA_HEADS // 2, kh, 2 * w, kh * w).astype(F32)


def _na_call(a_lat, a_ctx, bias, batch, seq, n_ctx):
    rows = seq // GRID_W
    pairs = NA_HEADS // 2
    lat = lambda c0: pl.BlockSpec((seq, LANES), lambda b, p: (b, c0 + p))
    ctx = lambda c0: pl.BlockSpec((n_ctx, LANES), lambda b, p: (b, c0 + p))
    return pl.pallas_call(
        functools.partial(_na_body, rows),
        out_shape=jax.ShapeDtypeStruct((batch * seq, NA_HEADS * NA_DIM), BF16),
        grid=(batch, pairs),
        in_specs=[lat(0), lat(pairs), lat(2 * pairs), ctx(pairs), ctx(2 * pairs),
                  pl.BlockSpec((1,) + bias.shape[1:], lambda b, p: (p, 0, 0, 0))],
        out_specs=pl.BlockSpec((seq, LANES), lambda b, p: (b, p)),
        compiler_params=_params("parallel", "parallel"), name="na_attn",
    )(a_lat, a_lat, a_lat, a_ctx, a_ctx, bias)


def _da_lambda(lam_ref, lam_init):
    lf = lam_ref[...]
    a = jnp.sum(lf[0:1] * lf[1:2], axis=-1, keepdims=True)
    b = jnp.sum(lf[2:3] * lf[3:4], axis=-1, keepdims=True)
    return jnp.exp(a) - jnp.exp(b) + lam_init


def _da_finish(acc, l, lam, g):
    tq = acc.shape[0] // 2
    o = acc / l
    d = o[:tq] - lam * o[tq:]
    return d * lax.rsqrt(jnp.mean(d * d, axis=-1, keepdims=True) + EPS) * g


def _flash_step(qs, k, v, m, l, acc):
    s = _nt_dot(qs, k)
    m_new = jnp.maximum(m, jnp.max(s, axis=-1, keepdims=True))
    alpha = jnp.exp(m - m_new)
    p = jnp.exp(s - m_new)
    l = alpha * l + jnp.sum(p, axis=-1, keepdims=True)
    acc = alpha * acc + jnp.dot(p.astype(BF16), v, preferred_element_type=F32)
    return m_new, l, acc


def _da_body(lam_init, n_chunks, tk, q_ref, k_ref, v_ref, kc_ref, vc_ref, lam_ref, g_ref, o_ref):
    qs = _stack_heads(q_ref[...])
    r = qs.shape[0]
    chunks = [(k_ref, v_ref, c * tk, tk) for c in range(n_chunks)] + [(kc_ref, vc_ref, 0, kc_ref.shape[0])]

    def scores(i):
        kr, _, k0, n = chunks[i]
        return _nt_dot(qs, kr[k0:k0 + n, :])

    m = jnp.full((r, 1), NEG_INF, F32)
    l = jnp.zeros((r, 1), F32)
    acc = jnp.zeros((r, LANES), F32)
    s_next = scores(0)
    pending = None
    for i in range(len(chunks)):
        s = s_next
        if i + 1 < len(chunks):
            s_next = scores(i + 1)
        m_new = jnp.maximum(m, jnp.max(s, axis=-1, keepdims=True))
        alpha = jnp.exp(m - m_new)
        p = jnp.exp(s - m_new)
        l = alpha * l + jnp.sum(p, axis=-1, keepdims=True)
        m = m_new
        if pending is not None:
            p_prev, alpha_prev, (_, vr, k0, n) = pending
            acc = alpha_prev * acc + jnp.dot(p_prev, vr[k0:k0 + n, :], preferred_element_type=F32)
        pending = (p.astype(BF16), alpha, chunks[i])
    p_prev, alpha_prev, (_, vr, k0, n) = pending
    acc = alpha_prev * acc + jnp.dot(p_prev, vr[k0:k0 + n, :], preferred_element_type=F32)
    o_ref[...] = _da_finish(acc, l, _da_lambda(lam_ref, lam_init), g_ref[...]).astype(o_ref.dtype)


def _da_call(qk_lat, qk_ctx, a_lat, a_ctx, lam_p, g_scaled, lam_init, batch, seq, n_ctx, tq=256, tk=512):
    nq = seq // tq
    h = DA_HEADS
    return pl.pallas_call(
        functools.partial(_da_body, lam_init, seq // tk, tk),
        out_shape=jax.ShapeDtypeStruct((batch * seq, h * 2 * DA_DIM), BF16),
        grid=(batch, h, nq),
        in_specs=[pl.BlockSpec((tq, LANES), lambda b, hh, i: (b * nq + i, hh)),
                  pl.BlockSpec((seq, LANES), lambda b, hh, i: (b, h + hh)),
                  pl.BlockSpec((seq, LANES), lambda b, hh, i: (b, 3 * h + hh)),
                  pl.BlockSpec((n_ctx, LANES), lambda b, hh, i: (b, h + hh)),
                  pl.BlockSpec((n_ctx, LANES), lambda b, hh, i: (b, 3 * h + hh)),
                  pl.BlockSpec(lam_p.shape, lambda b, hh, i: (0, 0)),
                  pl.BlockSpec((1, LANES), lambda b, hh, i: (0, 0))],
        out_specs=pl.BlockSpec((tq, LANES), lambda b, hh, i: (b * nq + i, hh)),
        compiler_params=_params("parallel", "parallel", "arbitrary"), name="da_attn",
    )(qk_lat, qk_lat, a_lat, qk_ctx, a_ctx, lam_p, g_scaled)


def _ctx_attn_body(lam_init, a_ref, qk_ref, lam_ref, g_ref, na_ref, da_ref):
    pairs = NA_HEADS // 2
    for p in range(pairs):
        qs = _stack_heads(a_ref[:, p * LANES:(p + 1) * LANES])
        k = a_ref[:, (pairs + p) * LANES:(pairs + p + 1) * LANES]
        v = a_ref[:, (2 * pairs + p) * LANES:(2 * pairs + p + 1) * LANES]
        s = _nt_dot(qs, k)
        e = jnp.exp(s - jnp.max(s, axis=-1, keepdims=True))
        o = jnp.dot(e.astype(BF16), v, preferred_element_type=F32) / jnp.sum(e, axis=-1, keepdims=True)
        na_ref[:, p * LANES:(p + 1) * LANES] = _unstack_heads(o).astype(na_ref.dtype)
    lam = _da_lambda(lam_ref, lam_init)
    for hh in range(DA_HEADS):
        qs = _stack_heads(qk_ref[:, hh * LANES:(hh + 1) * LANES])
        k = qk_ref[:, (DA_HEADS + hh) * LANES:(DA_HEADS + hh + 1) * LANES]
        v = a_ref[:, (3 * pairs + hh) * LANES:(3 * pairs + hh + 1) * LANES]
        r = qs.shape[0]
        init = (jnp.full((r, 1), NEG_INF, F32), jnp.zeros((r, 1), F32), jnp.zeros((r, LANES), F32))
        m, l, acc = _flash_step(qs, k, v, *init)
        da_ref[:, hh * LANES:(hh + 1) * LANES] = _da_finish(acc, l, lam, g_ref[...]).astype(da_ref.dtype)


def _ctx_attn_call(a_ctx, qk_ctx, lam_p, g_scaled, lam_init, batch, n_ctx):
    return pl.pallas_call(
        functools.partial(_ctx_attn_body, lam_init),
        out_shape=(jax.ShapeDtypeStruct((batch * n_ctx, NA_HEADS * NA_DIM), BF16),
                   jax.ShapeDtypeStruct((batch * n_ctx, DA_HEADS * 2 * DA_DIM), BF16)),
        grid=(batch,),
        in_specs=[pl.BlockSpec((n_ctx, a_ctx.shape[1]), lambda b: (b, 0)),
                  pl.BlockSpec((n_ctx, qk_ctx.shape[1]), lambda b: (b, 0)),
                  pl.BlockSpec(lam_p.shape, lambda b: (0, 0)),
                  pl.BlockSpec((1, LANES), lambda b: (0, 0))],
        out_specs=(pl.BlockSpec((n_ctx, NA_HEADS * NA_DIM), lambda b: (b, 0)),
                   pl.BlockSpec((n_ctx, DA_HEADS * 2 * DA_DIM), lambda b: (b, 0))),
        compiler_params=_params("parallel"), name="ctx_attn",
    )(a_ctx, qk_ctx, lam_p, g_scaled)


def _conv_seg(u, w, bias):
    n = u.shape[0]
    t = lax.broadcasted_iota(jnp.int32, u.shape, 0)
    acc = jnp.zeros_like(u) + bias
    for kk in range(SSD_CONV):
        d = kk - SSD_CONV // 2
        if d == 0:
            sh = u
        else:
            sh = pltpu.roll(u, (-d) % n, 0)
            sh = jnp.where((t + d >= 0) & (t + d < n), sh, 0.0)
        acc = acc + sh * w[kk:kk + 1, :]
    return acc * jax.nn.sigmoid(acc)


def _ssd_prep_body(seq, lat_ref, ctx_ref, w_ref, b_ref, o_ref):
    w = w_ref[...]
    bias = b_ref[...]
    o_ref[0, :seq, :] = _conv_seg(lat_ref[...], w, bias)
    o_ref[0, seq:, :] = _conv_seg(ctx_ref[...], w, bias)


def _ssd_prep_call(s_lat, s_ctx, conv_w, conv_b, batch, seq, n_ctx):
    n_zc = SSD_INNER // LANES
    return pl.pallas_call(
        functools.partial(_ssd_prep_body, seq),
        out_shape=jax.ShapeDtypeStruct((batch, seq + n_ctx, SSD_XBC), F32),
        grid=(batch, SSD_XBC // LANES),
        in_specs=[pl.BlockSpec((seq, LANES), lambda b, j: (b, n_zc + j)),
                  pl.BlockSpec((n_ctx, LANES), lambda b, j: (b, n_zc + j)),
                  pl.BlockSpec((SSD_CONV, LANES), lambda b, j: (0, j)),
                  pl.BlockSpec((1, LANES), lambda b, j: (0, j))],
        out_specs=pl.BlockSpec((1, seq + n_ctx, LANES), lambda b, j: (b, 0, j)),
        compiler_params=_params("parallel", "parallel"), name="ssd_prep",
    )(s_lat, s_ctx, conv_w, conv_b.reshape(1, -1))


def _softplus(x):
    return jnp.maximum(x, 0.0) + jnp.log1p(jnp.exp(-jnp.abs(x)))


def _ssd_chunk(direction, xs_ref, bc_ref, dt_raw, dtt_ref, pr_ref, pc_ref, dexp_ref, s_ref):
    q = SSD_CHUNK
    nh = SSD_HEADS
    hi = lax.Precision.HIGHEST
    d0 = direction * nh
    dt = _softplus(dt_raw[:, d0:d0 + nh] + pr_ref[0:1, :])
    a = dt * (-jnp.exp(pr_ref[1:2, :]))
    dtt = _softplus(dtt_ref[0][d0:d0 + nh, :] + pc_ref[:, 0:1])
    at = dtt * (-jnp.exp(pc_ref[:, 1:2]))
    ii = lax.broadcasted_iota(jnp.int32, (q, q), 0)
    jj = lax.broadcasted_iota(jnp.int32, (q, q), 1)
    tri = (jj <= ii) if direction == 0 else (jj >= ii)
    tri_f = tri.astype(F32)
    cs = jnp.dot(tri_f, a, precision=hi, preferred_element_type=F32)
    cst = lax.dot_general(at, tri_f, (((1,), (1,)), ((), ())), precision=hi,
                          preferred_element_type=F32)
    edge = q - 1 if direction == 0 else 0
    tot_t = cst[:, edge:edge + 1]
    dst = jnp.exp(tot_t - cst) * dtt
    cst_dt = cst - jnp.log(dtt)
    etot = jnp.exp(cs[edge:edge + 1, :])

    lo = _lane_lt64((q, LANES))
    groups = []
    for g in range(SSD_GROUPS):
        b_g = bc_ref[0][:, g * SSD_STATE:(g + 1) * SSD_STATE]
        c_g = bc_ref[0][:, (SSD_GROUPS + g) * SSD_STATE:(SSD_GROUPS + g + 1) * SSD_STATE]
        groups.append((c_g, _nt_dot(c_g.astype(BF16), b_g.astype(BF16)), b_g.T))
    ys = []
    for pair in range(nh // 2):
        c_g, cb, bt_g = groups[(2 * pair) // (nh // SSD_GROUPS)]
        xs_p = xs_ref[0][:, pair * LANES:(pair + 1) * LANES]
        r0, r1 = 2 * pair, 2 * pair + 1
        xb = xs_p.astype(BF16)
        zero_b = jnp.zeros_like(xb)
        xb_bd = jnp.concatenate([jnp.where(lo, xb, zero_b), jnp.where(lo, zero_b, xb)], axis=0)
        s_p = s_ref[pair]
        s_b = s_p.astype(BF16)
        s_bd = jnp.concatenate([jnp.where(lo, s_b, zero_b), jnp.where(lo, zero_b, s_b)], axis=0)
        lhs, lhs_state, bts = [], [], []
        for r in (r0, r1):
            colb = jnp.broadcast_to(cs[:, r:r + 1], (q, q))
            rowb = jnp.broadcast_to(cst_dt[r:r + 1, :], (q, q))
            decay = jnp.exp(jnp.where(tri, colb - rowb, NEG_INF))
            lhs.append((cb * decay).astype(BF16))
            lhs_state.append((c_g * jnp.exp(colb)).astype(BF16))
            bts.append((bt_g * jnp.broadcast_to(dst[r:r + 1, :], (SSD_STATE, q))).astype(BF16))
        y = jnp.dot(jnp.concatenate(lhs + lhs_state, axis=1), jnp.concatenate([xb_bd, s_bd], axis=0),
                    preferred_element_type=F32)
        y = y + xs_p * dexp_ref[:, pair * LANES:(pair + 1) * LANES]
        dec_p = jnp.where(_lane_lt64((1, LANES)), jnp.broadcast_to(etot[:, r0:r0 + 1], (1, LANES)),
                          jnp.broadcast_to(etot[:, r1:r1 + 1], (1, LANES)))
        s_ref[pair] = s_p * dec_p + jnp.dot(jnp.concatenate(bts, axis=1), xb_bd, preferred_element_type=F32)
        ys.append(y)
    return jnp.concatenate(ys, axis=1)


def _ssd_fwd_body(n_c, xs_ref, bc_ref, dtl_ref, dtc_ref, dtt_ref, pr_ref, pc_ref, dexp_ref, y_ref, s_ref):
    step = pl.program_id(1)

    @pl.when(step == 0)
    def _():
        s_ref[...] = jnp.zeros_like(s_ref)

    dt_raw = jnp.where(step < n_c, dtc_ref[...], dtl_ref[...])
    y_ref[0] = _ssd_chunk(0, xs_ref, bc_ref, dt_raw, dtt_ref, pr_ref, pc_ref, dexp_ref, s_ref)


def _ssd_bwd_body(n_c, xs_ref, bc_ref, dtl_ref, dtc_ref, dtt_ref, pr_ref, pc_ref, dexp_ref, yf_ref, zl_ref, zc_ref,
                  g_ref, o_ref, s_ref):
    step = pl.program_id(1)

    @pl.when(step == 0)
    def _():
        s_ref[...] = jnp.zeros_like(s_ref)

    is_ctx = step < n_c
    dt_raw = jnp.where(is_ctx, dtc_ref[...], dtl_ref[...])
    y = _ssd_chunk(1, xs_ref, bc_ref, dt_raw, dtt_ref, pr_ref, pc_ref, dexp_ref, s_ref) + yf_ref[0]
    z = jnp.where(is_ctx, zc_ref[...], zl_ref[...])
    gated = y * (z * jax.nn.sigmoid(z))
    gw = SSD_INNER // SSD_GROUPS
    parts = []
    for g in range(SSD_GROUPS):
        gg = gated[:, g * gw:(g + 1) * gw]
        parts.append(gg * lax.rsqrt(jnp.mean(gg * gg, axis=-1, keepdims=True) + EPS))
    o_ref[0] = (jnp.concatenate(parts, axis=1) * g_ref[...]).astype(o_ref.dtype)


def _ssd_call(ux, s_lat, s_ctx, dtt, dt_bias, a_log, d_skip, norm_g, batch, seq, n_ctx):
    q = SSD_CHUNK
    n_lat, n_c = seq // q, n_ctx // q
    n_ch = n_lat + n_c
    nh = SSD_HEADS
    fwd_chunk = lambda s: jnp.where(s < n_c, n_lat + s, s - n_c)
    bwd_chunk = lambda s: n_ch - 1 - s
    bc_blk = SSD_INNER // (2 * SSD_GROUPS * SSD_STATE)
    dt_blk = (SSD_INNER + SSD_XBC) // LANES
    dexp = [jnp.repeat(d_skip[d], SSD_HEAD_DIM).reshape(1, SSD_INNER) for d in range(2)]
    prow = [jnp.stack([dt_bias[d], a_log[d]], axis=0) for d in range(2)]
    pcol = [jnp.stack([dt_bias[d], a_log[d]], axis=1) for d in range(2)]
    lat_row = lambda chunk: (lambda b, s: b * n_lat + jnp.clip(chunk(s), 0, n_lat - 1))
    ctx_row = lambda chunk: (lambda b, s: b * n_c + jnp.clip(chunk(s) - n_lat, 0, n_c - 1))

    def specs(chunk):
        lr, cr = lat_row(chunk), ctx_row(chunk)
        return [pl.BlockSpec((1, q, SSD_INNER), lambda b, s: (b, chunk(s), 0)),
                pl.BlockSpec((1, q, 2 * SSD_GROUPS * SSD_STATE), lambda b, s: (b, chunk(s), bc_blk)),
                pl.BlockSpec((q, LANES), lambda b, s: (lr(b, s), dt_blk)),
                pl.BlockSpec((q, LANES), lambda b, s: (cr(b, s), dt_blk)),
                pl.BlockSpec((1, 2 * nh, q), lambda b, s: (b, 0, chunk(s))),
                pl.BlockSpec((2, nh), lambda b, s: (0, 0)),
                pl.BlockSpec((nh, 2), lambda b, s: (0, 0)),
                pl.BlockSpec((1, SSD_INNER), lambda b, s: (0, 0))]

    state = pltpu.VMEM((nh // 2, SSD_STATE, LANES), F32)
    yblk = lambda chunk: pl.BlockSpec((1, q, SSD_INNER), lambda b, s: (b, chunk(s), 0))
    y_f = pl.pallas_call(
        functools.partial(_ssd_fwd_body, n_c),
        out_shape=jax.ShapeDtypeStruct((batch, seq + n_ctx, SSD_INNER), F32),
        grid=(batch, n_ch), in_specs=specs(fwd_chunk), out_specs=yblk(fwd_chunk),
        scratch_shapes=[state], compiler_params=_params("parallel", "arbitrary"), name="ssd_fwd",
    )(ux, ux, s_lat, s_ctx, dtt, prow[0], pcol[0], dexp[0])
    lr, cr = lat_row(bwd_chunk), ctx_row(bwd_chunk)
    return pl.pallas_call(
        functools.partial(_ssd_bwd_body, n_c),
        out_shape=jax.ShapeDtypeStruct((batch, seq + n_ctx, SSD_INNER), BF16),
        grid=(batch, n_ch),
        in_specs=specs(bwd_chunk) + [yblk(bwd_chunk),
                                     pl.BlockSpec((q, SSD_INNER), lambda b, s: (lr(b, s), 0)),
                                     pl.BlockSpec((q, SSD_INNER), lambda b, s: (cr(b, s), 0)),
                                     pl.BlockSpec((1, SSD_INNER), lambda b, s: (0, 0))],
        out_specs=yblk(bwd_chunk),
        scratch_shapes=[state], compiler_params=_params("parallel", "arbitrary"), name="ssd_bwd",
    )(ux, ux, s_lat, s_ctx, dtt, prow[1], pcol[1], dexp[1], y_f, s_lat, s_ctx, norm_g.reshape(1, SSD_INNER))


def _residual_norm(x, f, gate, g_post, pre):
    x = x + gate * _rms(f, g_post)
    if pre is None:
        return x, None
    g_pre, shift, scale = pre
    return x, (_rms(x, g_pre) * (1.0 + scale) + shift).astype(BF16)


def _wout_body(na_ref, da_ref, ssd_ref, w_ref, x_ref, gate_ref, gpost_ref, gpre_ref, shift_ref, scale_ref,
               xo_ref, h_ref):
    k1 = na_ref.shape[1]
    k2 = k1 + da_ref.shape[1]
    tm = na_ref.shape[0]
    n_part = 2 if tm % 32 == 0 else 1
    part = tm // n_part

    def project(r0):
        rows = slice(r0, r0 + part)
        acc = jnp.dot(na_ref[rows, :], w_ref[:k1, :], preferred_element_type=F32)
        acc = acc + jnp.dot(da_ref[rows, :], w_ref[k1:k2, :], preferred_element_type=F32)
        return acc + jnp.dot(ssd_ref[0, rows, :], w_ref[k2:, :], preferred_element_type=F32)

    acc_next = project(0)
    for p in range(n_part):
        acc = acc_next
        if p + 1 < n_part:
            acc_next = project((p + 1) * part)
        rows = slice(p * part, (p + 1) * part)
        xo_ref[rows, :], h_ref[rows, :] = _residual_norm(x_ref[rows, :], acc, gate_ref[0], gpost_ref[...],
                                                         (gpre_ref[...], shift_ref[0], scale_ref[0]))


def _wout_call(na, da, ssd, w, x, sel, res, pre, ssd_tile, tm):
    m = na.shape[0]
    d = w.shape[1]
    row = pl.BlockSpec((tm, d), lambda i: (i, 0))
    vec = pl.BlockSpec((1, d), lambda i: (0, 0))
    tab = pl.BlockSpec((1, 1, d), lambda i: (sel(i), 0, 0))
    return pl.pallas_call(
        _wout_body,
        out_shape=(jax.ShapeDtypeStruct((m, d), F32), jax.ShapeDtypeStruct((m, d), BF16)),
        grid=(m // tm,),
        in_specs=[pl.BlockSpec((tm, na.shape[1]), lambda i: (i, 0)),
                  pl.BlockSpec((tm, da.shape[1]), lambda i: (i, 0)),
                  pl.BlockSpec((1, tm, ssd.shape[2]), lambda i: ssd_tile(i) + (0,)),
                  pl.BlockSpec(w.shape, lambda i: (0, 0)),
                  row, tab, vec, vec, tab, tab],
        out_specs=(row, row),
        compiler_params=_params("parallel"), name="w_out",
    )(na, da, ssd, w, x, res[0], res[1], *pre)


def _batcher_pairs(n):
    pairs = []
    p = 1
    while p < n:
        k = p
        while k >= 1:
            for j in range(k % p, n - k, 2 * k):
                for i in range(min(k, n - j - k)):
                    if (i + j) // (2 * p) == (i + j + k) // (2 * p):
                        pairs.append((i + j, i + j + k))
            k //= 2
        p *= 2
    return pairs


def _top_vals_network(x, k):
    sub = 8
    tiles = [x[v * sub:(v + 1) * sub, :] for v in range(k)]

    def exchange(i, j):
        hi, lo = jnp.maximum(tiles[i], tiles[j]), jnp.minimum(tiles[i], tiles[j])
        tiles[i], tiles[j] = hi, lo

    for i, j in _batcher_pairs(k):
        exchange(i, j)
    shift = sub // 2
    while shift >= 1:
        other = [pltpu.roll(t, shift, 0) for t in tiles]
        tiles = [jnp.maximum(tiles[i], other[k - 1 - i]) for i in range(k)]
        d = k // 2
        while d >= 1:
            for i in range(k):
                if i & d == 0:
                    exchange(i, i + d)
            d //= 2
        shift //= 2
    return jnp.concatenate([t[0:1, :] for t in tiles], axis=0)


def _pair_candidates(v1, v2):
    k = PEER_TOPK
    half = k // 2
    row = lax.broadcasted_iota(jnp.int32, (half, v1.shape[1]), 0)
    parts = [v2 + v1[0:1, :]]
    for i in range(1, half):
        parts.append(jnp.where(row < k // (i + 1), v2[:half] + v1[i:i + 1, :], NEG_INF))
    parts.append(v1[half:] + v2[0:1, :])
    return jnp.concatenate(parts, axis=0)


def _peer_sel_body(h_ref, wq_ref, keys_ref, cnt_ref, e1_ref, rank_ref, e2_ref):
    qd = PEER_QDIM // 2
    k = PEER_TOPK
    qv = jnp.dot(h_ref[...], wq_ref[...], preferred_element_type=F32).astype(BF16)
    for hh in range(PEER_HEADS):
        c0 = hh * PEER_QDIM
        s1 = _nt_dot(keys_ref[2 * hh], qv[:, c0:c0 + qd])
        s2 = _nt_dot(keys_ref[2 * hh + 1], qv[:, c0 + qd:c0 + 2 * qd])
        v1 = _top_vals_network(s1, k)
        v2 = _top_vals_network(s2, k)
        rank = jnp.zeros_like(s2)
        for r in range(k):
            rank = jnp.where(v2[r:r + 1, :] > s2, float(r + 1), rank)
        cand = _pair_candidates(v1, v2)
        pad = jnp.full((8 * k - cand.shape[0], cand.shape[1]), NEG_INF, F32)
        top = _top_vals_network(jnp.concatenate([cand, pad], axis=0), k)
        thr = top[k - 1:k, :]
        z = jnp.sum(jnp.exp(top - top[0:1, :]), axis=0, keepdims=True)
        cnt_sorted = jnp.zeros_like(v1)
        for r in range(k):
            cnt_sorted = cnt_sorted + jnp.where(v1 + v2[r:r + 1, :] >= thr, 1.0, 0.0)
        cnt = jnp.zeros_like(s1)
        for r in range(k):
            cnt = jnp.where(s1 == v1[r:r + 1, :], cnt_sorted[r:r + 1, :], cnt)
        cnt_ref[hh] = cnt
        rank_ref[hh] = rank.astype(BF16)
        e1_ref[hh] = jnp.exp(s1 - v1[0:1, :])
        e2_ref[hh] = (jnp.exp(s2 - v2[0:1, :]) / z).astype(BF16)


def _peer_sel_call(h, wq, keys, tm=256):
    t, d = h.shape
    shape = (PEER_HEADS, PEER_N_KEYS, t)
    sblk = pl.BlockSpec((PEER_HEADS, PEER_N_KEYS, tm), lambda i: (0, 0, i))
    return pl.pallas_call(
        _peer_sel_body,
        out_shape=(jax.ShapeDtypeStruct(shape, F32), jax.ShapeDtypeStruct(shape, F32),
                   jax.ShapeDtypeStruct(shape, BF16), jax.ShapeDtypeStruct(shape, BF16)),
        grid=(t // tm,),
        in_specs=[pl.BlockSpec((tm, d), lambda i: (i, 0)),
                  pl.BlockSpec(wq.shape, lambda i: (0, 0)),
                  pl.BlockSpec(keys.shape, lambda i: (0, 0, 0))],
        out_specs=(sblk, sblk, sblk, sblk),
        compiler_params=_params("parallel"), name="peer_select",
    )(h, wq, keys)


def _gelu_tanh(x):
    return 0.5 * x * (1.0 + jnp.tanh(math.sqrt(2.0 / math.pi) * (x + 0.044715 * (x * x * x))))


def _peer_ffn_body(n_a, n_split, has_pre, h_ref, u_ref, vt_ref, cnt_ref, e1_ref, rank_ref, e2_ref,
                   x_ref, gate_ref, gpost_ref, *rest):
    if has_pre:
        gpre_ref, shift_ref, scale_ref, xo_ref, ho_ref, acc_ref, wt_ref = rest
    else:
        xo_ref, acc_ref, wt_ref = rest
    j = pl.program_id(1)

    @pl.when(j == 0)
    def _():
        acc_ref[...] = jnp.zeros_like(acc_ref)

    nk = PEER_N_KEYS
    a_part = n_a // n_split
    rows = a_part * nk
    h = h_ref[...]

    def act(s):
        return _gelu_tanh(_nt_dot(u_ref[s * rows:(s + 1) * rows, :], h)).astype(BF16)

    act_next = act(0)
    for s in range(n_split):
        act_s = act_next
        if s + 1 < n_split:
            act_next = act(s + 1)
        for al in range(a_part):
            a = s * a_part + al
            gate = None
            for hh in range(PEER_HEADS):
                cnt = cnt_ref[hh, a:a + 1, :].astype(BF16)
                e1 = e1_ref[hh, a:a + 1, :].astype(BF16)
                w = jnp.where(rank_ref[hh] < cnt, e2_ref[hh], jnp.zeros((), BF16)) * e1
                gate = w if gate is None else gate + w
            wt_ref[a * nk:(a + 1) * nk, :] = gate * act_s[al * nk:(al + 1) * nk, :]
        acc_ref[...] += jnp.dot(vt_ref[:, s * rows:(s + 1) * rows], wt_ref[s * rows:(s + 1) * rows, :],
                                preferred_element_type=F32)

    @pl.when(j == pl.num_programs(1) - 1)
    def _():
        pre = (gpre_ref[...], shift_ref[0], scale_ref[0]) if has_pre else None
        x_new, h_next = _residual_norm(x_ref[...], acc_ref[...].T, gate_ref[0], gpost_ref[...], pre)
        xo_ref[...] = x_new
        if has_pre:
            ho_ref[...] = h_next


def _peer_ffn_call(h, u, vt, layer, cnt, e1, rank, e2, x, sel, res, pre, tm=TOKEN_TILE, n_a=8, n_split=2):
    t, d = h.shape
    e = u.shape[1]
    te = n_a * PEER_N_KEYS
    a_blk = pl.BlockSpec((PEER_HEADS, n_a, tm), lambda i, j: (0, j, i))
    b_blk = pl.BlockSpec((PEER_HEADS, PEER_N_KEYS, tm), lambda i, j: (0, 0, i))
    row = pl.BlockSpec((tm, d), lambda i, j: (i, 0))
    vec = pl.BlockSpec((1, d), lambda i, j: (0, 0))
    tab = pl.BlockSpec((1, 1, d), lambda i, j: (sel(i), 0, 0))
    args = [h, u, vt, cnt, e1, rank, e2, x, res[0], res[1]]
    specs = [row, pl.BlockSpec((None, te, d), lambda i, j: (layer, j, 0)),
             pl.BlockSpec((None, d, te), lambda i, j: (layer, 0, j)),
             a_blk, a_blk, b_blk, b_blk, row, tab, vec]
    outs = [jax.ShapeDtypeStruct((t, d), F32)]
    ospecs = [row]
    if pre is not None:
        args += list(pre)
        specs += [vec, tab, tab]
        outs.append(jax.ShapeDtypeStruct((t, d), BF16))
        ospecs.append(row)
    out = pl.pallas_call(
        functools.partial(_peer_ffn_body, n_a, n_split, pre is not None),
        out_shape=tuple(outs), grid=(t // tm, e // te), in_specs=specs, out_specs=tuple(ospecs),
        scratch_shapes=[pltpu.VMEM((d, tm), F32), pltpu.VMEM((te, tm), BF16)],
        compiler_params=_params("parallel", "arbitrary"), name="peer_ffn",
    )(*args)
    return (out[0], out[1]) if pre is not None else (out[0], None)


def _peer(h, wq, keys, u, vt, layer, x, sel, res, pre):
    cnt, e1, rank, e2 = _peer_sel_call(h, wq, keys)
    return _peer_ffn_call(h, u, vt, layer, cnt, e1, rank, e2, x, sel, res, pre)


def kernel(x, c, ctx, c_ctx, ada_w, ada_b, norm_g, w_in, w_out, na_rpb, ssd_conv_w, ssd_conv_b, ssd_dt_bias,
           ssd_a_log, ssd_d, ssd_norm_g, da_lambda, da_sub_g, peer_w_q, peer_sub_keys, peer_u, peer_v):
    batch, seq, d = x.shape
    n_ctx = ctx.shape[1]
    depth = ada_w.shape[0]
    xl = x.reshape(batch * seq, d)
    xc = ctx.reshape(batch * n_ctx, d)
    tm_norm = 256
    sel_lat = lambda i: i // (seq // tm_norm)
    sel_ctx = lambda i: batch
    cos_t, sin_t = _rope_tables(seq)

    o_nq, o_nk, o_nv = 0, 512, 1024
    o_dq, o_dk, o_dv = 1536, 2048, 2560
    o_z = 3072
    o_end = o_z + SSD_INNER + SSD_XBC + 2 * SSD_HEADS
    scale_a = jnp.concatenate([jnp.full((512,), NA_DIM ** -0.5, F32), jnp.ones((1536,), F32)])
    scale_b = jnp.concatenate([jnp.full((512,), DA_DIM ** -0.5, F32), jnp.ones((512,), F32)])

    cond = jnp.concatenate([c, c_ctx[None, :], jnp.zeros((8 - batch - 1, d), F32)], axis=0)
    mods = [_ada_call(cond, ada_w, ada_b, layer)[:batch + 1].reshape(batch + 1, N_MOD, 1, d)
            for layer in range(depth)]
    pu = peer_u.astype(BF16)
    pvt = jnp.swapaxes(peer_v, 1, 2).astype(BF16)
    h_l = h_c = None
    for layer in range(depth):
        need_ctx = layer < depth - 1
        lam_init = 0.8 - 0.6 * math.exp(-0.3 * layer)
        tab = [mods[layer][:, k] for k in range(N_MOD)]
        g = norm_g[layer]
        if layer == 0:
            _, h_l = _norm_call(xl, sel_lat, pre=(g[0:1], tab[0], tab[1]), tm=tm_norm)
            _, h_c = _norm_call(xc, sel_ctx, pre=(g[0:1], tab[0], tab[1]), tm=tm_norm)

        wi = w_in[layer]
        w_a = jnp.concatenate([wi[:, o_nq:o_dq], wi[:, o_dv:o_z]], axis=1).astype(BF16)
        w_b = wi[:, o_dq:o_dv].astype(BF16)
        w_c = jnp.pad(wi[:, o_z:o_end], ((0, 0), (0, SSD_IN_W - (o_end - o_z)))).astype(BF16)
        a_lat = _mm_call(h_l, w_a, BF16, colscale=scale_a, name="proj_attn")
        a_ctx = _mm_call(h_c, w_a, BF16, colscale=scale_a, name="proj_attn")
        qk_lat = _mm_call(h_l, w_b, BF16, colscale=scale_b, rope=(cos_t, sin_t), name="proj_rope")
        qk_ctx = _mm_call(h_c, w_b, BF16, colscale=scale_b, name="proj_qk_ctx")
        s_lat = _mm_call(h_l, w_c, F32, name="proj_ssd")
        s_ctx = _mm_call(h_c, w_c, F32, name="proj_ssd")

        lam_p = da_lambda[layer]
        g_sub = (da_sub_g[layer] * (1.0 - lam_init)).reshape(1, LANES)
        na_l = _na_call(a_lat, a_ctx, _na_bias(na_rpb[layer]), batch, seq, n_ctx)
        da_l = _da_call(qk_lat, qk_ctx, a_lat, a_ctx, lam_p, g_sub, lam_init, batch, seq, n_ctx)

        ux = _ssd_prep_call(s_lat, s_ctx, ssd_conv_w[layer], ssd_conv_b[layer], batch, seq, n_ctx)
        dt0 = SSD_INNER + SSD_XBC
        dt_cols = lambda s, n: s[:, dt0:dt0 + 2 * SSD_HEADS].reshape(batch, n, 2 * SSD_HEADS)
        dtt = jnp.transpose(jnp.concatenate([dt_cols(s_lat, seq), dt_cols(s_ctx, n_ctx)], axis=1), (0, 2, 1))
        ssd_o = _ssd_call(ux, s_lat, s_ctx, dtt, ssd_dt_bias[layer], ssd_a_log[layer], ssd_d[layer],
                          ssd_norm_g[layer], batch, seq, n_ctx)

        wo = w_out[layer].astype(BF16)
        wq = peer_w_q[layer].astype(BF16)
        keys = peer_sub_keys[layer].reshape(2 * PEER_HEADS, PEER_N_KEYS, PEER_QDIM // 2).astype(BF16)
        nxt = layer + 1 < depth

        lat_tiles = seq // TOKEN_TILE
        sel_lat_t = lambda i: i // lat_tiles
        res_mix, pre_ffn, res_ffn = (tab[2], g[1:2]), (g[2:3], tab[3], tab[4]), (tab[5], g[3:4])
        pre_n = (norm_g[layer + 1][0:1], mods[layer + 1][:, 0], mods[layer + 1][:, 1]) if nxt else None
        xl, h2_l = _wout_call(na_l, da_l, ssd_o, wo, xl, sel_lat_t, res_mix, pre_ffn,
                              lambda i: (i // lat_tiles, i % lat_tiles), TOKEN_TILE)
        xl, h_l = _peer(h2_l, wq, keys, pu, pvt, layer, xl, sel_lat_t, res_ffn, pre_n)

        if need_ctx:
            na_c, da_c = _ctx_attn_call(a_ctx, qk_ctx, lam_p, g_sub, lam_init, batch, n_ctx)
            xc, h2_c = _wout_call(na_c, da_c, ssd_o, wo, xc, sel_ctx, res_mix, pre_ffn,
                                  lambda i: (i, seq // n_ctx), n_ctx)
            xc, h_c = _peer(h2_c, wq, keys, pu, pvt, layer, xc, sel_ctx, res_ffn, pre_n)
    return xl.reshape(batch, seq, d)
```

```python
import functools
import math

import numpy as np
import jax
import jax.numpy as jnp
from jax import lax
from jax.experimental import pallas as pl
from jax.experimental.pallas import tpu as pltpu

F32 = jnp.float32
BF16 = jnp.bfloat16
EPS = 1e-6
NEG_INF = float("-inf")

VMEM_LIMIT_BYTES = 58 * 1024 * 1024
LANES = 128

GRID_W = 64
NA_DIM = 64
NA_HEADS = 8
NA_WIN_ROWS = 8
NA_WIN_COLS = 16
DA_DIM = 64
DA_HEADS = 4
ROPE_BASE = 10000.0
SSD_INNER = 1024
SSD_HEAD_DIM = 64
SSD_HEADS = 16
SSD_GROUPS = 2
SSD_STATE = 128
SSD_CONV = 5
SSD_CHUNK = 128
SSD_XBC = SSD_INNER + 2 * SSD_GROUPS * SSD_STATE
SSD_IN_W = SSD_INNER + SSD_XBC + LANES
PEER_HEADS = 8
PEER_QDIM = 256
PEER_N_KEYS = 128
PEER_TOPK = 16
N_MOD = 6
TOKEN_TILE = 512


def _params(*sem):
    return pltpu.CompilerParams(dimension_semantics=sem, vmem_limit_bytes=VMEM_LIMIT_BYTES)


def _nt_dot(a, b):
    return lax.dot_general(a, b, (((1,), (1,)), ((), ())), preferred_element_type=F32)


def _lane_lt64(shape):
    return lax.broadcasted_iota(jnp.int32, shape, len(shape) - 1) < (LANES // 2)


def _ada_body(c_ref, w_ref, b_ref, o_ref):
    c = c_ref[...]
    sc = (c * jax.nn.sigmoid(c)).astype(BF16)
    o_ref[...] = jnp.dot(sc, w_ref[...].astype(BF16), preferred_element_type=F32) + b_ref[...]


def _ada_call(cond, w, b, layer, tn=768):
    m, d = cond.shape
    depth, _, n = w.shape
    return pl.pallas_call(
        _ada_body,
        out_shape=jax.ShapeDtypeStruct((m, n), F32),
        grid=(n // tn,),
        in_specs=[pl.BlockSpec((m, d), lambda j: (0, 0)),
                  pl.BlockSpec((None, d, tn), lambda j: (layer, 0, j)),
                  pl.BlockSpec((None, 1, tn), lambda j: (layer, 0, j))],
        out_specs=pl.BlockSpec((m, tn), lambda j: (0, j)),
        compiler_params=_params("arbitrary"),
        name="ada_mod",
    )(cond, w, b.reshape(depth, 1, n))


def _rms(x, g):
    return x * lax.rsqrt(jnp.mean(x * x, axis=-1, keepdims=True) + EPS) * g


def _norm_body(has_res, has_pre, *refs):
    refs = list(refs)
    x = refs.pop(0)[...]
    if has_res:
        f_ref, gate_ref, gpost_ref = refs.pop(0), refs.pop(0), refs.pop(0)
        x = x + gate_ref[0] * _rms(f_ref[...], gpost_ref[...])
    if has_pre:
        gpre_ref, shift_ref, scale_ref = refs.pop(0), refs.pop(0), refs.pop(0)
    if has_res:
        refs.pop(0)[...] = x
    if has_pre:
        h = _rms(x, gpre_ref[...]) * (1.0 + scale_ref[0]) + shift_ref[0]
        refs.pop(0)[...] = h.astype(BF16)


def _norm_call(x, sel, *, res=None, pre=None, tm=256):
    m, d = x.shape
    row = pl.BlockSpec((tm, d), lambda i: (i, 0))
    vec = pl.BlockSpec((1, d), lambda i: (0, 0))
    tab = pl.BlockSpec((1, 1, d), lambda i: (sel(i), 0, 0))
    args, specs, outs, ospecs = [x], [row], [], []
    if res is not None:
        args += list(res)
        specs += [row, tab, vec]
        outs.append(jax.ShapeDtypeStruct((m, d), F32))
        ospecs.append(row)
    if pre is not None:
        args += list(pre)
        specs += [vec, tab, tab]
        outs.append(jax.ShapeDtypeStruct((m, d), BF16))
        ospecs.append(row)
    res_out = pl.pallas_call(
        functools.partial(_norm_body, res is not None, pre is not None),
        out_shape=tuple(outs), grid=(m // tm,), in_specs=specs, out_specs=tuple(ospecs),
        compiler_params=_params("parallel"), name="norm_mod",
    )(*args)
    res_out = list(res_out)
    x_new = res_out.pop(0) if res is not None else None
    h = res_out.pop(0) if pre is not None else None
    return x_new, h


def _rope_partner(x):
    lane = lax.broadcasted_iota(jnp.int32, x.shape, 1)
    first = (lane % 32) < 16
    return jnp.where(first, pltpu.roll(x, LANES - 16, 1), pltpu.roll(x, 16, 1))


def _mm_body(epi, h_ref, w_ref, *rest):
    o_ref = rest[-1]
    tm = h_ref.shape[0]
    n_part = 2 if (epi == "rope" and tm % 32 == 0) else 1
    part = tm // n_part

    def project(p):
        return jnp.dot(h_ref[p * part:(p + 1) * part, :], w_ref[...], preferred_element_type=F32)

    acc_next = project(0)
    for p in range(n_part):
        acc = acc_next
        if p + 1 < n_part:
            acc_next = project(p + 1)
        rows = slice(p * part, (p + 1) * part)
        if epi == "scale":
            acc = acc * rest[0][...]
        elif epi == "rope":
            cos_ref, sin_ref, cs_ref = rest[:3]
            cos = cos_ref[rows, :]
            sin = sin_ref[rows, :]
            blocks = []
            for c in range(acc.shape[1] // LANES):
                xb = acc[:, c * LANES:(c + 1) * LANES]
                blocks.append(xb * cos + _rope_partner(xb) * sin)
            acc = jnp.concatenate(blocks, axis=1) * cs_ref[...]
        o_ref[rows, :] = acc.astype(o_ref.dtype)


def _mm_call(h, w, out_dtype, *, colscale=None, rope=None, tm=512, tn=None, name="proj"):
    m, k = h.shape
    n = w.shape[1]
    tn = n if tn is None else tn
    args = [h, w]
    specs = [pl.BlockSpec((tm, k), lambda i, j: (i, 0)),
             pl.BlockSpec((k, tn), lambda i, j: (0, j))]
    epi = "plain"
    if rope is not None:
        cos_t, sin_t = rope
        nt = cos_t.shape[0] // tm
        args += [cos_t, sin_t]
        specs += [pl.BlockSpec((tm, LANES), lambda i, j: (i % nt, 0))] * 2
        epi = "rope"
    if colscale is not None:
        args.append(colscale.reshape(1, n))
        specs.append(pl.BlockSpec((1, tn), lambda i, j: (0, j)))
        epi = "scale" if epi == "plain" else epi
    return pl.pallas_call(
        functools.partial(_mm_body, epi),
        out_shape=jax.ShapeDtypeStruct((m, n), out_dtype),
        grid=(m // tm, n // tn), in_specs=specs,
        out_specs=pl.BlockSpec((tm, tn), lambda i, j: (i, j)),
        compiler_params=_params("parallel", "arbitrary"), name=name,
    )(*args)


def _rope_tables(seq):
    quarter = DA_DIM // 4
    inv_freq = 1.0 / (ROPE_BASE ** (jnp.arange(quarter, dtype=F32) / quarter))
    t = jnp.arange(seq)
    ang_r = (t // GRID_W).astype(F32)[:, None] * inv_freq
    ang_c = (t % GRID_W).astype(F32)[:, None] * inv_freq

    def unit(ang):
        return (jnp.concatenate([jnp.cos(ang), jnp.cos(ang)], -1),
                jnp.concatenate([-jnp.sin(ang), jnp.sin(ang)], -1))

    cr, sr = unit(ang_r)
    cc, sc = unit(ang_c)
    cos64 = jnp.concatenate([cr, cc], -1)
    sin64 = jnp.concatenate([sr, sc], -1)
    return jnp.tile(cos64, (1, 2)), jnp.tile(sin64, (1, 2))


def _stack_heads(q2):
    lo = _lane_lt64(q2.shape)
    zero = jnp.zeros_like(q2)
    return jnp.concatenate([jnp.where(lo, q2, zero), jnp.where(lo, zero, q2)], axis=0)


def _unstack_heads(o):
    r = o.shape[0] // 2
    return jnp.where(_lane_lt64((r, o.shape[1])), o[:r], o[r:])


NA_ROW_GROUP = 32


def _na_body(rows, q_ref, k_ref, v_ref, kc_ref, vc_ref, bias_ref, o_ref):
    kc = kc_ref[...]
    vc = vc_ref[...]
    kh = NA_WIN_ROWS

    def scores(r):
        rs = jnp.clip(r - kh // 2, 0, rows - kh)
        q0 = pl.multiple_of(r * GRID_W, GRID_W)
        k0 = pl.multiple_of(rs * GRID_W, GRID_W)
        qs = _stack_heads(q_ref[pl.ds(q0, GRID_W), :])
        s_loc = _nt_dot(qs, k_ref[pl.ds(k0, kh * GRID_W), :]) + bias_ref[0, r - rs]
        return q0, k0, s_loc, _nt_dot(qs, kc)

    def finish(q0, k0, p_loc, p_ctx, l):
        o = (jnp.dot(p_loc, v_ref[pl.ds(k0, kh * GRID_W), :], preferred_element_type=F32)
             + jnp.dot(p_ctx, vc, preferred_element_type=F32)) / l
        o_ref[pl.ds(q0, GRID_W), :] = _unstack_heads(o).astype(o_ref.dtype)

    group = math.gcd(rows, NA_ROW_GROUP)

    def row_group(gi, carry):
        nxt = scores(gi * group)
        pending = None
        for j in range(group):
            q0, k0, s_loc, s_ctx = nxt
            if j + 1 < group:
                nxt = scores(gi * group + j + 1)
            m = jnp.maximum(jnp.max(s_loc, axis=-1, keepdims=True), jnp.max(s_ctx, axis=-1, keepdims=True))
            p_loc = jnp.exp(s_loc - m)
            p_ctx = jnp.exp(s_ctx - m)
            l = jnp.sum(p_loc, axis=-1, keepdims=True) + jnp.sum(p_ctx, axis=-1, keepdims=True)
            if pending is not None:
                finish(*pending)
            pending = (q0, k0, p_loc.astype(BF16), p_ctx.astype(BF16), l)
        finish(*pending)
        return carry

    lax.fori_loop(0, rows // group, row_group, 0)


def _na_bias(rpb):
    kh, w = NA_WIN_ROWS, GRID_W
    c = np.arange(w)
    wstart = np.clip(c - NA_WIN_COLS // 2, 0, w - NA_WIN_COLS)
    kc = np.arange(w)
    valid = (kc[None, :] >= wstart[:, None]) & (kc[None, :] < wstart[:, None] + NA_WIN_COLS)
    cidx = np.clip(kc[None, :] - c[:, None] + NA_WIN_COLS - 1, 0, 2 * NA_WIN_COLS - 2)
    typ = np.arange(kh)
    ridx = np.arange(kh)[None, :] - typ[:, None] + NA_WIN_ROWS - 1
    pick_r = (ridx[:, :, None] == np.arange(2 * NA_WIN_ROWS - 1)).astype(np.float32)
    pick_c = ((cidx[:, :, None] == np.arange(2 * NA_WIN_COLS - 1)) & valid[:, :, None]).astype(np.float32)
    b = jnp.einsum('hrc,tjr,qkc->htqjk', rpb.astype(F32), pick_r, pick_c, precision=lax.Precision.HIGHEST)
    b = b + np.where(valid, 0.0, NEG_INF).astype(np.float32)[None, None, :, None, :]
    b = b.reshape(NA_HEADS // 2, 2, kh, w, kh * w)
    return jnp.transpose(b, (0, 2, 1, 3, 4)).reshape(NA_HEADS // 2, kh, 2 * w, kh * w).astype(F32)


def _na_call(a_lat, a_ctx, bias, batch, seq, n_ctx):
    rows = seq // GRID_W
    pairs = NA_HEADS // 2
    lat = lambda c0: pl.BlockSpec((seq, LANES), lambda b, p: (b, c0 + p))
    ctx = lambda c0: pl.BlockSpec((n_ctx, LANES), lambda b, p: (b, c0 + p))
    return pl.pallas_call(
        functools.partial(_na_body, rows),
        out_shape=jax.ShapeDtypeStruct((batch * seq, NA_HEADS * NA_DIM), BF16),
        grid=(batch, pairs),
        in_specs=[lat(0), lat(pairs), lat(2 * pairs), ctx(pairs), ctx(2 * pairs),
                  pl.BlockSpec((1,) + bias.shape[1:], lambda b, p: (p, 0, 0, 0))],
        out_specs=pl.BlockSpec((seq, LANES), lambda b, p: (b, p)),
        compiler_params=_params("parallel", "parallel"), name="na_attn",
    )(a_lat, a_lat, a_lat, a_ctx, a_ctx, bias)


def _da_lambda(lam_ref, lam_init):
    lf = lam_ref[...]
    a = jnp.sum(lf[0:1] * lf[1:2], axis=-1, keepdims=True)
    b = jnp.sum(lf[2:3] * lf[3:4], axis=-1, keepdims=True)
    return jnp.exp(a) - jnp.exp(b) + lam_init


def _da_finish(acc, l, lam, g):
    tq = acc.shape[0] // 2
    o = acc / l
    d = o[:tq] - lam * o[tq:]
    return d * lax.rsqrt(jnp.mean(d * d, axis=-1, keepdims=True) + EPS) * g


def _flash_step(qs, k, v, m, l, acc):
    s = _nt_dot(qs, k)
    m_new = jnp.maximum(m, jnp.max(s, axis=-1, keepdims=True))
    alpha = jnp.exp(m - m_new)
    p = jnp.exp(s - m_new)
    l = alpha * l + jnp.sum(p, axis=-1, keepdims=True)
    acc = alpha * acc + jnp.dot(p.astype(BF16), v, preferred_element_type=F32)
    return m_new, l, acc


def _da_body(lam_init, n_chunks, tk, q_ref, k_ref, v_ref, kc_ref, vc_ref, lam_ref, g_ref, o_ref):
    qs = _stack_heads(q_ref[...])
    r = qs.shape[0]
    chunks = [(k_ref, v_ref, c * tk, tk) for c in range(n_chunks)] + [(kc_ref, vc_ref, 0, kc_ref.shape[0])]

    def scores(i):
        kr, _, k0, n = chunks[i]
        return _nt_dot(qs, kr[k0:k0 + n, :])

    m = jnp.full((r, 1), NEG_INF, F32)
    l = jnp.zeros((r, 1), F32)
    acc = jnp.zeros((r, LANES), F32)
    s_next = scores(0)
    pending = None
    for i in range(len(chunks)):
        s = s_next
        if i + 1 < len(chunks):
            s_next = scores(i + 1)
        m_new = jnp.maximum(m, jnp.max(s, axis=-1, keepdims=True))
        alpha = jnp.exp(m - m_new)
        p = jnp.exp(s - m_new)
        l = alpha * l + jnp.sum(p, axis=-1, keepdims=True)
        m = m_new
        if pending is not None:
            p_prev, alpha_prev, (_, vr, k0, n) = pending
            acc = alpha_prev * acc + jnp.dot(p_prev, vr[k0:k0 + n, :], preferred_element_type=F32)
        pending = (p.astype(BF16), alpha, chunks[i])
    p_prev, alpha_prev, (_, vr, k0, n) = pending
    acc = alpha_prev * acc + jnp.dot(p_prev, vr[k0:k0 + n, :], preferred_element_type=F32)
    o_ref[...] = _da_finish(acc, l, _da_lambda(lam_ref, lam_init), g_ref[...]).astype(o_ref.dtype)


def _da_call(qk_lat, qk_ctx, a_lat, a_ctx, lam_p, g_scaled, lam_init, batch, seq, n_ctx, tq=256, tk=512):
    nq = seq // tq
    h = DA_HEADS
    return pl.pallas_call(
        functools.partial(_da_body, lam_init, seq // tk, tk),
        out_shape=jax.ShapeDtypeStruct((batch * seq, h * 2 * DA_DIM), BF16),
        grid=(batch, h, nq),
        in_specs=[pl.BlockSpec((tq, LANES), lambda b, hh, i: (b * nq + i, hh)),
                  pl.BlockSpec((seq, LANES), lambda b, hh, i: (b, h + hh)),
                  pl.BlockSpec((seq, LANES), lambda b, hh, i: (b, 3 * h + hh)),
                  pl.BlockSpec((n_ctx, LANES), lambda b, hh, i: (b, h + hh)),
                  pl.BlockSpec((n_ctx, LANES), lambda b, hh, i: (b, 3 * h + hh)),
                  pl.BlockSpec(lam_p.shape, lambda b, hh, i: (0, 0)),
                  pl.BlockSpec((1, LANES), lambda b, hh, i: (0, 0))],
        out_specs=pl.BlockSpec((tq, LANES), lambda b, hh, i: (b * nq + i, hh)),
        compiler_params=_params("parallel", "parallel", "arbitrary"), name="da_attn",
    )(qk_lat, qk_lat, a_lat, qk_ctx, a_ctx, lam_p, g_scaled)


def _ctx_attn_body(lam_init, a_ref, qk_ref, lam_ref, g_ref, na_ref, da_ref):
    pairs = NA_HEADS // 2
    for p in range(pairs):
        qs = _stack_heads(a_ref[:, p * LANES:(p + 1) * LANES])
        k = a_ref[:, (pairs + p) * LANES:(pairs + p + 1) * LANES]
        v = a_ref[:, (2 * pairs + p) * LANES:(2 * pairs + p + 1) * LANES]
        s = _nt_dot(qs, k)
        e = jnp.exp(s - jnp.max(s, axis=-1, keepdims=True))
        o = jnp.dot(e.astype(BF16), v, preferred_element_type=F32) / jnp.sum(e, axis=-1, keepdims=True)
        na_ref[:, p * LANES:(p + 1) * LANES] = _unstack_heads(o).astype(na_ref.dtype)
    lam = _da_lambda(lam_ref, lam_init)
    for hh in range(DA_HEADS):
        qs = _stack_heads(qk_ref[:, hh * LANES:(hh + 1) * LANES])
        k = qk_ref[:, (DA_HEADS + hh) * LANES:(DA_HEADS + hh + 1) * LANES]
        v = a_ref[:, (3 * pairs + hh) * LANES:(3 * pairs + hh + 1) * LANES]
        r = qs.shape[0]
        init = (jnp.full((r, 1), NEG_INF, F32), jnp.zeros((r, 1), F32), jnp.zeros((r, LANES), F32))
        m, l, acc = _flash_step(qs, k, v, *init)
        da_ref[:, hh * LANES:(hh + 1) * LANES] = _da_finish(acc, l, lam, g_ref[...]).astype(da_ref.dtype)


def _ctx_attn_call(a_ctx, qk_ctx, lam_p, g_scaled, lam_init, batch, n_ctx):
    return pl.pallas_call(
        functools.partial(_ctx_attn_body, lam_init),
        out_shape=(jax.ShapeDtypeStruct((batch * n_ctx, NA_HEADS * NA_DIM), BF16),
                   jax.ShapeDtypeStruct((batch * n_ctx, DA_HEADS * 2 * DA_DIM), BF16)),
        grid=(batch,),
        in_specs=[pl.BlockSpec((n_ctx, a_ctx.shape[1]), lambda b: (b, 0)),
                  pl.BlockSpec((n_ctx, qk_ctx.shape[1]), lambda b: (b, 0)),
                  pl.BlockSpec(lam_p.shape, lambda b: (0, 0)),
                  pl.BlockSpec((1, LANES), lambda b: (0, 0))],
        out_specs=(pl.BlockSpec((n_ctx, NA_HEADS * NA_DIM), lambda b: (b, 0)),
                   pl.BlockSpec((n_ctx, DA_HEADS * 2 * DA_DIM), lambda b: (b, 0))),
        compiler_params=_params("parallel"), name="ctx_attn",
    )(a_ctx, qk_ctx, lam_p, g_scaled)


def _conv_seg(u, w, bias):
    n = u.shape[0]
    t = lax.broadcasted_iota(jnp.int32, u.shape, 0)
    acc = jnp.zeros_like(u) + bias
    for kk in range(SSD_CONV):
        d = kk - SSD_CONV // 2
        if d == 0:
            sh = u
        else:
            sh = pltpu.roll(u, (-d) % n, 0)
            sh = jnp.where((t + d >= 0) & (t + d < n), sh, 0.0)
        acc = acc + sh * w[kk:kk + 1, :]
    return acc * jax.nn.sigmoid(acc)


def _ssd_prep_body(seq, lat_ref, ctx_ref, w_ref, b_ref, o_ref):
    w = w_ref[...]
    bias = b_ref[...]
    o_ref[0, :seq, :] = _conv_seg(lat_ref[...], w, bias)
    o_ref[0, seq:, :] = _conv_seg(ctx_ref[...], w, bias)


def _ssd_prep_call(s_lat, s_ctx, conv_w, conv_b, batch, seq, n_ctx):
    n_zc = SSD_INNER // LANES
    return pl.pallas_call(
        functools.partial(_ssd_prep_body, seq),
        out_shape=jax.ShapeDtypeStruct((batch, seq + n_ctx, SSD_XBC), F32),
        grid=(batch, SSD_XBC // LANES),
        in_specs=[pl.BlockSpec((seq, LANES), lambda b, j: (b, n_zc + j)),
                  pl.BlockSpec((n_ctx, LANES), lambda b, j: (b, n_zc + j)),
                  pl.BlockSpec((SSD_CONV, LANES), lambda b, j: (0, j)),
                  pl.BlockSpec((1, LANES), lambda b, j: (0, j))],
        out_specs=pl.BlockSpec((1, seq + n_ctx, LANES), lambda b, j: (b, 0, j)),
        compiler_params=_params("parallel", "parallel"), name="ssd_prep",
    )(s_lat, s_ctx, conv_w, conv_b.reshape(1, -1))


def _softplus(x):
    return jnp.maximum(x, 0.0) + jnp.log1p(jnp.exp(-jnp.abs(x)))


def _ssd_chunk(direction, xs_ref, bc_ref, dt_raw, dtt_ref, pr_ref, pc_ref, dexp_ref, s_ref):
    q = SSD_CHUNK
    nh = SSD_HEADS
    hi = lax.Precision.HIGHEST
    d0 = direction * nh
    dt = _softplus(dt_raw[:, d0:d0 + nh] + pr_ref[0:1, :])
    a = dt * (-jnp.exp(pr_ref[1:2, :]))
    dtt = _softplus(dtt_ref[0][d0:d0 + nh, :] + pc_ref[:, 0:1])
    at = dtt * (-jnp.exp(pc_ref[:, 1:2]))
    ii = lax.broadcasted_iota(jnp.int32, (q, q), 0)
    jj = lax.broadcasted_iota(jnp.int32, (q, q), 1)
    tri = (jj <= ii) if direction == 0 else (jj >= ii)
    tri_f = tri.astype(F32)
    cs = jnp.dot(tri_f, a, precision=hi, preferred_element_type=F32)
    cst = lax.dot_general(at, tri_f, (((1,), (1,)), ((), ())), precision=hi,
                          preferred_element_type=F32)
    edge = q - 1 if direction == 0 else 0
    tot_t = cst[:, edge:edge + 1]
    dst = jnp.exp(tot_t - cst) * dtt
    cst_dt = cst - jnp.log(dtt)
    etot = jnp.exp(cs[edge:edge + 1, :])

    lo = _lane_lt64((q, LANES))
    groups = []
    for g in range(SSD_GROUPS):
        b_g = bc_ref[0][:, g * SSD_STATE:(g + 1) * SSD_STATE]
        c_g = bc_ref[0][:, (SSD_GROUPS + g) * SSD_STATE:(SSD_GROUPS + g + 1) * SSD_STATE]
        groups.append((c_g, _nt_dot(c_g.astype(BF16), b_g.astype(BF16)), b_g.T))
    ys = []
    for pair in range(nh // 2):
        c_g, cb, bt_g = groups[(2 * pair) // (nh // SSD_GROUPS)]
        xs_p = xs_ref[0][:, pair * LANES:(pair + 1) * LANES]
        r0, r1 = 2 * pair, 2 * pair + 1
        xb = xs_p.astype(BF16)
        zero_b = jnp.zeros_like(xb)
        xb_bd = jnp.concatenate([jnp.where(lo, xb, zero_b), jnp.where(lo, zero_b, xb)], axis=0)
        s_p = s_ref[pair]
        s_b = s_p.astype(BF16)
        s_bd = jnp.concatenate([jnp.where(lo, s_b, zero_b), jnp.where(lo, zero_b, s_b)], axis=0)
        lhs, lhs_state, bts = [], [], []
        for r in (r0, r1):
            colb = jnp.broadcast_to(cs[:, r:r + 1], (q, q))
            rowb = jnp.broadcast_to(cst_dt[r:r + 1, :], (q, q))
            decay = jnp.exp(jnp.where(tri, colb - rowb, NEG_INF))
            lhs.append((cb * decay).astype(BF16))
            lhs_state.append((c_g * jnp.exp(colb)).astype(BF16))
            bts.append((bt_g * jnp.broadcast_to(dst[r:r + 1, :], (SSD_STATE, q))).astype(BF16))
        y = jnp.dot(jnp.concatenate(lhs + lhs_state, axis=1), jnp.concatenate([xb_bd, s_bd], axis=0),
                    preferred_element_type=F32)
        y = y + xs_p * dexp_ref[:, pair * LANES:(pair + 1) * LANES]
        dec_p = jnp.where(_lane_lt64((1, LANES)), jnp.broadcast_to(etot[:, r0:r0 + 1], (1, LANES)),
                          jnp.broadcast_to(etot[:, r1:r1 + 1], (1, LANES)))
        s_ref[pair] = s_p * dec_p + jnp.dot(jnp.concatenate(bts, axis=1), xb_bd, preferred_element_type=F32)
        ys.append(y)
    return jnp.concatenate(ys, axis=1)


def _ssd_fwd_body(n_c, xs_ref, bc_ref, dtl_ref, dtc_ref, dtt_ref, pr_ref, pc_ref, dexp_ref, y_ref, s_ref):
    step = pl.program_id(1)

    @pl.when(step == 0)
    def _():
        s_ref[...] = jnp.zeros_like(s_ref)

    dt_raw = jnp.where(step < n_c, dtc_ref[...], dtl_ref[...])
    y_ref[0] = _ssd_chunk(0, xs_ref, bc_ref, dt_raw, dtt_ref, pr_ref, pc_ref, dexp_ref, s_ref)


def _ssd_bwd_body(n_c, xs_ref, bc_ref, dtl_ref, dtc_ref, dtt_ref, pr_ref, pc_ref, dexp_ref, yf_ref, zl_ref, zc_ref,
                  g_ref, o_ref, s_ref):
    step = pl.program_id(1)

    @pl.when(step == 0)
    def _():
        s_ref[...] = jnp.zeros_like(s_ref)

    is_ctx = step < n_c
    dt_raw = jnp.where(is_ctx, dtc_ref[...], dtl_ref[...])
    y = _ssd_chunk(1, xs_ref, bc_ref, dt_raw, dtt_ref, pr_ref, pc_ref, dexp_ref, s_ref) + yf_ref[0]
    z = jnp.where(is_ctx, zc_ref[...], zl_ref[...])
    gated = y * (z * jax.nn.sigmoid(z))
    gw = SSD_INNER // SSD_GROUPS
    parts = []
    for g in range(SSD_GROUPS):
        gg = gated[:, g * gw:(g + 1) * gw]
        parts.append(gg * lax.rsqrt(jnp.mean(gg * gg, axis=-1, keepdims=True) + EPS))
    o_ref[0] = (jnp.concatenate(parts, axis=1) * g_ref[...]).astype(o_ref.dtype)


def _ssd_call(ux, s_lat, s_ctx, dtt, dt_bias, a_log, d_skip, norm_g, batch, seq, n_ctx):
    q = SSD_CHUNK
    n_lat, n_c = seq // q, n_ctx // q
    n_ch = n_lat + n_c
    nh = SSD_HEADS
    fwd_chunk = lambda s: jnp.where(s < n_c, n_lat + s, s - n_c)
    bwd_chunk = lambda s: n_ch - 1 - s
    bc_blk = SSD_INNER // (2 * SSD_GROUPS * SSD_STATE)
    dt_blk = (SSD_INNER + SSD_XBC) // LANES
    dexp = [jnp.repeat(d_skip[d], SSD_HEAD_DIM).reshape(1, SSD_INNER) for d in range(2)]
    prow = [jnp.stack([dt_bias[d], a_log[d]], axis=0) for d in range(2)]
    pcol = [jnp.stack([dt_bias[d], a_log[d]], axis=1) for d in range(2)]
    lat_row = lambda chunk: (lambda b, s: b * n_lat + jnp.clip(chunk(s), 0, n_lat - 1))
    ctx_row = lambda chunk: (lambda b, s: b * n_c + jnp.clip(chunk(s) - n_lat, 0, n_c - 1))

    def specs(chunk):
        lr, cr = lat_row(chunk), ctx_row(chunk)
        return [pl.BlockSpec((1, q, SSD_INNER), lambda b, s: (b, chunk(s), 0)),
                pl.BlockSpec((1, q, 2 * SSD_GROUPS * SSD_STATE), lambda b, s: (b, chunk(s), bc_blk)),
                pl.BlockSpec((q, LANES), lambda b, s: (lr(b, s), dt_blk)),
                pl.BlockSpec((q, LANES), lambda b, s: (cr(b, s), dt_blk)),
                pl.BlockSpec((1, 2 * nh, q), lambda b, s: (b, 0, chunk(s))),
                pl.BlockSpec((2, nh), lambda b, s: (0, 0)),
                pl.BlockSpec((nh, 2), lambda b, s: (0, 0)),
                pl.BlockSpec((1, SSD_INNER), lambda b, s: (0, 0))]

    state = pltpu.VMEM((nh // 2, SSD_STATE, LANES), F32)
    yblk = lambda chunk: pl.BlockSpec((1, q, SSD_INNER), lambda b, s: (b, chunk(s), 0))
    y_f = pl.pallas_call(
        functools.partial(_ssd_fwd_body, n_c),
        out_shape=jax.ShapeDtypeStruct((batch, seq + n_ctx, SSD_INNER), F32),
        grid=(batch, n_ch), in_specs=specs(fwd_chunk), out_specs=yblk(fwd_chunk),
        scratch_shapes=[state], compiler_params=_params("parallel", "arbitrary"), name="ssd_fwd",
    )(ux, ux, s_lat, s_ctx, dtt, prow[0], pcol[0], dexp[0])
    lr, cr = lat_row(bwd_chunk), ctx_row(bwd_chunk)
    return pl.pallas_call(
        functools.partial(_ssd_bwd_body, n_c),
        out_shape=jax.ShapeDtypeStruct((batch, seq + n_ctx, SSD_INNER), BF16),
        grid=(batch, n_ch),
        in_specs=specs(bwd_chunk) + [yblk(bwd_chunk),
                                     pl.BlockSpec((q, SSD_INNER), lambda b, s: (lr(b, s), 0)),
                                     pl.BlockSpec((q, SSD_INNER), lambda b, s: (cr(b, s), 0)),
                                     pl.BlockSpec((1, SSD_INNER), lambda b, s: (0, 0))],
        out_specs=yblk(bwd_chunk),
        scratch_shapes=[state], compiler_params=_params("parallel", "arbitrary"), name="ssd_bwd",
    )(ux, ux, s_lat, s_ctx, dtt, prow[1], pcol[1], dexp[1], y_f, s_lat, s_ctx, norm_g.reshape(1, SSD_INNER))


def _residual_norm(x, f, gate, g_post, pre):
    x = x + gate * _rms(f, g_post)
    if pre is None:
        return x, None
    g_pre, shift, scale = pre
    return x, (_rms(x, g_pre) * (1.0 + scale) + shift).astype(BF16)


def _wout_body(na_ref, da_ref, ssd_ref, w_ref, x_ref, gate_ref, gpost_ref, gpre_ref, shift_ref, scale_ref,
               xo_ref, h_ref):
    k1 = na_ref.shape[1]
    k2 = k1 + da_ref.shape[1]
    tm = na_ref.shape[0]
    n_part = 2 if tm % 32 == 0 else 1
    part = tm // n_part

    def project(r0):
        rows = slice(r0, r0 + part)
        acc = jnp.dot(na_ref[rows, :], w_ref[:k1, :], preferred_element_type=F32)
        acc = acc + jnp.dot(da_ref[rows, :], w_ref[k1:k2, :], preferred_element_type=F32)
        return acc + jnp.dot(ssd_ref[0, rows, :], w_ref[k2:, :], preferred_element_type=F32)

    acc_next = project(0)
    for p in range(n_part):
        acc = acc_next
        if p + 1 < n_part:
            acc_next = project((p + 1) * part)
        rows = slice(p * part, (p + 1) * part)
        xo_ref[rows, :], h_ref[rows, :] = _residual_norm(x_ref[rows, :], acc, gate_ref[0], gpost_ref[...],
                                                         (gpre_ref[...], shift_ref[0], scale_ref[0]))


def _wout_call(na, da, ssd, w, x, sel, res, pre, ssd_tile, tm):
    m = na.shape[0]
    d = w.shape[1]
    row = pl.BlockSpec((tm, d), lambda i: (i, 0))
    vec = pl.BlockSpec((1, d), lambda i: (0, 0))
    tab = pl.BlockSpec((1, 1, d), lambda i: (sel(i), 0, 0))
    return pl.pallas_call(
        _wout_body,
        out_shape=(jax.ShapeDtypeStruct((m, d), F32), jax.ShapeDtypeStruct((m, d), BF16)),
        grid=(m // tm,),
        in_specs=[pl.BlockSpec((tm, na.shape[1]), lambda i: (i, 0)),
                  pl.BlockSpec((tm, da.shape[1]), lambda i: (i, 0)),
                  pl.BlockSpec((1, tm, ssd.shape[2]), lambda i: ssd_tile(i) + (0,)),
                  pl.BlockSpec(w.shape, lambda i: (0, 0)),
                  row, tab, vec, vec, tab, tab],
        out_specs=(row, row),
        compiler_params=_params("parallel"), name="w_out",
    )(na, da, ssd, w, x, res[0], res[1], *pre)


def _batcher_pairs(n):
    pairs = []
    p = 1
    while p < n:
        k = p
        while k >= 1:
            for j in range(k % p, n - k, 2 * k):
                for i in range(min(k, n - j - k)):
                    if (i + j) // (2 * p) == (i + j + k) // (2 * p):
                        pairs.append((i + j, i + j + k))
            k //= 2
        p *= 2
    return pairs


def _top_vals_network(x, k):
    sub = 8
    tiles = [x[v * sub:(v + 1) * sub, :] for v in range(k)]

    def exchange(i, j):
        hi, lo = jnp.maximum(tiles[i], tiles[j]), jnp.minimum(tiles[i], tiles[j])
        tiles[i], tiles[j] = hi, lo

    for i, j in _batcher_pairs(k):
        exchange(i, j)
    shift = sub // 2
    while shift >= 1:
        other = [pltpu.roll(t, shift, 0) for t in tiles]
        tiles = [jnp.maximum(tiles[i], other[k - 1 - i]) for i in range(k)]
        d = k // 2
        while d >= 1:
            for i in range(k):
                if i & d == 0:
                    exchange(i, i + d)
            d //= 2
        shift //= 2
    return jnp.concatenate([t[0:1, :] for t in tiles], axis=0)


def _pair_candidates(v1, v2):
    k = PEER_TOPK
    half = k // 2
    row = lax.broadcasted_iota(jnp.int32, (half, v1.shape[1]), 0)
    parts = [v2 + v1[0:1, :]]
    for i in range(1, half):
        parts.append(jnp.where(row < k // (i + 1), v2[:half] + v1[i:i + 1, :], NEG_INF))
    parts.append(v1[half:] + v2[0:1, :])
    return jnp.concatenate(parts, axis=0)


def _peer_sel_body(h_ref, wq_ref, keys_ref, cnt_ref, e1_ref, rank_ref, e2_ref):
    qd = PEER_QDIM // 2
    k = PEER_TOPK
    qv = jnp.dot(h_ref[...], wq_ref[...], preferred_element_type=F32).astype(BF16)
    for hh in range(PEER_HEADS):
        c0 = hh * PEER_QDIM
        s1 = _nt_dot(keys_ref[2 * hh], qv[:, c0:c0 + qd])
        s2 = _nt_dot(keys_ref[2 * hh + 1], qv[:, c0 + qd:c0 + 2 * qd])
        v1 = _top_vals_network(s1, k)
        v2 = _top_vals_network(s2, k)
        rank = jnp.zeros_like(s2)
        for r in range(k):
            rank = jnp.where(v2[r:r + 1, :] > s2, float(r + 1), rank)
        cand = _pair_candidates(v1, v2)
        pad = jnp.full((8 * k - cand.shape[0], cand.shape[1]), NEG_INF, F32)
        top = _top_vals_network(jnp.concatenate([cand, pad], axis=0), k)
        thr = top[k - 1:k, :]
        z = jnp.sum(jnp.exp(top - top[0:1, :]), axis=0, keepdims=True)
        cnt_sorted = jnp.zeros_like(v1)
        for r in range(k):
            cnt_sorted = cnt_sorted + jnp.where(v1 + v2[r:r + 1, :] >= thr, 1.0, 0.0)
        cnt = jnp.zeros_like(s1)
        for r in range(k):
            cnt = jnp.where(s1 == v1[r:r + 1, :], cnt_sorted[r:r + 1, :], cnt)
        cnt_ref[hh] = cnt
        rank_ref[hh] = rank.astype(BF16)
        e1_ref[hh] = jnp.exp(s1 - v1[0:1, :])
        e2_ref[hh] = (jnp.exp(s2 - v2[0:1, :]) / z).astype(BF16)


def _peer_sel_call(h, wq, keys, tm=256):
    t, d = h.shape
    shape = (PEER_HEADS, PEER_N_KEYS, t)
    sblk = pl.BlockSpec((PEER_HEADS, PEER_N_KEYS, tm), lambda i: (0, 0, i))
    return pl.pallas_call(
        _peer_sel_body,
        out_shape=(jax.ShapeDtypeStruct(shape, F32), jax.ShapeDtypeStruct(shape, F32),
                   jax.ShapeDtypeStruct(shape, BF16), jax.ShapeDtypeStruct(shape, BF16)),
        grid=(t // tm,),
        in_specs=[pl.BlockSpec((tm, d), lambda i: (i, 0)),
                  pl.BlockSpec(wq.shape, lambda i: (0, 0)),
                  pl.BlockSpec(keys.shape, lambda i: (0, 0, 0))],
        out_specs=(sblk, sblk, sblk, sblk),
        compiler_params=_params("parallel"), name="peer_select",
    )(h, wq, keys)


def _gelu_tanh(x):
    return 0.5 * x * (1.0 + jnp.tanh(math.sqrt(2.0 / math.pi) * (x + 0.044715 * (x * x * x))))


def _peer_ffn_body(n_a, n_split, has_pre, h_ref, u_ref, vt_ref, cnt_ref, e1_ref, rank_ref, e2_ref,
                   x_ref, gate_ref, gpost_ref, *rest):
    if has_pre:
        gpre_ref, shift_ref, scale_ref, xo_ref, ho_ref, acc_ref, wt_ref = rest
    else:
        xo_ref, acc_ref, wt_ref = rest
    j = pl.program_id(1)

    @pl.when(j == 0)
    def _():
        acc_ref[...] = jnp.zeros_like(acc_ref)

    nk = PEER_N_KEYS
    a_part = n_a // n_split
    rows = a_part * nk
    h = h_ref[...]

    def act(s):
        return _gelu_tanh(_nt_dot(u_ref[s * rows:(s + 1) * rows, :], h)).astype(BF16)

    act_next = act(0)
    for s in range(n_split):
        act_s = act_next
        if s + 1 < n_split:
            act_next = act(s + 1)
        for al in range(a_part):
            a = s * a_part + al
            gate = None
            for hh in range(PEER_HEADS):
                cnt = cnt_ref[hh, a:a + 1, :].astype(BF16)
                e1 = e1_ref[hh, a:a + 1, :].astype(BF16)
                w = jnp.where(rank_ref[hh] < cnt, e2_ref[hh], jnp.zeros((), BF16)) * e1
                gate = w if gate is None else gate + w
            wt_ref[a * nk:(a + 1) * nk, :] = gate * act_s[al * nk:(al + 1) * nk, :]
        acc_ref[...] += jnp.dot(vt_ref[:, s * rows:(s + 1) * rows], wt_ref[s * rows:(s + 1) * rows, :],
                                preferred_element_type=F32)

    @pl.when(j == pl.num_programs(1) - 1)
    def _():
        pre = (gpre_ref[...], shift_ref[0], scale_ref[0]) if has_pre else None
        x_new, h_next = _residual_norm(x_ref[...], acc_ref[...].T, gate_ref[0], gpost_ref[...], pre)
        xo_ref[...] = x_new
        if has_pre:
            ho_ref[...] = h_next


def _peer_ffn_call(h, u, vt, layer, cnt, e1, rank, e2, x, sel, res, pre, tm=TOKEN_TILE, n_a=8, n_split=2):
    t, d = h.shape
    e = u.shape[1]
    te = n_a * PEER_N_KEYS
    a_blk = pl.BlockSpec((PEER_HEADS, n_a, tm), lambda i, j: (0, j, i))
    b_blk = pl.BlockSpec((PEER_HEADS, PEER_N_KEYS, tm), lambda i, j: (0, 0, i))
    row = pl.BlockSpec((tm, d), lambda i, j: (i, 0))
    vec = pl.BlockSpec((1, d), lambda i, j: (0, 0))
    tab = pl.BlockSpec((1, 1, d), lambda i, j: (sel(i), 0, 0))
    args = [h, u, vt, cnt, e1, rank, e2, x, res[0], res[1]]
    specs = [row, pl.BlockSpec((None, te, d), lambda i, j: (layer, j, 0)),
             pl.BlockSpec((None, d, te), lambda i, j: (layer, 0, j)),
             a_blk, a_blk, b_blk, b_blk, row, tab, vec]
    outs = [jax.ShapeDtypeStruct((t, d), F32)]
    ospecs = [row]
    if pre is not None:
        args += list(pre)
        specs += [vec, tab, tab]
        outs.append(jax.ShapeDtypeStruct((t, d), BF16))
        ospecs.append(row)
    out = pl.pallas_call(
        functools.partial(_peer_ffn_body, n_a, n_split, pre is not None),
        out_shape=tuple(outs), grid=(t // tm, e // te), in_specs=specs, out_specs=tuple(ospecs),
        scratch_shapes=[pltpu.VMEM((d, tm), F32), pltpu.VMEM((te, tm), BF16)],
        compiler_params=_params("parallel", "arbitrary"), name="peer_ffn",
    )(*args)
    return (out[0], out[1]) if pre is not None else (out[0], None)


def _transpose_cast_body(x_ref, o_ref):
    o_ref[...] = x_ref[...].T.astype(o_ref.dtype)


def _transpose_cast_call(w, tr=512):
    depth, rows, cols = w.shape
    return pl.pallas_call(
        _transpose_cast_body,
        out_shape=jax.ShapeDtypeStruct((depth, cols, rows), BF16),
        grid=(depth, rows // tr),
        in_specs=[pl.BlockSpec((None, tr, cols), lambda l, i: (l, i, 0))],
        out_specs=pl.BlockSpec((None, cols, tr), lambda l, i: (l, 0, i)),
        compiler_params=_params("parallel", "parallel"), name="transpose_cast",
    )(w)


def _peer(h, wq, keys, u, vt, layer, x, sel, res, pre):
    cnt, e1, rank, e2 = _peer_sel_call(h, wq, keys)
    return _peer_ffn_call(h, u, vt, layer, cnt, e1, rank, e2, x, sel, res, pre)


def kernel(x, c, ctx, c_ctx, ada_w, ada_b, norm_g, w_in, w_out, na_rpb, ssd_conv_w, ssd_conv_b, ssd_dt_bias,
           ssd_a_log, ssd_d, ssd_norm_g, da_lambda, da_sub_g, peer_w_q, peer_sub_keys, peer_u, peer_v):
    batch, seq, d = x.shape
    n_ctx = ctx.shape[1]
    depth = ada_w.shape[0]
    xl = x.reshape(batch * seq, d)
    xc = ctx.reshape(batch * n_ctx, d)
    tm_norm = TOKEN_TILE
    sel_lat = lambda i: i // (seq // tm_norm)
    sel_ctx = lambda i: batch
    cos_t, sin_t = _rope_tables(seq)

    o_nq, o_nk, o_nv = 0, 512, 1024
    o_dq, o_dk, o_dv = 1536, 2048, 2560
    o_z = 3072
    o_end = o_z + SSD_INNER + SSD_XBC + 2 * SSD_HEADS
    scale_a = jnp.concatenate([jnp.full((512,), NA_DIM ** -0.5, F32), jnp.ones((1536,), F32)])
    scale_b = jnp.concatenate([jnp.full((512,), DA_DIM ** -0.5, F32), jnp.ones((512,), F32)])

    cond = jnp.concatenate([c, c_ctx[None, :], jnp.zeros((8 - batch - 1, d), F32)], axis=0)
    mods = [_ada_call(cond, ada_w, ada_b, layer)[:batch + 1].reshape(batch + 1, N_MOD, 1, d)
            for layer in range(depth)]
    pu = peer_u.astype(BF16)
    pvt = _transpose_cast_call(peer_v)
    h_l = h_c = None
    for layer in range(depth):
        need_ctx = layer < depth - 1
        lam_init = 0.8 - 0.6 * math.exp(-0.3 * layer)
        tab = [mods[layer][:, k] for k in range(N_MOD)]
        g = norm_g[layer]
        if layer == 0:
            _, h_l = _norm_call(xl, sel_lat, pre=(g[0:1], tab[0], tab[1]), tm=tm_norm)
            _, h_c = _norm_call(xc, sel_ctx, pre=(g[0:1], tab[0], tab[1]), tm=tm_norm)

        wi = w_in[layer]
        w_a = jnp.concatenate([wi[:, o_nq:o_dq], wi[:, o_dv:o_z]], axis=1).astype(BF16)
        w_b = wi[:, o_dq:o_dv].astype(BF16)
        w_c = jnp.pad(wi[:, o_z:o_end], ((0, 0), (0, SSD_IN_W - (o_end - o_z)))).astype(BF16)
        a_lat = _mm_call(h_l, w_a, BF16, colscale=scale_a, name="proj_attn")
        a_ctx = _mm_call(h_c, w_a, BF16, colscale=scale_a, name="proj_attn")
        qk_lat = _mm_call(h_l, w_b, BF16, colscale=scale_b, rope=(cos_t, sin_t), name="proj_rope")
        qk_ctx = _mm_call(h_c, w_b, BF16, colscale=scale_b, name="proj_qk_ctx")
        s_lat = _mm_call(h_l, w_c, F32, name="proj_ssd")
        s_ctx = _mm_call(h_c, w_c, F32, name="proj_ssd")

        lam_p = da_lambda[layer]
        g_sub = (da_sub_g[layer] * (1.0 - lam_init)).reshape(1, LANES)
        na_l = _na_call(a_lat, a_ctx, _na_bias(na_rpb[layer]), batch, seq, n_ctx)
        da_l = _da_call(qk_lat, qk_ctx, a_lat, a_ctx, lam_p, g_sub, lam_init, batch, seq, n_ctx)

        ux = _ssd_prep_call(s_lat, s_ctx, ssd_conv_w[layer], ssd_conv_b[layer], batch, seq, n_ctx)
        dt0 = SSD_INNER + SSD_XBC
        dt_cols = lambda s, n: s[:, dt0:dt0 + 2 * SSD_HEADS].reshape(batch, n, 2 * SSD_HEADS)
        dtt = jnp.transpose(jnp.concatenate([dt_cols(s_lat, seq), dt_cols(s_ctx, n_ctx)], axis=1), (0, 2, 1))
        ssd_o = _ssd_call(ux, s_lat, s_ctx, dtt, ssd_dt_bias[layer], ssd_a_log[layer], ssd_d[layer],
                          ssd_norm_g[layer], batch, seq, n_ctx)

        wo = w_out[layer].astype(BF16)
        wq = peer_w_q[layer].astype(BF16)
        keys = peer_sub_keys[layer].reshape(2 * PEER_HEADS, PEER_N_KEYS, PEER_QDIM // 2).astype(BF16)
        nxt = layer + 1 < depth

        lat_tiles = seq // TOKEN_TILE
        sel_lat_t = lambda i: i // lat_tiles
        res_mix, pre_ffn, res_ffn = (tab[2], g[1:2]), (g[2:3], tab[3], tab[4]), (tab[5], g[3:4])
        pre_n = (norm_g[layer + 1][0:1], mods[layer + 1][:, 0], mods[layer + 1][:, 1]) if nxt else None
        xl, h2_l = _wout_call(na_l, da_l, ssd_o, wo, xl, sel_lat_t, res_mix, pre_ffn,
                              lambda i: (i // lat_tiles, i % lat_tiles), TOKEN_TILE)
        xl, h_l = _peer(h2_l, wq, keys, pu, pvt, layer, xl, sel_lat_t, res_ffn, pre_n)

        if need_ctx:
            na_c, da_c = _ctx_attn_call(a_ctx, qk_ctx, lam_p, g_sub, lam_init, batch, n_ctx)
            xc, h2_c = _wout_call(na_c, da_c, ssd_o, wo, xc, sel_ctx, res_mix, pre_ffn,
                                  lambda i: (i, seq // n_ctx), n_ctx)
            xc, h_c = _peer(h2_c, wq, keys, pu, pvt, layer, xc, sel_ctx, res_ffn, pre_n)
    return xl.reshape(batch, seq, d)
```
